```python
import jax, jax.numpy as jnp
from jax import lax
import numpy as np

D_MODEL = 2048
BATCH = 2
SEQ = 8192
DEPTH = 4

HEAD_DIM = 128
D_MIX = D_MODEL
POOL_WIDTH = D_MIX // 4
N_POOL_GROUPS = 4
POOL_GROUP_DIM = POOL_WIDTH // N_POOL_GROUPS
POOL_WINDOWS = (2, 4, 8, 16)
ATTN_WIDTH = D_MIX // 2
N_ATTN_HEADS = ATTN_WIDTH // HEAD_DIM
LRU_WIDTH = D_MIX - POOL_WIDTH - ATTN_WIDTH
N_LRU_BLOCKS = 4
LRU_BLOCK_DIM = LRU_WIDTH // N_LRU_BLOCKS
LRU_CONV_WIDTH = 4
LRU_C = 8.0
D_FF = ((8 * D_MODEL // 3 + 255) // 256) * 256
FFN_CONV_WIDTH = 3
Q_BLOCK = 128
N_IN = POOL_WIDTH + 3 * ATTN_WIDTH + N_ATTN_HEADS + 2 * LRU_WIDTH
EPS = 1e-6

kernel_name = "hymba_style_pool_fox_rglru_convffn"


def rmsnorm(x, g):
    xf = x.astype(jnp.float32)
    y = xf * lax.rsqrt(jnp.mean(xf * xf, axis=-1, keepdims=True) + EPS)
    return (y * g.astype(jnp.float32)).astype(x.dtype)


def causal_dwconv(u, w, b):
    K = w.shape[0]
    S = u.shape[1]
    up = jnp.pad(u, ((0, 0), (K - 1, 0), (0, 0)))
    out = b + up[:, 0:S] * w[0]
    for k in range(1, K):
        out = out + up[:, k:k + S] * w[k]
    return out


def pool_mixer(u, w, scale):
    B, S, _ = u.shape
    uf = u.astype(jnp.float32).reshape(B, S, N_POOL_GROUPS, POOL_GROUP_DIM)
    cs = jnp.cumsum(uf, axis=1)
    pos = jnp.arange(1, S + 1, dtype=jnp.float32)
    outs = []
    for g, win in enumerate(POOL_WINDOWS):
        csg = cs[:, :, g]
        lag = jnp.pad(csg, ((0, 0), (win, 0), (0, 0)))[:, :S]
        mean = (csg - lag) / jnp.minimum(pos, float(win))[None, :, None]
        outs.append(mean - uf[:, :, g])
    d = jnp.stack(outs, axis=2).astype(u.dtype)
    y = jnp.einsum('bsgc,gcd->bsgd', d, w).reshape(B, S, POOL_WIDTH)
    return y * scale


def forgetting_attention(q, k, v, f_logit, b_f):
    B, S, _ = q.shape
    H = N_ATTN_HEADS
    q = q.reshape(B, S, H, HEAD_DIM).transpose(0, 2, 1, 3)
    k = k.reshape(B, S, H, HEAD_DIM).transpose(0, 2, 1, 3)
    v = v.reshape(B, S, H, HEAD_DIM).transpose(0, 2, 1, 3)
    log_f = jax.nn.log_sigmoid(f_logit.astype(jnp.float32) + b_f.astype(jnp.float32))
    F = jnp.cumsum(log_f, axis=1).transpose(0, 2, 1)
    nb = S // Q_BLOCK
    qb = q.reshape(B, H, nb, Q_BLOCK, HEAD_DIM).transpose(2, 0, 1, 3, 4)
    Fb = F.reshape(B, H, nb, Q_BLOCK).transpose(2, 0, 1, 3)
    kpos = jnp.arange(S)
    scale = HEAD_DIM ** -0.5

    def block(args):
        q_blk, F_blk, i = args
        qpos = i * Q_BLOCK + jnp.arange(Q_BLOCK)
        s = (jnp.einsum('bhqd,bhkd->bhqk', q_blk, k).astype(jnp.float32) * scale
             + F_blk[..., None] - F[:, :, None, :])
        s = jnp.where(kpos[None, :] <= qpos[:, None], s, -jnp.inf)
        p = jax.nn.softmax(s, axis=-1).astype(v.dtype)
        return jnp.einsum('bhqk,bhkd->bhqd', p, v)

    o = lax.map(block, (qb, Fb, jnp.arange(nb)))
    return o.transpose(1, 0, 3, 2, 4).reshape(B, S, H * HEAD_DIM)


def rg_lru_branch(xb, yb, conv_w, conv_b, wa, ba, wi, bi, lam):
    B, S, _ = xb.shape
    xc = causal_dwconv(xb, conv_w, conv_b)
    xh = xc.reshape(B, S, N_LRU_BLOCKS, LRU_BLOCK_DIM)
    gate_r = jax.nn.sigmoid((jnp.einsum('bshc,hcd->bshd', xh, wa).reshape(B, S, LRU_WIDTH) + ba).astype(jnp.float32))
    gate_i = jax.nn.sigmoid((jnp.einsum('bshc,hcd->bshd', xh, wi).reshape(B, S, LRU_WIDTH) + bi).astype(jnp.float32))
    log_a = -LRU_C * gate_r * jax.nn.softplus(-lam.astype(jnp.float32))
    a = jnp.exp(log_a)
    inp = jnp.sqrt(-jnp.expm1(2.0 * log_a)) * (gate_i * xc.astype(jnp.float32))

    def combine(left, right):
        a1, b1 = left
        a2, b2 = right
        return a1 * a2, a2 * b1 + b2

    _, h = lax.associative_scan(combine, (a, inp), axis=1)
    return (h * jax.nn.gelu(yb.astype(jnp.float32))).astype(xb.dtype)


def hybrid_mixer(h, w_in, b_f, pool_w, pool_scale, lru_conv_w, lru_conv_b,
                 lru_wa, lru_ba, lru_wi, lru_bi, lru_lambda, w_out):
    z = h @ w_in
    sizes = [POOL_WIDTH, ATTN_WIDTH, ATTN_WIDTH, ATTN_WIDTH, N_ATTN_HEADS, LRU_WIDTH]
    offsets = [int(o) for o in np.cumsum(sizes)]
    zp, zq, zk, zv, zf, zx, zy = jnp.split(z, offsets, axis=-1)
    y_pool = pool_mixer(zp, pool_w, pool_scale)
    y_attn = forgetting_attention(zq, zk, zv, zf, b_f)
    y_lru = rg_lru_branch(zx, zy, lru_conv_w, lru_conv_b, lru_wa, lru_ba, lru_wi, lru_bi, lru_lambda)
    y = jnp.concatenate([y_pool, y_attn.astype(h.dtype), y_lru], axis=-1)
    return y @ w_out


def conv_glu_ffn(h, w_gate, w_up, conv_w, conv_b, w_down):
    g = causal_dwconv(h @ w_gate, conv_w, conv_b)
    return (jax.nn.silu(g) * (h @ w_up)) @ w_down


def setup_inputs(seed: int = 0) -> dict:
    key = jax.random.key(seed)
    ks = jax.random.split(key, 26)
    f32 = jnp.float32

    def nrm(k, shape, s):
        return jax.random.normal(k, shape, f32) * s

    u_lam = jax.random.uniform(ks[15], (DEPTH, LRU_WIDTH), f32, 0.9, 0.999)
    s_lam = u_lam ** (1.0 / LRU_C)
    lru_lambda = jnp.log(s_lam) - jnp.log1p(-s_lam)
    return {
        "x": nrm(ks[0], (BATCH, SEQ, D_MODEL), 1.0),
        "c": nrm(ks[1], (BATCH, D_MODEL), 1.0),
        "w_ada": nrm(ks[2], (DEPTH, D_MODEL, 6 * D_MODEL), 0.5 * D_MODEL ** -0.5),
        "b_ada": nrm(ks[3], (DEPTH, 6 * D_MODEL), 0.01),
        "g_mix": 1.0 + nrm(ks[4], (DEPTH, D_MODEL), 0.05),
        "w_in": nrm(ks[5], (DEPTH, D_MODEL, N_IN), D_MODEL ** -0.5),
        "b_f": jax.random.uniform(ks[6], (DEPTH, N_ATTN_HEADS), f32, 1.0, 5.0),
        "pool_w": nrm(ks[7], (DEPTH, N_POOL_GROUPS, POOL_GROUP_DIM, POOL_GROUP_DIM), POOL_GROUP_DIM ** -0.5),
        "pool_scale": 1.0 + nrm(ks[8], (DEPTH, POOL_WIDTH), 0.1),
        "lru_conv_w": nrm(ks[9], (DEPTH, LRU_CONV_WIDTH, LRU_WIDTH), LRU_CONV_WIDTH ** -0.5),
        "lru_conv_b": nrm(ks[10], (DEPTH, LRU_WIDTH), 0.01),
        "lru_wa": nrm(ks[11], (DEPTH, N_LRU_BLOCKS, LRU_BLOCK_DIM, LRU_BLOCK_DIM), LRU_BLOCK_DIM ** -0.5),
        "lru_ba": nrm(ks[12], (DEPTH, LRU_WIDTH), 0.01),
        "lru_wi": nrm(ks[13], (DEPTH, N_LRU_BLOCKS, LRU_BLOCK_DIM, LRU_BLOCK_DIM), LRU_BLOCK_DIM ** -0.5),
        "lru_bi": nrm(ks[14], (DEPTH, LRU_WIDTH), 0.01),
        "lru_lambda": lru_lambda,
        "w_out": nrm(ks[16], (DEPTH, D_MIX, D_MODEL), D_MIX ** -0.5),
        "g_ffn": 1.0 + nrm(ks[17], (DEPTH, D_MODEL), 0.05),
        "w_ffn_gate": nrm(ks[18], (DEPTH, D_MODEL, D_FF), D_MODEL ** -0.5),
        "w_ffn_up": nrm(ks[19], (DEPTH, D_MODEL, D_FF), D_MODEL ** -0.5),
        "ffn_conv_w": nrm(ks[20], (DEPTH, FFN_CONV_WIDTH, D_FF), FFN_CONV_WIDTH ** -0.5),
        "ffn_conv_b": nrm(ks[21], (DEPTH, D_FF), 0.01),
        "w_ffn_down": nrm(ks[22], (DEPTH, D_FF, D_MODEL), D_FF ** -0.5),
        "final_g": 1.0 + nrm(ks[23], (D_MODEL,), 0.05),
    }


def reference(x, c, w_ada, b_ada, g_mix, w_in, b_f, pool_w, pool_scale, lru_conv_w, lru_conv_b,
              lru_wa, lru_ba, lru_wi, lru_bi, lru_lambda, w_out, g_ffn, w_ffn_gate, w_ffn_up,
              ffn_conv_w, ffn_conv_b, w_ffn_down, final_g):
    c_act = jax.nn.silu(c)
    for l in range(DEPTH):
        mod = c_act @ w_ada[l] + b_ada[l]
        sh1, sc1, gt1, sh2, sc2, gt2 = jnp.split(mod[:, None, :], 6, axis=-1)
        h = rmsnorm(x, g_mix[l]) * (1.0 + sc1) + sh1
        x = x + gt1 * hybrid_mixer(h, w_in[l], b_f[l], pool_w[l], pool_scale[l], lru_conv_w[l], lru_conv_b[l],
                                   lru_wa[l], lru_ba[l], lru_wi[l], lru_bi[l], lru_lambda[l], w_out[l])
        h = rmsnorm(x, g_ffn[l]) * (1.0 + sc2) + sh2
        x = x + gt2 * conv_glu_ffn(h, w_ffn_gate[l], w_ffn_up[l], ffn_conv_w[l], ffn_conv_b[l], w_ffn_down[l])
    return rmsnorm(x, final_g)
```

```python
import functools

import jax
import jax.numpy as jnp
from jax import lax
from jax.experimental import pallas as pl
from jax.experimental.pallas import tpu as pltpu

F32 = jnp.float32
BF16 = jnp.bfloat16

HEAD_DIM = 128
N_POOL_GROUPS = 4
POOL_WINDOWS = (2, 4, 8, 16)
POOL_HALO = 16
N_LRU_BLOCKS = 4
LRU_CONV_WIDTH = 4
LRU_C = 8.0
FFN_CONV_WIDTH = 3
EPS = 1e-6

LANES = 128
SUBLANES = 8
VMEM_LIMIT_CAP_V7X = 56 * 2**20
MASK_VALUE = -1e30


def _cparams(semantics, vmem_bytes):
    return pltpu.CompilerParams(dimension_semantics=semantics,
                                vmem_limit_bytes=int(min(VMEM_LIMIT_CAP_V7X, max(vmem_bytes, 16 * 2**20))))


def _sigmoid(x):
    return 1.0 / (1.0 + jnp.exp(-x))


def _norm_mod(x, g, sc, sh):
    ms = jnp.mean(x * x, axis=-1, keepdims=True)
    return (x * lax.rsqrt(ms + EPS) * g) * (1.0 + sc) + sh


def _split3_bf16(x):
    hi = x.astype(BF16)
    r1 = x - hi.astype(F32)
    mid = r1.astype(BF16)
    lo = (r1 - mid.astype(F32)).astype(BF16)
    return hi, mid, lo


def _ada_kernel(c_ref, w_ref, b_ref, o_ref):
    nb = c_ref.shape[0]
    tn = w_ref.shape[3]
    o_ref[...] = jnp.zeros_like(o_ref)
    ca = [c_ref[b] * _sigmoid(c_ref[b]) for b in range(nb)]
    for jc in range(tn // LANES):
        sl = slice(jc * LANES, (jc + 1) * LANES)
        w = w_ref[0, :, :, sl]
        for b in range(nb):
            part = jnp.sum(w * ca[b], axis=0)
            o_ref[0, b:b + 1, sl] = jnp.sum(part, axis=0, keepdims=True) + b_ref[0, :, sl]


def _ada_mod(c, w_ada, b_ada):
    depth, d, n = w_ada.shape
    batch = c.shape[0]
    tn = 512
    dr = d // SUBLANES
    c_lanes = jnp.broadcast_to(c[:, :, None], (batch, d, LANES)).reshape(batch, dr, SUBLANES, LANES)
    return pl.pallas_call(
        _ada_kernel,
        grid=(depth, n // tn),
        in_specs=[pl.BlockSpec((batch, dr, SUBLANES, LANES), lambda l, j: (0, 0, 0, 0)),
                  pl.BlockSpec((1, dr, SUBLANES, tn), lambda l, j: (l, 0, 0, j)),
                  pl.BlockSpec((1, 1, tn), lambda l, j: (l, 0, j))],
        out_specs=pl.BlockSpec((1, SUBLANES, tn), lambda l, j: (l, 0, j)),
        out_shape=jax.ShapeDtypeStruct((depth, SUBLANES, n), F32),
        compiler_params=_cparams(("parallel", "parallel"), 4 * d * tn * 4 + 8 * batch * d * LANES * 4),
        name="ada_mod",
    )(c_lanes, w_ada.reshape(depth, dr, SUBLANES, n), b_ada.reshape(depth, 1, n))


def _norm_kernel(x_ref, g_ref, sc_ref, sh_ref, h_ref):
    h_ref[...] = _norm_mod(x_ref[...], g_ref[...], sc_ref[0], sh_ref[0]).astype(h_ref.dtype)


def _norm(x2, g, sc, sh, seq):
    t, d = x2.shape
    tm = min(512, seq)
    tpb = seq // tm
    return pl.pallas_call(
        _norm_kernel,
        grid=(t // tm,),
        in_specs=[pl.BlockSpec((tm, d), lambda i: (i, 0)),
                  pl.BlockSpec((1, d), lambda i: (0, 0)),
                  pl.BlockSpec((1, 1, d), lambda i: (i // tpb, 0, 0)),
                  pl.BlockSpec((1, 1, d), lambda i: (i // tpb, 0, 0))],
        out_specs=pl.BlockSpec((tm, d), lambda i: (i, 0)),
        out_shape=jax.ShapeDtypeStruct((t, d), BF16),
        compiler_params=_cparams(("parallel",), 6 * tm * d * 4),
        name="norm_mod",
    )(x2, g, sc, sh)


def _mm_kernel(a_ref, w_ref, o_ref):
    o_ref[...] = jnp.dot(a_ref[...], w_ref[0], preferred_element_type=F32).astype(o_ref.dtype)


def _mm_scaled_kernel(a_ref, w_ref, cs_ref, o_ref):
    acc = jnp.dot(a_ref[...], w_ref[0], preferred_element_type=F32)
    o_ref[...] = (acc * cs_ref[...]).astype(o_ref.dtype)


def _matmul(a, w_stack, layer, out_dtype, tn, col_scale=None):
    t, k = a.shape
    n = w_stack.shape[2]
    tm = min(1024, t)
    in_specs = [pl.BlockSpec((tm, k), lambda i, j: (i, 0)),
                pl.BlockSpec((1, k, tn), lambda i, j: (layer, 0, j))]
    args = [a, w_stack]
    body = _mm_kernel
    if col_scale is not None:
        in_specs.append(pl.BlockSpec((1, tn), lambda i, j: (0, j)))
        args.append(col_scale)
        body = _mm_scaled_kernel
    vmem = 2 * (tm * k * 2 + k * tn * 2 + tm * tn * 4) + 2 * tm * tn * 4
    return pl.pallas_call(
        body,
        grid=(t // tm, n // tn),
        in_specs=in_specs,
        out_specs=pl.BlockSpec((tm, tn), lambda i, j: (i, j)),
        out_shape=jax.ShapeDtypeStruct((t, n), out_dtype),
        compiler_params=_cparams(("parallel", "parallel"), vmem),
        name="matmul_dmodel",
    )(*args)


def _fcum_kernel(z_ref, b_ref, o_ref, carry_ref):
    ts = z_ref.shape[1]

    @pl.when(pl.program_id(1) == 0)
    def _():
        carry_ref[...] = jnp.zeros_like(carry_ref)

    x = z_ref[0] + b_ref[...]
    lf = jnp.minimum(x, 0.0) - jnp.log(1.0 + jnp.exp(-jnp.abs(x)))
    r = lax.broadcasted_iota(jnp.int32, (ts, ts), 0)
    c = lax.broadcasted_iota(jnp.int32, (ts, ts), 1)
    tri = jnp.where(c <= r, 1.0, 0.0).astype(BF16)
    cs = carry_ref[...]
    for part in _split3_bf16(lf):
        cs = cs + jnp.dot(tri, part, preferred_element_type=F32)
    o_ref[0] = cs
    carry_ref[...] = cs[ts - 1:ts, :]


def _forget_cumsum(zf, bf_pad):
    b, s, w = zf.shape
    ts = min(512, s)
    return pl.pallas_call(
        _fcum_kernel,
        grid=(b, s // ts),
        in_specs=[pl.BlockSpec((1, ts, w), lambda bi, i: (bi, i, 0)),
                  pl.BlockSpec((1, w), lambda bi, i: (0, 0))],
        out_specs=pl.BlockSpec((1, ts, w), lambda bi, i: (bi, i, 0)),
        out_shape=jax.ShapeDtypeStruct((b, s, w), F32),
        scratch_shapes=[pltpu.VMEM((1, w), F32)],
        compiler_params=_cparams(("arbitrary", "arbitrary"), 8 * ts * ts * 4),
        name="forget_cumsum",
    )(zf, bf_pad)


def _attn_kernel(q_ref, k_ref, v_ref, fk_ref, o_ref):
    tq = q_ref.shape[1]
    tk = tq
    qi = pl.program_id(2)
    q = q_ref[0]

    def chunk(kc, carry, masked):
        m, l, acc = carry
        k0 = pl.multiple_of(kc * tk, tk)
        k = k_ref[0, pl.ds(k0, tk), :]
        v = v_ref[0, pl.ds(k0, tk), :]
        s = lax.dot_general(q, k, (((1,), (1,)), ((), ())), preferred_element_type=F32)
        s = s - fk_ref[0, :, pl.ds(k0, tk)]
        if masked:
            row = lax.broadcasted_iota(jnp.int32, (tq, tk), 0)
            col = lax.broadcasted_iota(jnp.int32, (tq, tk), 1)
            s = jnp.where(col <= row, s, MASK_VALUE)
        m_new = jnp.maximum(m, jnp.max(s, axis=-1, keepdims=True))
        alpha = jnp.exp(m - m_new)
        p = jnp.exp(s - m_new)
        l = alpha * l + jnp.sum(p, axis=-1, keepdims=True)
        acc = alpha * acc + jnp.dot(p.astype(BF16), v, preferred_element_type=F32)
        return m_new, l, acc

    init = (jnp.full((tq, 1), MASK_VALUE, F32), jnp.zeros((tq, 1), F32), jnp.zeros((tq, HEAD_DIM), F32))
    carry = lax.fori_loop(0, qi, lambda kc, c: chunk(kc, c, False), init)
    _, l, acc = chunk(qi, carry, True)
    o_ref[0] = (acc / l).astype(o_ref.dtype)


def _attention(zqkv, fk_rows, n_heads):
    b, s, _ = zqkv.shape
    tq = min(512, s)
    vmem = 2 * (2 * s * HEAD_DIM * 2) + 8 * tq * tq * 4 + 4 * s * 4 * 8
    return pl.pallas_call(
        _attn_kernel,
        grid=(b, n_heads, s // tq),
        in_specs=[pl.BlockSpec((1, tq, HEAD_DIM), lambda bi, h, i: (bi, i, h)),
                  pl.BlockSpec((1, s, HEAD_DIM), lambda bi, h, i: (bi, 0, n_heads + h)),
                  pl.BlockSpec((1, s, HEAD_DIM), lambda bi, h, i: (bi, 0, 2 * n_heads + h)),
                  pl.BlockSpec((1, 1, s), lambda bi, h, i: (bi * n_heads + h, 0, 0))],
        out_specs=pl.BlockSpec((1, tq, HEAD_DIM), lambda bi, h, i: (bi, i, h)),
        out_shape=jax.ShapeDtypeStruct((b, s, n_heads * HEAD_DIM), BF16),
        compiler_params=_cparams(("parallel", "parallel", "parallel"), vmem),
        name="fox_attention",
    )(zqkv, zqkv, zqkv, fk_rows)


def _pool_kernel(u_ref, halo_ref, w_ref, sc_ref, o_ref):
    tm = u_ref.shape[1]
    i = pl.program_id(1)
    halo = jnp.where(i > 0, halo_ref[0], 0.0)
    ext = jnp.concatenate([halo, u_ref[0]], axis=0)
    pos = (i * tm + 1 + lax.broadcasted_iota(jnp.int32, (tm, 1), 0)).astype(F32)
    gd = LANES
    for g, win in enumerate(POOL_WINDOWS):
        e = ext[:, g * gd:(g + 1) * gd]
        ssum = e
        shift = 1
        while shift < win:
            ssum = ssum + pltpu.roll(ssum, shift, 0)
            shift *= 2
        mean = ssum[POOL_HALO:] * (1.0 / jnp.minimum(pos, float(win)))
        dlt = mean - e[POOL_HALO:]
        y = jnp.dot(dlt.astype(BF16), w_ref[0, g], preferred_element_type=F32)
        o_ref[0, :, g * gd:(g + 1) * gd] = (y * sc_ref[:, g * gd:(g + 1) * gd]).astype(o_ref.dtype)


def _pool_mixer(z32, pool_w_stack, layer, scale_row, width):
    b, s, _ = z32.shape
    tm = min(512, s)
    hb = tm // POOL_HALO
    return pl.pallas_call(
        _pool_kernel,
        grid=(b, s // tm),
        in_specs=[pl.BlockSpec((1, tm, width), lambda bi, i: (bi, i, 0)),
                  pl.BlockSpec((1, POOL_HALO, width), lambda bi, i: (bi, jnp.maximum(i * hb - 1, 0), 0)),
                  pl.BlockSpec((1, N_POOL_GROUPS, LANES, LANES), lambda bi, i: (layer, 0, 0, 0)),
                  pl.BlockSpec((1, width), lambda bi, i: (0, 0))],
        out_specs=pl.BlockSpec((1, tm, width), lambda bi, i: (bi, i, 0)),
        out_shape=jax.ShapeDtypeStruct((b, s, width), BF16),
        compiler_params=_cparams(("parallel", "parallel"), 16 * tm * width * 4),
        name="pool_mixer",
    )(z32, z32, pool_w_stack, scale_row)


def _lru_kernel(x_ref, halo_ref, y_ref, cw_ref, cb_ref, wa_ref, ba_ref, wi_ref, bi_ref, lam_ref,
                o_ref, h_ref, a_s, b_s, h_s):
    tm = x_ref.shape[1]
    i = pl.program_id(1)

    @pl.when(i == 0)
    def _():
        h_ref[...] = jnp.zeros_like(h_ref)

    halo = jnp.where(i > 0, halo_ref[0], 0.0)
    ext = jnp.concatenate([halo, x_ref[0]], axis=0)
    xc = cb_ref[...] + cw_ref[LRU_CONV_WIDTH - 1:LRU_CONV_WIDTH, :] * ext[SUBLANES:]
    for k in range(LRU_CONV_WIDTH - 1):
        shifted = pltpu.roll(ext, LRU_CONV_WIDTH - 1 - k, 0)[SUBLANES:]
        xc = xc + cw_ref[k:k + 1, :] * shifted

    lam = lam_ref[...]
    neg_softplus = -(jnp.maximum(-lam, 0.0) + jnp.log(1.0 + jnp.exp(-jnp.abs(lam))))
    gd = LANES
    for blk in range(N_LRU_BLOCKS):
        sl = slice(blk * gd, (blk + 1) * gd)
        xb = xc[:, sl]
        xb16 = xb.astype(BF16)
        gate_r = _sigmoid(jnp.dot(xb16, wa_ref[0, blk], preferred_element_type=F32) + ba_ref[:, sl])
        gate_i = _sigmoid(jnp.dot(xb16, wi_ref[0, blk], preferred_element_type=F32) + bi_ref[:, sl])
        log_a = LRU_C * gate_r * neg_softplus[:, sl]
        a_s[:, sl] = jnp.exp(log_a)
        b_s[:, sl] = jnp.sqrt(1.0 - jnp.exp(2.0 * log_a)) * (gate_i * xb)

    def group(gi, h):
        r0 = pl.multiple_of(gi * SUBLANES, SUBLANES)
        a8 = a_s[pl.ds(r0, SUBLANES), :]
        b8 = b_s[pl.ds(r0, SUBLANES), :]
        rows = []
        for r in range(SUBLANES):
            h = a8[r:r + 1, :] * h + b8[r:r + 1, :]
            rows.append(h)
        h_s[pl.ds(r0, SUBLANES), :] = jnp.concatenate(rows, axis=0)
        return h

    h_ref[...] = lax.fori_loop(0, tm // SUBLANES, group, h_ref[...])

    y = y_ref[0]
    gelu = 0.5 * y * (1.0 + jnp.tanh(0.7978845608028654 * (y + 0.044715 * (y * y * y))))
    o_ref[0] = (h_s[...] * gelu).astype(o_ref.dtype)


def _lru_mixer(z32, layer, p, width, x_col, y_col):
    b, s, _ = z32.shape
    tm = min(512, s)
    hb = tm // SUBLANES
    row = lambda bi, i: (0, 0)
    return pl.pallas_call(
        _lru_kernel,
        grid=(b, s // tm),
        in_specs=[pl.BlockSpec((1, tm, width), lambda bi, i: (bi, i, x_col)),
                  pl.BlockSpec((1, SUBLANES, width), lambda bi, i: (bi, jnp.maximum(i * hb - 1, 0), x_col)),
                  pl.BlockSpec((1, tm, width), lambda bi, i: (bi, i, y_col)),
                  pl.BlockSpec((LRU_CONV_WIDTH, width), row),
                  pl.BlockSpec((1, width), row),
                  pl.BlockSpec((1, N_LRU_BLOCKS, LANES, LANES), lambda bi, i: (layer, 0, 0, 0)),
                  pl.BlockSpec((1, width), row),
                  pl.BlockSpec((1, N_LRU_BLOCKS, LANES, LANES), lambda bi, i: (layer, 0, 0, 0)),
                  pl.BlockSpec((1, width), row),
                  pl.BlockSpec((1, width), row)],
        out_specs=pl.BlockSpec((1, tm, width), lambda bi, i: (bi, i, 0)),
        out_shape=jax.ShapeDtypeStruct((b, s, width), BF16),
        scratch_shapes=[pltpu.VMEM((1, width), F32), pltpu.VMEM((tm, width), F32),
                        pltpu.VMEM((tm, width), F32), pltpu.VMEM((tm, width), F32)],
        compiler_params=_cparams(("arbitrary", "arbitrary"), 24 * tm * width * 4),
        name="rg_lru",
    )(z32, z32, z32, p["conv_w"], p["conv_b"], p["wa"], p["ba"], p["wi"], p["bi"], p["lam"])


def _residual_norm(acc, x_ref, gt_ref, g_ref, sc_ref, sh_ref, out_refs, emit_x):
    xn = x_ref[...] + gt_ref[0] * acc
    if emit_x:
        out_refs[0][...] = xn
    h_ref = out_refs[-1]
    h_ref[...] = _norm_mod(xn, g_ref[...], sc_ref[0], sh_ref[0]).astype(h_ref.dtype)


def _mix_out_kernel(yp_ref, ya_ref, yl_ref, w_ref, x_ref, gt_ref, g_ref, sc_ref, sh_ref, *out_refs):
    kp = yp_ref.shape[1]
    ka = ya_ref.shape[1]
    acc = jnp.dot(yp_ref[...], w_ref[0, 0:kp, :], preferred_element_type=F32)
    acc = acc + jnp.dot(ya_ref[...], w_ref[0, kp:kp + ka, :], preferred_element_type=F32)
    acc = acc + jnp.dot(yl_ref[...], w_ref[0, kp + ka:, :], preferred_element_type=F32)
    _residual_norm(acc, x_ref, gt_ref, g_ref, sc_ref, sh_ref, out_refs, True)


def _mix_out(yp, ya, yl, w_stack, layer, x2, gt, g, sc, sh, seq):
    t, d = x2.shape
    tm = min(512, seq)
    tpb = seq // tm
    per_b = lambda i: (i // tpb, 0, 0)
    lhs = lambda y: pl.BlockSpec((tm, y.shape[1]), lambda i: (i, 0))
    vmem = 2 * d * d * 2 + 2 * (tm * d * 2 + 3 * tm * d * 4) + 3 * tm * d * 4
    return pl.pallas_call(
        _mix_out_kernel,
        grid=(t // tm,),
        in_specs=[lhs(yp), lhs(ya), lhs(yl),
                  pl.BlockSpec((1, d, d), lambda i: (layer, 0, 0)),
                  pl.BlockSpec((tm, d), lambda i: (i, 0)),
                  pl.BlockSpec((1, 1, d), per_b),
                  pl.BlockSpec((1, d), lambda i: (0, 0)),
                  pl.BlockSpec((1, 1, d), per_b),
                  pl.BlockSpec((1, 1, d), per_b)],
        out_specs=[pl.BlockSpec((tm, d), lambda i: (i, 0)), pl.BlockSpec((tm, d), lambda i: (i, 0))],
        out_shape=[jax.ShapeDtypeStruct((t, d), F32), jax.ShapeDtypeStruct((t, d), BF16)],
        compiler_params=_cparams(("parallel",), vmem),
        name="mix_out_proj",
    )(yp, ya, yl, w_stack, x2, gt, g, sc, sh)


def _ffn_down_kernel(a_ref, w_ref, x_ref, gt_ref, g_ref, sc_ref, sh_ref, *refs, emit_x):
    out_refs, acc_ref = refs[:-1], refs[-1]
    k = pl.program_id(1)

    @pl.when(k == 0)
    def _():
        acc_ref[...] = jnp.zeros_like(acc_ref)

    acc_ref[...] += jnp.dot(a_ref[...], w_ref[0], preferred_element_type=F32)

    @pl.when(k == pl.num_programs(1) - 1)
    def _():
        _residual_norm(acc_ref[...], x_ref, gt_ref, g_ref, sc_ref, sh_ref, out_refs, emit_x)


def _ffn_down(act, w_stack, layer, x2, gt, g, sc, sh, seq, emit_x, h_dtype):
    t, d = x2.shape
    kf = act.shape[1]
    tm = min(512, seq)
    tk = 512
    tpb = seq // tm
    per_b = lambda i, k: (i // tpb, 0, 0)
    row_tile = pl.BlockSpec((tm, d), lambda i, k: (i, 0))
    out_specs = [row_tile]
    out_shape = [jax.ShapeDtypeStruct((t, d), h_dtype)]
    if emit_x:
        out_specs = [row_tile, row_tile]
        out_shape = [jax.ShapeDtypeStruct((t, d), F32)] + out_shape
    vmem = 2 * (tm * tk * 2 + tk * d * 2 + 3 * tm * d * 4) + 3 * tm * d * 4
    return pl.pallas_call(
        functools.partial(_ffn_down_kernel, emit_x=emit_x),
        grid=(t // tm, kf // tk),
        in_specs=[pl.BlockSpec((tm, tk), lambda i, k: (i, k)),
                  pl.BlockSpec((1, tk, d), lambda i, k: (layer, k, 0)),
                  row_tile,
                  pl.BlockSpec((1, 1, d), per_b),
                  pl.BlockSpec((1, d), lambda i, k: (0, 0)),
                  pl.BlockSpec((1, 1, d), per_b),
                  pl.BlockSpec((1, 1, d), per_b)],
        out_specs=out_specs,
        out_shape=out_shape,
        scratch_shapes=[pltpu.VMEM((tm, d), F32)],
        compiler_params=_cparams(("parallel", "arbitrary"), vmem),
        name="ffn_down_proj",
    )(act, w_stack, x2, gt, g, sc, sh)


def _ffn_gate_kernel(h_ref, wg_ref, wu_ref, cw_ref, cb_ref, o_ref, carry_ref, *, tiles_per_batch):
    tm = h_ref.shape[0]
    i = pl.program_id(0)
    j = pl.program_id(1)

    @pl.when(i % tiles_per_batch == 0)
    def _():
        carry_ref[j] = jnp.zeros(carry_ref.shape[1:], F32)

    h = h_ref[...]
    u = jnp.dot(h, wg_ref[0], preferred_element_type=F32)
    ext = jnp.concatenate([carry_ref[j], u], axis=0)
    carry_ref[j] = u[tm - SUBLANES:, :]
    g = cb_ref[...] + cw_ref[FFN_CONV_WIDTH - 1:FFN_CONV_WIDTH, :] * u
    for k in range(FFN_CONV_WIDTH - 1):
        shifted = pltpu.roll(ext, FFN_CONV_WIDTH - 1 - k, 0)[SUBLANES:]
        g = g + cw_ref[k:k + 1, :] * shifted
    up = jnp.dot(h, wu_ref[0], preferred_element_type=F32)
    o_ref[...] = (g * _sigmoid(g) * up).astype(o_ref.dtype)


def _ffn_gate(h, wg_stack, wu_stack, layer, conv_w, conv_b, seq):
    t, d = h.shape
    kf = wg_stack.shape[2]
    tm = min(1024, seq)
    tf = 512
    wspec = pl.BlockSpec((1, d, tf), lambda i, j: (layer, 0, j))
    vmem = 2 * (tm * d * 2 + 2 * d * tf * 2 + tm * tf * 2) + 8 * tm * tf * 4
    return pl.pallas_call(
        functools.partial(_ffn_gate_kernel, tiles_per_batch=seq // tm),
        grid=(t // tm, kf // tf),
        in_specs=[pl.BlockSpec((tm, d), lambda i, j: (i, 0)), wspec, wspec,
                  pl.BlockSpec((FFN_CONV_WIDTH, tf), lambda i, j: (0, j)),
                  pl.BlockSpec((1, tf), lambda i, j: (0, j))],
        out_specs=pl.BlockSpec((tm, tf), lambda i, j: (i, j)),
        out_shape=jax.ShapeDtypeStruct((t, kf), BF16),
        scratch_shapes=[pltpu.VMEM((kf // tf, SUBLANES, tf), F32)],
        compiler_params=_cparams(("arbitrary", "arbitrary"), vmem),
        name="ffn_gate_up",
    )(h, wg_stack, wu_stack, conv_w, conv_b)


def kernel(x, c, w_ada, b_ada, g_mix, w_in, b_f, pool_w, pool_scale, lru_conv_w, lru_conv_b, lru_wa, lru_ba,
           lru_wi, lru_bi, lru_lambda, w_out, g_ffn, w_ffn_gate, w_ffn_up, ffn_conv_w, ffn_conv_b, w_ffn_down,
           final_g):
    batch, seq, d = x.shape
    depth = w_ada.shape[0]
    pool_width = pool_w.shape[1] * pool_w.shape[2]
    lru_width = lru_lambda.shape[1]
    n_heads = b_f.shape[1]
    attn_width = n_heads * HEAD_DIM
    assert w_in.shape[2] == pool_width + 3 * attn_width + n_heads + 2 * lru_width
    assert pool_width == lru_width == N_POOL_GROUPS * LANES and seq % SUBLANES == 0

    o_q = pool_width
    o_f = o_q + 3 * attn_width
    o_x = o_f + n_heads
    w_qkv = w_in[:, :, o_q:o_f].astype(BF16)
    w_pxy = jnp.concatenate([w_in[:, :, :o_q], w_in[:, :, o_x:]], axis=2).astype(BF16)
    w_f = jnp.pad(w_in[:, :, o_f:o_x], ((0, 0), (0, 0), (0, LANES - n_heads))).astype(BF16)
    bf_pad = jnp.pad(b_f, ((0, 0), (0, LANES - n_heads)))
    qkv_scale = jnp.concatenate([jnp.full((1, attn_width), HEAD_DIM ** -0.5, F32),
                                 jnp.ones((1, 2 * attn_width), F32)], axis=1)
    pool_w16 = pool_w.astype(BF16)
    lru_wa16 = lru_wa.astype(BF16)
    lru_wi16 = lru_wi.astype(BF16)
    w_out16 = w_out.astype(BF16)
    w_gate16 = w_ffn_gate.astype(BF16)
    w_up16 = w_ffn_up.astype(BF16)
    w_down16 = w_ffn_down.astype(BF16)

    mod = _ada_mod(c, w_ada, b_ada)[:, :batch]

    def mod_chunk(layer, idx):
        return mod[layer, :, idx * d:(idx + 1) * d].reshape(batch, 1, d)

    x2 = x.reshape(batch * seq, d)
    h = _norm(x2, g_mix[0][None], mod_chunk(0, 1), mod_chunk(0, 0), seq)
    out = None
    for layer in range(depth):
        sh1, sc1, gt1, sh2, sc2, gt2 = (mod_chunk(layer, idx) for idx in range(6))
        zqkv = _matmul(h, w_qkv, layer, BF16, 1024, qkv_scale).reshape(batch, seq, 3 * attn_width)
        z32 = _matmul(h, w_pxy, layer, F32, 512).reshape(batch, seq, pool_width + 2 * lru_width)
        zf = _matmul(h, w_f, layer, F32, LANES).reshape(batch, seq, LANES)

        fcum = _forget_cumsum(zf, bf_pad[layer][None])
        fk_rows = jnp.transpose(fcum[:, :, :n_heads], (0, 2, 1)).reshape(batch * n_heads, 1, seq)
        y_attn = _attention(zqkv, fk_rows, n_heads).reshape(batch * seq, attn_width)
        y_pool = _pool_mixer(z32, pool_w16, layer, pool_scale[layer][None], pool_width)
        lru_p = dict(conv_w=lru_conv_w[layer], conv_b=lru_conv_b[layer][None], wa=lru_wa16,
                     ba=lru_ba[layer][None], wi=lru_wi16, bi=lru_bi[layer][None], lam=lru_lambda[layer][None])
        y_lru = _lru_mixer(z32, layer, lru_p, lru_width, 1, 2)

        x2, h = _mix_out(y_pool.reshape(batch * seq, pool_width), y_attn, y_lru.reshape(batch * seq, lru_width),
                         w_out16, layer, x2, gt1, g_ffn[layer][None], sc2, sh2, seq)
        act = _ffn_gate(h, w_gate16, w_up16, layer, ffn_conv_w[layer], ffn_conv_b[layer][None], seq)
        if layer + 1 < depth:
            x2, h = _ffn_down(act, w_down16, layer, x2, gt2, g_mix[layer + 1][None], mod_chunk(layer + 1, 1),
                              mod_chunk(layer + 1, 0), seq, True, BF16)
        else:
            zeros = jnp.zeros((batch, 1, d), F32)
            (out,) = _ffn_down(act, w_down16, layer, x2, gt2, final_g[None], zeros, zeros, seq, False, F32)
    return out.reshape(batch, seq, d)
```

```python
import functools

import jax
import jax.numpy as jnp
from jax import lax
from jax.experimental import pallas as pl
from jax.experimental.pallas import tpu as pltpu

F32 = jnp.float32
BF16 = jnp.bfloat16

HEAD_DIM = 128
N_POOL_GROUPS = 4
POOL_WINDOWS = (2, 4, 8, 16)
POOL_HALO = 16
N_LRU_BLOCKS = 4
LRU_CONV_WIDTH = 4
LRU_C = 8.0
FFN_CONV_WIDTH = 3
EPS = 1e-6

LANES = 128
SUBLANES = 8
VMEM_LIMIT_CAP_V7X = 56 * 2**20
MASK_VALUE = -1e30
LOG2E = 1.4426950408889634


def _cparams(semantics, vmem_bytes):
    return pltpu.CompilerParams(dimension_semantics=semantics,
                                vmem_limit_bytes=int(min(VMEM_LIMIT_CAP_V7X, max(vmem_bytes, 16 * 2**20))))


def _largest_tile(n, cap, multiple):
    best = multiple
    for t in range(multiple, min(n, cap) + 1, multiple):
        if n % t == 0:
            best = t
    return best


def _sigmoid(x):
    return 1.0 / (1.0 + jnp.exp(-x))


def _norm_mod(x, g, sc, sh):
    ms = jnp.mean(x * x, axis=-1, keepdims=True)
    return (x * lax.rsqrt(ms + EPS) * g) * (1.0 + sc) + sh


def _split3_bf16(x):
    hi = x.astype(BF16)
    r1 = x - hi.astype(F32)
    mid = r1.astype(BF16)
    lo = (r1 - mid.astype(F32)).astype(BF16)
    return hi, mid, lo


def _ada_kernel(c_ref, w_ref, b_ref, o_ref):
    nb = c_ref.shape[0]
    tn = w_ref.shape[3]
    o_ref[...] = jnp.zeros_like(o_ref)
    ca = [c_ref[b] * _sigmoid(c_ref[b]) for b in range(nb)]
    for jc in range(tn // LANES):
        sl = slice(jc * LANES, (jc + 1) * LANES)
        w = w_ref[0, :, :, sl]
        for b in range(nb):
            part = jnp.sum(w * ca[b], axis=0)
            o_ref[0, b:b + 1, sl] = jnp.sum(part, axis=0, keepdims=True) + b_ref[0, :, sl]


def _ada_mod(c, w_ada, b_ada):
    depth, d, n = w_ada.shape
    batch = c.shape[0]
    tn = 512
    dr = d // SUBLANES
    c_lanes = jnp.broadcast_to(c[:, :, None], (batch, d, LANES)).reshape(batch, dr, SUBLANES, LANES)
    return pl.pallas_call(
        _ada_kernel,
        grid=(depth, n // tn),
        in_specs=[pl.BlockSpec((batch, dr, SUBLANES, LANES), lambda l, j: (0, 0, 0, 0)),
                  pl.BlockSpec((1, dr, SUBLANES, tn), lambda l, j: (l, 0, 0, j)),
                  pl.BlockSpec((1, 1, tn), lambda l, j: (l, 0, j))],
        out_specs=pl.BlockSpec((1, SUBLANES, tn), lambda l, j: (l, 0, j)),
        out_shape=jax.ShapeDtypeStruct((depth, SUBLANES, n), F32),
        compiler_params=_cparams(("parallel", "parallel"), 4 * d * tn * 4 + 8 * batch * d * LANES * 4),
        name="ada_mod",
    )(c_lanes, w_ada.reshape(depth, dr, SUBLANES, n), b_ada.reshape(depth, 1, n))


def _norm_kernel(x_ref, g_ref, sc_ref, sh_ref, h_ref):
    h_ref[...] = _norm_mod(x_ref[...], g_ref[...], sc_ref[0], sh_ref[0]).astype(h_ref.dtype)


def _norm(x2, g, sc, sh, seq):
    t, d = x2.shape
    tm = min(512, seq)
    tpb = seq // tm
    return pl.pallas_call(
        _norm_kernel,
        grid=(t // tm,),
        in_specs=[pl.BlockSpec((tm, d), lambda i: (i, 0)),
                  pl.BlockSpec((1, d), lambda i: (0, 0)),
                  pl.BlockSpec((1, 1, d), lambda i: (i // tpb, 0, 0)),
                  pl.BlockSpec((1, 1, d), lambda i: (i // tpb, 0, 0))],
        out_specs=pl.BlockSpec((tm, d), lambda i: (i, 0)),
        out_shape=jax.ShapeDtypeStruct((t, d), BF16),
        compiler_params=_cparams(("parallel",), 6 * tm * d * 4),
        name="norm_mod",
    )(x2, g, sc, sh)


def _mm_kernel(a_ref, w_ref, o_ref):
    o_ref[...] = jnp.dot(a_ref[...], w_ref[0], preferred_element_type=F32).astype(o_ref.dtype)


def _mm_scaled_kernel(a_ref, w_ref, cs_ref, o_ref):
    acc = jnp.dot(a_ref[...], w_ref[0], preferred_element_type=F32)
    o_ref[...] = (acc * cs_ref[...]).astype(o_ref.dtype)


def _matmul(a, w_stack, layer, out_dtype, tn, col_scale=None):
    t, k = a.shape
    n = w_stack.shape[2]
    tm = min(1024, t)
    in_specs = [pl.BlockSpec((tm, k), lambda i, j: (i, 0)),
                pl.BlockSpec((1, k, tn), lambda i, j: (layer, 0, j))]
    args = [a, w_stack]
    body = _mm_kernel
    if col_scale is not None:
        in_specs.append(pl.BlockSpec((1, tn), lambda i, j: (0, j)))
        args.append(col_scale)
        body = _mm_scaled_kernel
    vmem = 2 * (tm * k * 2 + k * tn * 2 + tm * tn * 4) + 2 * tm * tn * 4
    return pl.pallas_call(
        body,
        grid=(t // tm, n // tn),
        in_specs=in_specs,
        out_specs=pl.BlockSpec((tm, tn), lambda i, j: (i, j)),
        out_shape=jax.ShapeDtypeStruct((t, n), out_dtype),
        compiler_params=_cparams(("parallel", "parallel"), vmem),
        name="matmul_dmodel",
    )(*args)


def _fcum_kernel(z_ref, b_ref, o_ref, carry_ref):
    ts = z_ref.shape[1]
    w = z_ref.shape[2]
    wo = o_ref.shape[2]

    @pl.when(pl.program_id(1) == 0)
    def _():
        carry_ref[...] = jnp.zeros_like(carry_ref)

    x = z_ref[0] + b_ref[...]
    lf = jnp.minimum(x, 0.0) - jnp.log(1.0 + jnp.exp(-jnp.abs(x)))
    r = lax.broadcasted_iota(jnp.int32, (ts, ts), 0)
    c = lax.broadcasted_iota(jnp.int32, (ts, ts), 1)
    tri = jnp.where(c <= r, 1.0, 0.0).astype(BF16)
    cs = carry_ref[...]
    for part in _split3_bf16(lf):
        cs = cs + jnp.dot(tri, part, preferred_element_type=F32)
    carry_ref[...] = cs[ts - 1:ts, :]

    head = lax.broadcasted_iota(jnp.int32, (w, wo), 0)
    lane = lax.broadcasted_iota(jnp.int32, (w, wo), 1)
    bias = jnp.zeros((ts, wo), F32)
    for idx, part in enumerate(_split3_bf16(cs * LOG2E)):
        place = jnp.where(lane == head * HEAD_DIM + idx, 1.0, 0.0).astype(BF16)
        bias = bias + jnp.dot(part, place, preferred_element_type=F32)
    o_ref[0] = bias.astype(o_ref.dtype)


def _forget_bias(zf, bf_pad, n_heads):
    b, s, w = zf.shape
    ts = min(512, s)
    wo = n_heads * HEAD_DIM
    return pl.pallas_call(
        _fcum_kernel,
        grid=(b, s // ts),
        in_specs=[pl.BlockSpec((1, ts, w), lambda bi, i: (bi, i, 0)),
                  pl.BlockSpec((1, w), lambda bi, i: (0, 0))],
        out_specs=pl.BlockSpec((1, ts, wo), lambda bi, i: (bi, i, 0)),
        out_shape=jax.ShapeDtypeStruct((b, s, wo), BF16),
        scratch_shapes=[pltpu.VMEM((1, w), F32)],
        compiler_params=_cparams(("arbitrary", "arbitrary"), 8 * ts * ts * 4 + 8 * ts * wo * 4),
        name="forget_bias",
    )(zf, bf_pad)


ONES_ROWS = 16


MAX_CHAINS = 8
BIG_CHUNK = 4


def _attn_kernel(q_ref, k_ref, v_ref, fb_ref, o_ref, vt_ref, acc_ref, m_ref):
    tq = q_ref.shape[1]
    tk = tq
    s_len = k_ref.shape[1]
    qi = pl.program_id(2)

    @pl.when(qi == 0)
    def _():
        vt_ref[HEAD_DIM:, :] = jnp.ones((ONES_ROWS, s_len), BF16)
        for c in range(s_len // tk):
            vt_ref[:HEAD_DIM, c * tk:(c + 1) * tk] = v_ref[0, c * tk:(c + 1) * tk, :].astype(F32).T.astype(BF16)

    lane = lax.broadcasted_iota(jnp.int32, (tq, HEAD_DIM), 1)
    minus_one = jnp.where(lane < 3, -1.0, 0.0).astype(BF16)
    q_aug = jnp.concatenate([q_ref[0], minus_one], axis=1)
    m_ref[...] = jnp.full(m_ref.shape, MASK_VALUE, F32)
    acc_ref[...] = jnp.zeros_like(acc_ref)

    def chunk(k0, size, masked):
        k0 = pl.multiple_of(k0, tk)
        k_aug = jnp.concatenate([k_ref[0, pl.ds(k0, size), :], fb_ref[0, pl.ds(k0, size), :]], axis=1)
        st = lax.dot_general(k_aug, q_aug, (((1,), (1,)), ((), ())), preferred_element_type=F32)
        if masked:
            krow = lax.broadcasted_iota(jnp.int32, (size, tq), 0)
            qcol = lax.broadcasted_iota(jnp.int32, (size, tq), 1)
            st = jnp.where(krow <= qcol, st, MASK_VALUE)
        part = jnp.max(st.reshape(MAX_CHAINS, size // MAX_CHAINS, tq), axis=1)
        m_old = m_ref[...]
        m_new = jnp.maximum(m_old, jnp.max(part, axis=0, keepdims=True))
        m_ref[...] = m_new
        pt = jnp.exp2((st - m_new).astype(BF16))
        acc_ref[...] = jnp.exp2(m_old - m_new) * acc_ref[...] + jnp.dot(
            vt_ref[:, pl.ds(k0, size)], pt, preferred_element_type=F32)

    def body(kc, carry):
        chunk(kc * (BIG_CHUNK * tk), BIG_CHUNK * tk, False)
        return carry

    n_big = qi // BIG_CHUNK
    lax.fori_loop(0, n_big, body, 0)
    done = n_big * BIG_CHUNK
    part = BIG_CHUNK // 2
    while part >= 1:
        take = ((qi - done) & part) != 0

        @pl.when(take)
        def _(done=done, part=part):
            chunk(done * tk, part * tk, False)

        done = done + jnp.where(take, part, 0)
        part //= 2

    chunk(qi * tk, tk, True)
    o_t = acc_ref[:HEAD_DIM, :] * (1.0 / acc_ref[HEAD_DIM:HEAD_DIM + 1, :])
    o_ref[0] = o_t.T.astype(o_ref.dtype)


def _attention(zqkv, fbias, n_heads):
    b, s, _ = zqkv.shape
    tq = min(512, s)
    assert s % tq == 0
    vmem = 2 * (3 * s * HEAD_DIM * 2) + (HEAD_DIM + ONES_ROWS) * s * 2 + 6 * BIG_CHUNK * tq * tq * 4
    kv_spec = lambda off: pl.BlockSpec((1, s, HEAD_DIM), lambda bi, h, i: (bi, 0, off + h))
    return pl.pallas_call(
        _attn_kernel,
        grid=(b, n_heads, s // tq),
        in_specs=[pl.BlockSpec((1, tq, HEAD_DIM), lambda bi, h, i: (bi, i, h)),
                  kv_spec(n_heads), kv_spec(2 * n_heads),
                  pl.BlockSpec((1, s, HEAD_DIM), lambda bi, h, i: (bi, 0, h))],
        out_specs=pl.BlockSpec((1, tq, HEAD_DIM), lambda bi, h, i: (bi, i, h)),
        out_shape=jax.ShapeDtypeStruct((b, s, n_heads * HEAD_DIM), BF16),
        scratch_shapes=[pltpu.VMEM((HEAD_DIM + ONES_ROWS, s), BF16),
                        pltpu.VMEM((HEAD_DIM + ONES_ROWS, tq), F32),
                        pltpu.VMEM((1, tq), F32)],
        compiler_params=_cparams(("parallel", "parallel", "arbitrary"), vmem),
        name="fox_attention",
    )(zqkv, zqkv, zqkv, fbias)


def _pool_kernel(u_ref, halo_ref, w_ref, sc_ref, o_ref):
    tm = u_ref.shape[1]
    i = pl.program_id(1)
    halo = jnp.where(i > 0, halo_ref[0], 0.0)
    ext = jnp.concatenate([halo, u_ref[0]], axis=0)
    pos = (i * tm + 1 + lax.broadcasted_iota(jnp.int32, (tm, 1), 0)).astype(F32)
    gd = LANES
    for g, win in enumerate(POOL_WINDOWS):
        e = ext[:, g * gd:(g + 1) * gd]
        ssum = e
        shift = 1
        while shift < win:
            ssum = ssum + pltpu.roll(ssum, shift, 0)
            shift *= 2
        mean = ssum[POOL_HALO:] * (1.0 / jnp.minimum(pos, float(win)))
        dlt = mean - e[POOL_HALO:]
        y = jnp.dot(dlt.astype(BF16), w_ref[0, g], preferred_element_type=F32)
        o_ref[0, :, g * gd:(g + 1) * gd] = (y * sc_ref[:, g * gd:(g + 1) * gd]).astype(o_ref.dtype)


def _pool_mixer(z32, pool_w_stack, layer, scale_row, width):
    b, s, _ = z32.shape
    tm = min(512, s)
    hb = tm // POOL_HALO
    return pl.pallas_call(
        _pool_kernel,
        grid=(b, s // tm),
        in_specs=[pl.BlockSpec((1, tm, width), lambda bi, i: (bi, i, 0)),
                  pl.BlockSpec((1, POOL_HALO, width), lambda bi, i: (bi, jnp.maximum(i * hb - 1, 0), 0)),
                  pl.BlockSpec((1, N_POOL_GROUPS, LANES, LANES), lambda bi, i: (layer, 0, 0, 0)),
                  pl.BlockSpec((1, width), lambda bi, i: (0, 0))],
        out_specs=pl.BlockSpec((1, tm, width), lambda bi, i: (bi, i, 0)),
        out_shape=jax.ShapeDtypeStruct((b, s, width), BF16),
        compiler_params=_cparams(("parallel", "parallel"), 16 * tm * width * 4),
        name="pool_mixer",
    )(z32, z32, pool_w_stack, scale_row)


def _lru_kernel(x_ref, halo_ref, y_ref, cw_ref, cb_ref, wa_ref, ba_ref, wi_ref, bi_ref, lam_ref,
                o_ref, h_ref, a_s, b_s, h_s):
    tm = x_ref.shape[1]
    i = pl.program_id(1)

    @pl.when(i == 0)
    def _():
        h_ref[...] = jnp.zeros_like(h_ref)

    halo = jnp.where(i > 0, halo_ref[0], 0.0)
    ext = jnp.concatenate([halo, x_ref[0]], axis=0)
    xc = cb_ref[...] + cw_ref[LRU_CONV_WIDTH - 1:LRU_CONV_WIDTH, :] * ext[SUBLANES:]
    for k in range(LRU_CONV_WIDTH - 1):
        shifted = pltpu.roll(ext, LRU_CONV_WIDTH - 1 - k, 0)[SUBLANES:]
        xc = xc + cw_ref[k:k + 1, :] * shifted

    lam = lam_ref[...]
    neg_softplus = -(jnp.maximum(-lam, 0.0) + jnp.log(1.0 + jnp.exp(-jnp.abs(lam))))
    gd = LANES
    for blk in range(N_LRU_BLOCKS):
        sl = slice(blk * gd, (blk + 1) * gd)
        xb = xc[:, sl]
        xb16 = xb.astype(BF16)
        gate_r = _sigmoid(jnp.dot(xb16, wa_ref[0, blk], preferred_element_type=F32) + ba_ref[:, sl])
        gate_i = _sigmoid(jnp.dot(xb16, wi_ref[0, blk], preferred_element_type=F32) + bi_ref[:, sl])
        log_a = LRU_C * gate_r * neg_softplus[:, sl]
        a_s[blk] = jnp.exp(log_a)
        b_s[blk] = jnp.sqrt(1.0 - jnp.exp(2.0 * log_a)) * (gate_i * xb)

    n = tm // SUBLANES

    def step(s, carry):
        rows = pl.ds(s, SUBLANES, stride=n)
        out = []
        for blk in range(N_LRU_BLOCKS):
            h, p = carry[blk]
            a = a_s[blk, rows, :]
            h = a * h + b_s[blk, rows, :]
            p = a * p
            h_s[blk, rows, :] = h
            a_s[blk, rows, :] = p
            out.append((h, p))
        return tuple(out)

    init = tuple((jnp.zeros((SUBLANES, gd), F32), jnp.ones((SUBLANES, gd), F32)) for _ in range(N_LRU_BLOCKS))
    ends = lax.fori_loop(0, n, step, init)

    for blk in range(N_LRU_BLOCKS):
        sl = slice(blk * gd, (blk + 1) * gd)
        h_end, p_end = ends[blk]
        h_in = h_ref[:, sl]
        for c in range(SUBLANES):
            rows = slice(c * n, (c + 1) * n)
            y = y_ref[0, rows, sl]
            gelu = 0.5 * y * (1.0 + jnp.tanh(0.7978845608028654 * (y + 0.044715 * (y * y * y))))
            o_ref[0, rows, sl] = ((h_s[blk, rows, :] + a_s[blk, rows, :] * h_in) * gelu).astype(o_ref.dtype)
            h_in = h_end[c:c + 1, :] + p_end[c:c + 1, :] * h_in
        h_ref[:, sl] = h_in


def _lru_mixer(z32, layer, p, width, x_col, y_col):
    b, s, _ = z32.shape
    tm = min(512, s)
    hb = tm // SUBLANES
    row = lambda bi, i: (0, 0)
    return pl.pallas_call(
        _lru_kernel,
        grid=(b, s // tm),
        in_specs=[pl.BlockSpec((1, tm, width), lambda bi, i: (bi, i, x_col)),
                  pl.BlockSpec((1, SUBLANES, width), lambda bi, i: (bi, jnp.maximum(i * hb - 1, 0), x_col)),
                  pl.BlockSpec((1, tm, width), lambda bi, i: (bi, i, y_col)),
                  pl.BlockSpec((LRU_CONV_WIDTH, width), row),
                  pl.BlockSpec((1, width), row),
                  pl.BlockSpec((1, N_LRU_BLOCKS, LANES, LANES), lambda bi, i: (layer, 0, 0, 0)),
                  pl.BlockSpec((1, width), row),
                  pl.BlockSpec((1, N_LRU_BLOCKS, LANES, LANES), lambda bi, i: (layer, 0, 0, 0)),
                  pl.BlockSpec((1, width), row),
                  pl.BlockSpec((1, width), row)],
        out_specs=pl.BlockSpec((1, tm, width), lambda bi, i: (bi, i, 0)),
        out_shape=jax.ShapeDtypeStruct((b, s, width), BF16),
        scratch_shapes=[pltpu.VMEM((1, width), F32)] + [pltpu.VMEM((N_LRU_BLOCKS, tm, LANES), F32)] * 3,
        compiler_params=_cparams(("arbitrary", "arbitrary"), 24 * tm * width * 4),
        name="rg_lru",
    )(z32, z32, z32, p["conv_w"], p["conv_b"], p["wa"], p["ba"], p["wi"], p["bi"], p["lam"])


def _residual_norm(acc, x_ref, gt_ref, g_ref, sc_ref, sh_ref, out_refs, emit_x):
    xn = x_ref[...] + gt_ref[0] * acc
    if emit_x:
        out_refs[0][...] = xn
    h_ref = out_refs[-1]
    h_ref[...] = _norm_mod(xn, g_ref[...], sc_ref[0], sh_ref[0]).astype(h_ref.dtype)


def _mix_out_kernel(yp_ref, ya_ref, yl_ref, w_ref, x_ref, gt_ref, g_ref, sc_ref, sh_ref, *out_refs):
    kp = yp_ref.shape[1]
    ka = ya_ref.shape[1]
    acc = jnp.dot(yp_ref[...], w_ref[0, 0:kp, :], preferred_element_type=F32)
    acc = acc + jnp.dot(ya_ref[...], w_ref[0, kp:kp + ka, :], preferred_element_type=F32)
    acc = acc + jnp.dot(yl_ref[...], w_ref[0, kp + ka:, :], preferred_element_type=F32)
    _residual_norm(acc, x_ref, gt_ref, g_ref, sc_ref, sh_ref, out_refs, True)


def _mix_out(yp, ya, yl, w_stack, layer, x2, gt, g, sc, sh, seq):
    t, d = x2.shape
    tm = min(512, seq)
    tpb = seq // tm
    per_b = lambda i: (i // tpb, 0, 0)
    lhs = lambda y: pl.BlockSpec((tm, y.shape[1]), lambda i: (i, 0))
    vmem = 2 * d * d * 2 + 2 * (tm * d * 2 + 3 * tm * d * 4) + 3 * tm * d * 4
    return pl.pallas_call(
        _mix_out_kernel,
        grid=(t // tm,),
        in_specs=[lhs(yp), lhs(ya), lhs(yl),
                  pl.BlockSpec((1, d, d), lambda i: (layer, 0, 0)),
                  pl.BlockSpec((tm, d), lambda i: (i, 0)),
                  pl.BlockSpec((1, 1, d), per_b),
                  pl.BlockSpec((1, d), lambda i: (0, 0)),
                  pl.BlockSpec((1, 1, d), per_b),
                  pl.BlockSpec((1, 1, d), per_b)],
        out_specs=[pl.BlockSpec((tm, d), lambda i: (i, 0)), pl.BlockSpec((tm, d), lambda i: (i, 0))],
        out_shape=[jax.ShapeDtypeStruct((t, d), F32), jax.ShapeDtypeStruct((t, d), BF16)],
        compiler_params=_cparams(("parallel",), vmem),
        name="mix_out_proj",
    )(yp, ya, yl, w_stack, x2, gt, g, sc, sh)


def _ffn_down_kernel(a_ref, w_ref, x_ref, gt_ref, g_ref, sc_ref, sh_ref, *refs, emit_x):
    out_refs, acc_ref = refs[:-1], refs[-1]
    k = pl.program_id(1)

    @pl.when(k == 0)
    def _():
        acc_ref[...] = jnp.zeros_like(acc_ref)

    acc_ref[...] += jnp.dot(a_ref[...], w_ref[0], preferred_element_type=F32)

    @pl.when(k == pl.num_programs(1) - 1)
    def _():
        _residual_norm(acc_ref[...], x_ref, gt_ref, g_ref, sc_ref, sh_ref, out_refs, emit_x)


def _ffn_down(act, w_stack, layer, x2, gt, g, sc, sh, seq, emit_x, h_dtype):
    t, d = x2.shape
    kf = act.shape[1]
    tm = min(512, seq)
    tk = _largest_tile(kf, 1536, LANES)
    tpb = seq // tm
    per_b = lambda i, k: (i // tpb, 0, 0)
    row_tile = pl.BlockSpec((tm, d), lambda i, k: (i, 0))
    out_specs = [row_tile]
    out_shape = [jax.ShapeDtypeStruct((t, d), h_dtype)]
    if emit_x:
        out_specs = [row_tile, row_tile]
        out_shape = [jax.ShapeDtypeStruct((t, d), F32)] + out_shape
    vmem = 2 * (tm * tk * 2 + tk * d * 2 + 3 * tm * d * 4) + 3 * tm * d * 4
    return pl.pallas_call(
        functools.partial(_ffn_down_kernel, emit_x=emit_x),
        grid=(t // tm, kf // tk),
        in_specs=[pl.BlockSpec((tm, tk), lambda i, k: (i, k)),
                  pl.BlockSpec((1, tk, d), lambda i, k: (layer, k, 0)),
                  row_tile,
                  pl.BlockSpec((1, 1, d), per_b),
                  pl.BlockSpec((1, d), lambda i, k: (0, 0)),
                  pl.BlockSpec((1, 1, d), per_b),
                  pl.BlockSpec((1, 1, d), per_b)],
        out_specs=out_specs,
        out_shape=out_shape,
        scratch_shapes=[pltpu.VMEM((tm, d), F32)],
        compiler_params=_cparams(("parallel", "arbitrary"), vmem),
        name="ffn_down_proj",
    )(act, w_stack, x2, gt, g, sc, sh)


def _ffn_gate_kernel(h_ref, wg_ref, wu_ref, cw_ref, cb_ref, o_ref, carry_ref, *, tiles_per_batch):
    tm = h_ref.shape[0]
    i = pl.program_id(0)
    j = pl.program_id(1)

    @pl.when(i % tiles_per_batch == 0)
    def _():
        carry_ref[j] = jnp.zeros(carry_ref.shape[1:], F32)

    h = h_ref[...]
    u = jnp.dot(h, wg_ref[0], preferred_element_type=F32)
    ext = jnp.concatenate([carry_ref[j], u], axis=0)
    carry_ref[j] = u[tm - SUBLANES:, :]
    g = cb_ref[...] + cw_ref[FFN_CONV_WIDTH - 1:FFN_CONV_WIDTH, :] * u
    for k in range(FFN_CONV_WIDTH - 1):
        shifted = pltpu.roll(ext, FFN_CONV_WIDTH - 1 - k, 0)[SUBLANES:]
        g = g + cw_ref[k:k + 1, :] * shifted
    up = jnp.dot(h, wu_ref[0], preferred_element_type=F32)
    o_ref[...] = (g * _sigmoid(g) * up).astype(o_ref.dtype)


def _ffn_gate(h, wg_stack, wu_stack, layer, conv_w, conv_b, seq):
    t, d = h.shape
    kf = wg_stack.shape[2]
    tm = min(1024, seq)
    tf = 512
    wspec = pl.BlockSpec((1, d, tf), lambda i, j: (layer, 0, j))
    vmem = 2 * (tm * d * 2 + 2 * d * tf * 2 + tm * tf * 2) + 8 * tm * tf * 4
    return pl.pallas_call(
        functools.partial(_ffn_gate_kernel, tiles_per_batch=seq // tm),
        grid=(t // tm, kf // tf),
        in_specs=[pl.BlockSpec((tm, d), lambda i, j: (i, 0)), wspec, wspec,
                  pl.BlockSpec((FFN_CONV_WIDTH, tf), lambda i, j: (0, j)),
                  pl.BlockSpec((1, tf), lambda i, j: (0, j))],
        out_specs=pl.BlockSpec((tm, tf), lambda i, j: (i, j)),
        out_shape=jax.ShapeDtypeStruct((t, kf), BF16),
        scratch_shapes=[pltpu.VMEM((kf // tf, SUBLANES, tf), F32)],
        compiler_params=_cparams(("arbitrary", "arbitrary"), vmem),
        name="ffn_gate_up",
    )(h, wg_stack, wu_stack, conv_w, conv_b)


def kernel(x, c, w_ada, b_ada, g_mix, w_in, b_f, pool_w, pool_scale, lru_conv_w, lru_conv_b, lru_wa, lru_ba,
           lru_wi, lru_bi, lru_lambda, w_out, g_ffn, w_ffn_gate, w_ffn_up, ffn_conv_w, ffn_conv_b, w_ffn_down,
           final_g):
    batch, seq, d = x.shape
    depth = w_ada.shape[0]
    pool_width = pool_w.shape[1] * pool_w.shape[2]
    lru_width = lru_lambda.shape[1]
    n_heads = b_f.shape[1]
    attn_width = n_heads * HEAD_DIM
    assert w_in.shape[2] == pool_width + 3 * attn_width + n_heads + 2 * lru_width
    assert pool_width == lru_width == N_POOL_GROUPS * LANES and seq % SUBLANES == 0

    o_q = pool_width
    o_f = o_q + 3 * attn_width
    o_x = o_f + n_heads
    w_qkv = w_in[:, :, o_q:o_f].astype(BF16)
    w_pxy = jnp.concatenate([w_in[:, :, :o_q], w_in[:, :, o_x:]], axis=2).astype(BF16)
    w_f = jnp.pad(w_in[:, :, o_f:o_x], ((0, 0), (0, 0), (0, LANES - n_heads))).astype(BF16)
    bf_pad = jnp.pad(b_f, ((0, 0), (0, LANES - n_heads)))
    qkv_scale = jnp.concatenate([jnp.full((1, attn_width), HEAD_DIM ** -0.5 * LOG2E, F32),
                                 jnp.ones((1, 2 * attn_width), F32)], axis=1)
    pool_w16 = pool_w.astype(BF16)
    lru_wa16 = lru_wa.astype(BF16)
    lru_wi16 = lru_wi.astype(BF16)
    w_out16 = w_out.astype(BF16)
    w_gate16 = w_ffn_gate.astype(BF16)
    w_up16 = w_ffn_up.astype(BF16)
    w_down16 = w_ffn_down.astype(BF16)

    mod = _ada_mod(c, w_ada, b_ada)[:, :batch]

    def mod_chunk(layer, idx):
        return mod[layer, :, idx * d:(idx + 1) * d].reshape(batch, 1, d)

    x2 = x.reshape(batch * seq, d)
    h = _norm(x2, g_mix[0][None], mod_chunk(0, 1), mod_chunk(0, 0), seq)
    out = None
    for layer in range(depth):
        sh1, sc1, gt1, sh2, sc2, gt2 = (mod_chunk(layer, idx) for idx in range(6))
        zqkv = _matmul(h, w_qkv, layer, BF16, 1024, qkv_scale).reshape(batch, seq, 3 * attn_width)
        z32 = _matmul(h, w_pxy, layer, F32, 512).reshape(batch, seq, pool_width + 2 * lru_width)
        zf = _matmul(h, w_f, layer, F32, LANES).reshape(batch, seq, LANES)

        fbias = _forget_bias(zf, bf_pad[layer][None], n_heads)
        y_attn = _attention(zqkv, fbias, n_heads).reshape(batch * seq, attn_width)
        y_pool = _pool_mixer(z32, pool_w16, layer, pool_scale[layer][None], pool_width)
        lru_p = dict(conv_w=lru_conv_w[layer], conv_b=lru_conv_b[layer][None], wa=lru_wa16,
                     ba=lru_ba[layer][None], wi=lru_wi16, bi=lru_bi[layer][None], lam=lru_lambda[layer][None])
        y_lru = _lru_mixer(z32, layer, lru_p, lru_width, 1, 2)

        x2, h = _mix_out(y_pool.reshape(batch * seq, pool_width), y_attn, y_lru.reshape(batch * seq, lru_width),
                         w_out16, layer, x2, gt1, g_ffn[layer][None], sc2, sh2, seq)
        act = _ffn_gate(h, w_gate16, w_up16, layer, ffn_conv_w[layer], ffn_conv_b[layer][None], seq)
        if layer + 1 < depth:
            x2, h = _ffn_down(act, w_down16, layer, x2, gt2, g_mix[layer + 1][None], mod_chunk(layer + 1, 1),
                              mod_chunk(layer + 1, 0), seq, True, BF16)
        else:
            zeros = jnp.zeros((batch, 1, d), F32)
            (out,) = _ffn_down(act, w_down16, layer, x2, gt2, final_g[None], zeros, zeros, seq, False, F32)
    return out.reshape(batch, seq, d)
```

```python
import functools

import jax
import jax.numpy as jnp
from jax import lax
from jax.experimental import pallas as pl
from jax.experimental.pallas import tpu as pltpu

F32 = jnp.float32
BF16 = jnp.bfloat16

HEAD_DIM = 128
N_POOL_GROUPS = 4
POOL_WINDOWS = (2, 4, 8, 16)
POOL_HALO = 16
N_LRU_BLOCKS = 4
LRU_CONV_WIDTH = 4
LRU_C = 8.0
FFN_CONV_WIDTH = 3
EPS = 1e-6

LANES = 128
SUBLANES = 8
VMEM_LIMIT_CAP_V7X = 58 * 2**20
MASK_VALUE = -1e30
LOG2E = 1.4426950408889634


def _cparams(semantics, vmem_bytes):
    return pltpu.CompilerParams(dimension_semantics=semantics,
                                vmem_limit_bytes=int(min(VMEM_LIMIT_CAP_V7X, max(vmem_bytes, 16 * 2**20))))


def _largest_tile(n, cap, multiple):
    best = multiple
    for t in range(multiple, min(n, cap) + 1, multiple):
        if n % t == 0:
            best = t
    return best


def _sigmoid(x):
    return 0.5 * jnp.tanh(0.5 * x) + 0.5


def _norm_mod(x, g, sc, sh):
    ms = jnp.mean(x * x, axis=-1, keepdims=True)
    return (x * lax.rsqrt(ms + EPS) * g) * (1.0 + sc) + sh


def _split3_bf16(x):
    hi = x.astype(BF16)
    r1 = x - hi.astype(F32)
    mid = r1.astype(BF16)
    lo = (r1 - mid.astype(F32)).astype(BF16)
    return hi, mid, lo


def _ada_kernel(c_ref, w_ref, b_ref, o_ref):
    nb = c_ref.shape[0]
    tn = w_ref.shape[3]
    o_ref[...] = jnp.zeros_like(o_ref)
    ca = [c_ref[b] * _sigmoid(c_ref[b]) for b in range(nb)]
    for jc in range(tn // LANES):
        sl = slice(jc * LANES, (jc + 1) * LANES)
        w = w_ref[0, :, :, sl]
        for b in range(nb):
            part = jnp.sum(w * ca[b], axis=0)
            o_ref[0, b:b + 1, sl] = jnp.sum(part, axis=0, keepdims=True) + b_ref[0, :, sl]


def _ada_mod(c, w_ada, b_ada):
    depth, d, n = w_ada.shape
    batch = c.shape[0]
    tn = 512
    dr = d // SUBLANES
    c_lanes = jnp.broadcast_to(c[:, :, None], (batch, d, LANES)).reshape(batch, dr, SUBLANES, LANES)
    return pl.pallas_call(
        _ada_kernel,
        grid=(depth, n // tn),
        in_specs=[pl.BlockSpec((batch, dr, SUBLANES, LANES), lambda l, j: (0, 0, 0, 0)),
                  pl.BlockSpec((1, dr, SUBLANES, tn), lambda l, j: (l, 0, 0, j)),
                  pl.BlockSpec((1, 1, tn), lambda l, j: (l, 0, j))],
        out_specs=pl.BlockSpec((1, SUBLANES, tn), lambda l, j: (l, 0, j)),
        out_shape=jax.ShapeDtypeStruct((depth, SUBLANES, n), F32),
        compiler_params=_cparams(("parallel", "parallel"), 4 * d * tn * 4 + 8 * batch * d * LANES * 4),
        name="ada_mod",
    )(c_lanes, w_ada.reshape(depth, dr, SUBLANES, n), b_ada.reshape(depth, 1, n))


def _norm_kernel(x_ref, g_ref, sc_ref, sh_ref, h_ref):
    h_ref[...] = _norm_mod(x_ref[...], g_ref[...], sc_ref[0], sh_ref[0]).astype(h_ref.dtype)


def _norm(x2, g, sc, sh, seq):
    t, d = x2.shape
    tm = min(512, seq)
    tpb = seq // tm
    return pl.pallas_call(
        _norm_kernel,
        grid=(t // tm,),
        in_specs=[pl.BlockSpec((tm, d), lambda i: (i, 0)),
                  pl.BlockSpec((1, d), lambda i: (0, 0)),
                  pl.BlockSpec((1, 1, d), lambda i: (i // tpb, 0, 0)),
                  pl.BlockSpec((1, 1, d), lambda i: (i // tpb, 0, 0))],
        out_specs=pl.BlockSpec((tm, d), lambda i: (i, 0)),
        out_shape=jax.ShapeDtypeStruct((t, d), BF16),
        compiler_params=_cparams(("parallel",), 6 * tm * d * 4),
        name="norm_mod",
    )(x2, g, sc, sh)


def _mm_kernel(a_ref, w_ref, o_ref):
    o_ref[...] = jnp.dot(a_ref[...], w_ref[0], preferred_element_type=F32).astype(o_ref.dtype)


def _mm_scaled_kernel(a_ref, w_ref, cs_ref, o_ref):
    acc = jnp.dot(a_ref[...], w_ref[0], preferred_element_type=F32)
    o_ref[...] = (acc * cs_ref[...]).astype(o_ref.dtype)


def _matmul(a, w_stack, layer, out_dtype, tn, col_scale=None):
    t, k = a.shape
    n = w_stack.shape[2]
    tm = min(1024, t)
    in_specs = [pl.BlockSpec((tm, k), lambda i, j: (i, 0)),
                pl.BlockSpec((1, k, tn), lambda i, j: (layer, 0, j))]
    args = [a, w_stack]
    body = _mm_kernel
    if col_scale is not None:
        in_specs.append(pl.BlockSpec((1, tn), lambda i, j: (0, j)))
        args.append(col_scale)
        body = _mm_scaled_kernel
    vmem = 2 * (tm * k * 2 + k * tn * 2 + tm * tn * 4) + 2 * tm * tn * 4
    return pl.pallas_call(
        body,
        grid=(t // tm, n // tn),
        in_specs=in_specs,
        out_specs=pl.BlockSpec((tm, tn), lambda i, j: (i, j)),
        out_shape=jax.ShapeDtypeStruct((t, n), out_dtype),
        compiler_params=_cparams(("parallel", "parallel"), vmem),
        name="matmul_dmodel",
    )(*args)


def _fcum_kernel(z_ref, b_ref, o_ref, carry_ref):
    ts = z_ref.shape[1]
    w = z_ref.shape[2]
    wo = o_ref.shape[2]

    @pl.when(pl.program_id(1) == 0)
    def _():
        carry_ref[...] = jnp.zeros_like(carry_ref)

    x = z_ref[0] + b_ref[...]
    lf = jnp.minimum(x, 0.0) - jnp.log(1.0 + jnp.exp(-jnp.abs(x)))
    r = lax.broadcasted_iota(jnp.int32, (ts, ts), 0)
    c = lax.broadcasted_iota(jnp.int32, (ts, ts), 1)
    tri = jnp.where(c <= r, 1.0, 0.0).astype(BF16)
    cs = carry_ref[...]
    for part in _split3_bf16(lf):
        cs = cs + jnp.dot(tri, part, preferred_element_type=F32)
    carry_ref[...] = cs[ts - 1:ts, :]

    head = lax.broadcasted_iota(jnp.int32, (w, wo), 0)
    lane = lax.broadcasted_iota(jnp.int32, (w, wo), 1)
    bias = jnp.zeros((ts, wo), F32)
    for idx, part in enumerate(_split3_bf16(cs * LOG2E)):
        place = jnp.where(lane == head * HEAD_DIM + idx, 1.0, 0.0).astype(BF16)
        bias = bias + jnp.dot(part, place, preferred_element_type=F32)
    o_ref[0] = bias.astype(o_ref.dtype)


def _forget_bias(zf, bf_pad, n_heads):
    b, s, w = zf.shape
    ts = min(512, s)
    wo = n_heads * HEAD_DIM
    return pl.pallas_call(
        _fcum_kernel,
        grid=(b, s // ts),
        in_specs=[pl.BlockSpec((1, ts, w), lambda bi, i: (bi, i, 0)),
                  pl.BlockSpec((1, w), lambda bi, i: (0, 0))],
        out_specs=pl.BlockSpec((1, ts, wo), lambda bi, i: (bi, i, 0)),
        out_shape=jax.ShapeDtypeStruct((b, s, wo), BF16),
        scratch_shapes=[pltpu.VMEM((1, w), F32)],
        compiler_params=_cparams(("arbitrary", "arbitrary"), 8 * ts * ts * 4 + 8 * ts * wo * 4),
        name="forget_bias",
    )(zf, bf_pad)


ONES_ROWS = 16


MAX_CHAINS = 8
BIG_CHUNK = 4


def _attn_kernel(q_ref, k_ref, v_ref, fb_ref, o_ref, vt_ref, acc_ref, m_ref):
    tq = q_ref.shape[1]
    tk = tq
    s_len = k_ref.shape[1]
    qi = pl.program_id(2)

    @pl.when(qi == 0)
    def _():
        vt_ref[HEAD_DIM:, :] = jnp.ones((ONES_ROWS, s_len), BF16)
        for c in range(s_len // tk):
            vt_ref[:HEAD_DIM, c * tk:(c + 1) * tk] = v_ref[0, c * tk:(c + 1) * tk, :].astype(F32).T.astype(BF16)

    lane = lax.broadcasted_iota(jnp.int32, (tq, HEAD_DIM), 1)
    minus_one = jnp.where(lane < 3, -1.0, 0.0).astype(BF16)
    q_aug = jnp.concatenate([q_ref[0], minus_one], axis=1)
    m_ref[...] = jnp.full(m_ref.shape, MASK_VALUE, F32)
    acc_ref[...] = jnp.zeros_like(acc_ref)

    def logits(k0, size):
        k0 = pl.multiple_of(k0, tk)
        k_aug = jnp.concatenate([k_ref[0, pl.ds(k0, size), :], fb_ref[0, pl.ds(k0, size), :]], axis=1)
        return lax.dot_general(k_aug, q_aug, (((1,), (1,)), ((), ())), preferred_element_type=F32)

    def absorb(st, k0, size, masked):
        k0 = pl.multiple_of(k0, tk)
        if masked:
            krow = lax.broadcasted_iota(jnp.int32, (size, tq), 0)
            qcol = lax.broadcasted_iota(jnp.int32, (size, tq), 1)
            st = jnp.where(krow <= qcol, st, MASK_VALUE)
        part = jnp.max(st.reshape(MAX_CHAINS, size // MAX_CHAINS, tq), axis=1)
        m_old = m_ref[...]
        m_new = jnp.maximum(m_old, jnp.max(part, axis=0, keepdims=True))
        m_ref[...] = m_new
        pt = jnp.exp2((st - m_new).astype(BF16))
        acc_ref[...] = jnp.exp2(m_old - m_new) * acc_ref[...] + jnp.dot(
            vt_ref[:, pl.ds(k0, size)], pt, preferred_element_type=F32)

    def pair(k0, size_a, size_b, mask_b):
        st_a = logits(k0, size_a)
        st_b = logits(k0 + size_a, size_b)
        absorb(st_a, k0, size_a, False)
        absorb(st_b, k0 + size_a, size_b, mask_b)

    def body(kc, carry):
        half = BIG_CHUNK * tk // 2
        pair(kc * (BIG_CHUNK * tk), half, half, False)
        return carry

    n_big = qi // BIG_CHUNK
    lax.fori_loop(0, n_big, body, 0)
    tail_k0 = n_big * (BIG_CHUNK * tk)
    for visible in range(BIG_CHUNK):
        @pl.when(qi - n_big * BIG_CHUNK == visible)
        def _(visible=visible):
            if visible == 0:
                absorb(logits(tail_k0, tk), tail_k0, tk, True)
            else:
                pair(tail_k0, visible * tk, tk, True)

    o_t = acc_ref[:HEAD_DIM, :] * (1.0 / acc_ref[HEAD_DIM:HEAD_DIM + 1, :])
    o_ref[0] = o_t.T.astype(o_ref.dtype)


def _attention(zqkv, fbias, n_heads):
    b, s, _ = zqkv.shape
    tq = min(512, s)
    assert s % tq == 0
    vmem = 2 * (3 * s * HEAD_DIM * 2) + (HEAD_DIM + ONES_ROWS) * s * 2 + 6 * BIG_CHUNK * tq * tq * 4
    kv_spec = lambda off: pl.BlockSpec((1, s, HEAD_DIM), lambda bi, h, i: (bi, 0, off + h))
    return pl.pallas_call(
        _attn_kernel,
        grid=(b, n_heads, s // tq),
        in_specs=[pl.BlockSpec((1, tq, HEAD_DIM), lambda bi, h, i: (bi, i, h)),
                  kv_spec(n_heads), kv_spec(2 * n_heads),
                  pl.BlockSpec((1, s, HEAD_DIM), lambda bi, h, i: (bi, 0, h))],
        out_specs=pl.BlockSpec((1, tq, HEAD_DIM), lambda bi, h, i: (bi, i, h)),
        out_shape=jax.ShapeDtypeStruct((b, s, n_heads * HEAD_DIM), BF16),
        scratch_shapes=[pltpu.VMEM((HEAD_DIM + ONES_ROWS, s), BF16),
                        pltpu.VMEM((HEAD_DIM + ONES_ROWS, tq), F32),
                        pltpu.VMEM((1, tq), F32)],
        compiler_params=_cparams(("parallel", "parallel", "arbitrary"), vmem),
        name="fox_attention",
    )(zqkv, zqkv, zqkv, fbias)


def _pool_kernel(u_ref, halo_ref, w_ref, sc_ref, o_ref):
    tm = u_ref.shape[1]
    i = pl.program_id(1)
    halo = jnp.where(i > 0, halo_ref[0], 0.0)
    ext = jnp.concatenate([halo, u_ref[0]], axis=0)
    pos = (i * tm + 1 + lax.broadcasted_iota(jnp.int32, (tm, 1), 0)).astype(F32)
    gd = LANES
    for g, win in enumerate(POOL_WINDOWS):
        e = ext[:, g * gd:(g + 1) * gd]
        ssum = e
        shift = 1
        while shift < win:
            ssum = ssum + pltpu.roll(ssum, shift, 0)
            shift *= 2
        mean = ssum[POOL_HALO:] * (1.0 / jnp.minimum(pos, float(win)))
        dlt = mean - e[POOL_HALO:]
        y = jnp.dot(dlt.astype(BF16), w_ref[0, g], preferred_element_type=F32)
        o_ref[0, :, g * gd:(g + 1) * gd] = (y * sc_ref[:, g * gd:(g + 1) * gd]).astype(o_ref.dtype)


def _pool_mixer(z32, pool_w_stack, layer, scale_row, width):
    b, s, _ = z32.shape
    tm = min(512, s)
    hb = tm // POOL_HALO
    return pl.pallas_call(
        _pool_kernel,
        grid=(b, s // tm),
        in_specs=[pl.BlockSpec((1, tm, width), lambda bi, i: (bi, i, 0)),
                  pl.BlockSpec((1, POOL_HALO, width), lambda bi, i: (bi, jnp.maximum(i * hb - 1, 0), 0)),
                  pl.BlockSpec((1, N_POOL_GROUPS, LANES, LANES), lambda bi, i: (layer, 0, 0, 0)),
                  pl.BlockSpec((1, width), lambda bi, i: (0, 0))],
        out_specs=pl.BlockSpec((1, tm, width), lambda bi, i: (bi, i, 0)),
        out_shape=jax.ShapeDtypeStruct((b, s, width), BF16),
        compiler_params=_cparams(("parallel", "parallel"), 16 * tm * width * 4),
        name="pool_mixer",
    )(z32, z32, pool_w_stack, scale_row)


def _lru_kernel(x_ref, halo_ref, y_ref, cw_ref, cb_ref, wa_ref, ba_ref, wi_ref, bi_ref, lam_ref,
                o_ref, h_ref, a_s, b_s, h_s):
    tm = x_ref.shape[1]
    i = pl.program_id(1)

    @pl.when(i == 0)
    def _():
        h_ref[...] = jnp.zeros_like(h_ref)

    halo = jnp.where(i > 0, halo_ref[0], 0.0)
    ext = jnp.concatenate([halo, x_ref[0]], axis=0)
    xc = cb_ref[...] + cw_ref[LRU_CONV_WIDTH - 1:LRU_CONV_WIDTH, :] * ext[SUBLANES:]
    for k in range(LRU_CONV_WIDTH - 1):
        shifted = pltpu.roll(ext, LRU_CONV_WIDTH - 1 - k, 0)[SUBLANES:]
        xc = xc + cw_ref[k:k + 1, :] * shifted

    lam = lam_ref[...]
    neg_softplus = -(jnp.maximum(-lam, 0.0) + jnp.log(1.0 + jnp.exp(-jnp.abs(lam))))
    gd = LANES
    n_groups = tm // SUBLANES
    sub_row = lax.broadcasted_iota(jnp.int32, (n_groups, SUBLANES, gd), 1)
    for blk in range(N_LRU_BLOCKS):
        sl = slice(blk * gd, (blk + 1) * gd)
        xb = xc[:, sl]
        xb16 = xb.astype(BF16)
        gate_r = _sigmoid(jnp.dot(xb16, wa_ref[0, blk], preferred_element_type=F32) + ba_ref[:, sl])
        gate_i = _sigmoid(jnp.dot(xb16, wi_ref[0, blk], preferred_element_type=F32) + bi_ref[:, sl])
        log_a = LRU_C * gate_r * neg_softplus[:, sl]
        a = jnp.exp(log_a)
        b = jnp.sqrt(1.0 - a * a) * (gate_i * xb)
        a = a.reshape(n_groups, SUBLANES, gd)
        b = b.reshape(n_groups, SUBLANES, gd)
        dist = 1
        while dist < SUBLANES:
            has_prev = sub_row >= dist
            b = b + a * jnp.where(has_prev, pltpu.roll(b, dist, 1), 0.0)
            a = a * jnp.where(has_prev, pltpu.roll(a, dist, 1), 1.0)
            dist *= 2
        a_s[blk] = a
        b_s[blk] = b

    def group(gi, hs):
        out = []
        for blk in range(N_LRU_BLOCKS):
            rows = b_s[blk, gi] + a_s[blk, gi] * hs[blk]
            h_s[blk, gi] = rows
            out.append(rows[SUBLANES - 1:SUBLANES, :])
        return tuple(out)

    hs = lax.fori_loop(0, n_groups, group,
                       tuple(h_ref[:, blk * gd:(blk + 1) * gd] for blk in range(N_LRU_BLOCKS)), unroll=8)

    for blk in range(N_LRU_BLOCKS):
        sl = slice(blk * gd, (blk + 1) * gd)
        h_ref[:, sl] = hs[blk]
        y = y_ref[0, :, sl]
        gelu = 0.5 * y * (1.0 + jnp.tanh(0.7978845608028654 * (y + 0.044715 * (y * y * y))))
        o_ref[0, :, sl] = (h_s[blk].reshape(tm, gd) * gelu).astype(o_ref.dtype)


def _lru_mixer(z32, layer, p, width, x_col, y_col):
    b, s, _ = z32.shape
    tm = min(512, s)
    hb = tm // SUBLANES
    row = lambda bi, i: (0, 0)
    return pl.pallas_call(
        _lru_kernel,
        grid=(b, s // tm),
        in_specs=[pl.BlockSpec((1, tm, width), lambda bi, i: (bi, i, x_col)),
                  pl.BlockSpec((1, SUBLANES, width), lambda bi, i: (bi, jnp.maximum(i * hb - 1, 0), x_col)),
                  pl.BlockSpec((1, tm, width), lambda bi, i: (bi, i, y_col)),
                  pl.BlockSpec((LRU_CONV_WIDTH, width), row),
                  pl.BlockSpec((1, width), row),
                  pl.BlockSpec((1, N_LRU_BLOCKS, LANES, LANES), lambda bi, i: (layer, 0, 0, 0)),
                  pl.BlockSpec((1, width), row),
                  pl.BlockSpec((1, N_LRU_BLOCKS, LANES, LANES), lambda bi, i: (layer, 0, 0, 0)),
                  pl.BlockSpec((1, width), row),
                  pl.BlockSpec((1, width), row)],
        out_specs=pl.BlockSpec((1, tm, width), lambda bi, i: (bi, i, 0)),
        out_shape=jax.ShapeDtypeStruct((b, s, width), BF16),
        scratch_shapes=[pltpu.VMEM((1, width), F32)]
        + [pltpu.VMEM((N_LRU_BLOCKS, tm // SUBLANES, SUBLANES, LANES), F32)] * 3,
        compiler_params=_cparams(("arbitrary", "arbitrary"), 24 * tm * width * 4),
        name="rg_lru",
    )(z32, z32, z32, p["conv_w"], p["conv_b"], p["wa"], p["ba"], p["wi"], p["bi"], p["lam"])


COL_BLOCK = 512


def _col_blocks(d):
    return [slice(c, c + COL_BLOCK) for c in range(0, d, COL_BLOCK)]


def _residual_cols(acc, cols, x_ref, gt_ref, xn_refs):
    xn = x_ref[:, cols] + gt_ref[0, :, cols] * acc
    for ref in xn_refs:
        ref[:, cols] = xn
    return jnp.sum(xn * xn, axis=-1, keepdims=True)


def _norm_rows(sumsq, xn_ref, g_ref, sc_ref, sh_ref, h_ref):
    rs = lax.rsqrt(sumsq * (1.0 / xn_ref.shape[1]) + EPS)
    gain = g_ref[...] * (1.0 + sc_ref[0])
    h_ref[...] = (xn_ref[...] * rs * gain + sh_ref[0]).astype(h_ref.dtype)


def _mix_out_kernel(yp_ref, ya_ref, yl_ref, w_ref, x_ref, gt_ref, g_ref, sc_ref, sh_ref, xo_ref, h_ref):
    kp = yp_ref.shape[1]
    ka = ya_ref.shape[1]
    sumsq = jnp.zeros((x_ref.shape[0], 1), F32)
    for cols in _col_blocks(x_ref.shape[1]):
        acc = jnp.dot(yp_ref[...], w_ref[0, 0:kp, cols], preferred_element_type=F32)
        acc = acc + jnp.dot(ya_ref[...], w_ref[0, kp:kp + ka, cols], preferred_element_type=F32)
        acc = acc + jnp.dot(yl_ref[...], w_ref[0, kp + ka:, cols], preferred_element_type=F32)
        sumsq = sumsq + _residual_cols(acc, cols, x_ref, gt_ref, (xo_ref,))
    _norm_rows(sumsq, xo_ref, g_ref, sc_ref, sh_ref, h_ref)


def _mix_out(yp, ya, yl, w_stack, layer, x2, gt, g, sc, sh, seq):
    t, d = x2.shape
    tm = min(512, seq)
    tpb = seq // tm
    per_b = lambda i: (i // tpb, 0, 0)
    lhs = lambda y: pl.BlockSpec((tm, y.shape[1]), lambda i: (i, 0))
    vmem = 2 * d * d * 2 + 2 * (tm * d * 2 + 3 * tm * d * 4) + 3 * tm * d * 4
    return pl.pallas_call(
        _mix_out_kernel,
        grid=(t // tm,),
        in_specs=[lhs(yp), lhs(ya), lhs(yl),
                  pl.BlockSpec((1, d, d), lambda i: (layer, 0, 0)),
                  pl.BlockSpec((tm, d), lambda i: (i, 0)),
                  pl.BlockSpec((1, 1, d), per_b),
                  pl.BlockSpec((1, d), lambda i: (0, 0)),
                  pl.BlockSpec((1, 1, d), per_b),
                  pl.BlockSpec((1, 1, d), per_b)],
        out_specs=[pl.BlockSpec((tm, d), lambda i: (i, 0)), pl.BlockSpec((tm, d), lambda i: (i, 0))],
        out_shape=[jax.ShapeDtypeStruct((t, d), F32), jax.ShapeDtypeStruct((t, d), BF16)],
        compiler_params=_cparams(("parallel",), vmem),
        name="mix_out_proj",
    )(yp, ya, yl, w_stack, x2, gt, g, sc, sh)


def _ffn_down_kernel(a_ref, w_ref, x_ref, gt_ref, g_ref, sc_ref, sh_ref, *refs, emit_x, nk):
    out_refs, acc_ref = refs[:-1], refs[-1]
    h_ref = out_refs[-1]
    k = pl.program_id(1)
    last = nk - 1
    col_blocks = _col_blocks(x_ref.shape[1])

    def partial_product(cols):
        return jnp.dot(a_ref[...], w_ref[0, :, cols], preferred_element_type=F32)

    if nk > 1:
        @pl.when(k == 0)
        def _():
            for cols in col_blocks:
                acc_ref[:, cols] = partial_product(cols)

    if nk > 2:
        @pl.when((k > 0) & (k < last))
        def _():
            for cols in col_blocks:
                acc_ref[:, cols] += partial_product(cols)

    @pl.when(k == last)
    def _():
        sumsq = jnp.zeros((x_ref.shape[0], 1), F32)
        xn_refs = (acc_ref, out_refs[0]) if emit_x else (acc_ref,)
        for cols in col_blocks:
            acc = partial_product(cols)
            if nk > 1:
                acc = acc + acc_ref[:, cols]
            sumsq = sumsq + _residual_cols(acc, cols, x_ref, gt_ref, xn_refs)
        _norm_rows(sumsq, acc_ref, g_ref, sc_ref, sh_ref, h_ref)


def _ffn_down(act, w_stack, layer, x2, gt, g, sc, sh, seq, emit_x, h_dtype):
    t, d = x2.shape
    kf = act.shape[1]
    tm = min(512, seq)
    tk = _largest_tile(kf, 3072, 2 * LANES)
    tpb = seq // tm
    per_b = lambda i, k: (i // tpb, 0, 0)
    row_tile = pl.BlockSpec((tm, d), lambda i, k: (i, 0))
    out_specs = [row_tile]
    out_shape = [jax.ShapeDtypeStruct((t, d), h_dtype)]
    if emit_x:
        out_specs = [row_tile, row_tile]
        out_shape = [jax.ShapeDtypeStruct((t, d), F32)] + out_shape
    vmem = 2 * (tm * tk * 2 + tk * d * 2 + 3 * tm * d * 4) + 2 * tm * d * 4
    return pl.pallas_call(
        functools.partial(_ffn_down_kernel, emit_x=emit_x, nk=kf // tk),
        grid=(t // tm, kf // tk),
        in_specs=[pl.BlockSpec((tm, tk), lambda i, k: (i, k)),
                  pl.BlockSpec((1, tk, d), lambda i, k: (layer, k, 0)),
                  row_tile,
                  pl.BlockSpec((1, 1, d), per_b),
                  pl.BlockSpec((1, d), lambda i, k: (0, 0)),
                  pl.BlockSpec((1, 1, d), per_b),
                  pl.BlockSpec((1, 1, d), per_b)],
        out_specs=out_specs,
        out_shape=out_shape,
        scratch_shapes=[pltpu.VMEM((tm, d), F32)],
        compiler_params=_cparams(("parallel", "arbitrary"), vmem),
        name="ffn_down_proj",
    )(act, w_stack, x2, gt, g, sc, sh)


def _ffn_gate_kernel(h_ref, wg_ref, wu_ref, cw_ref, cb_ref, o_ref, carry_ref, *, tiles_per_batch):
    tm = h_ref.shape[0]
    i = pl.program_id(0)
    j = pl.program_id(1)

    @pl.when(i % tiles_per_batch == 0)
    def _():
        carry_ref[j] = jnp.zeros(carry_ref.shape[1:], F32)

    h = h_ref[...]
    u = jnp.dot(h, wg_ref[0], preferred_element_type=F32)
    ext = jnp.concatenate([carry_ref[j], u], axis=0)
    carry_ref[j] = u[tm - SUBLANES:, :]
    g = cb_ref[...] + cw_ref[FFN_CONV_WIDTH - 1:FFN_CONV_WIDTH, :] * u
    for k in range(FFN_CONV_WIDTH - 1):
        shifted = pltpu.roll(ext, FFN_CONV_WIDTH - 1 - k, 0)[SUBLANES:]
        g = g + cw_ref[k:k + 1, :] * shifted
    up = jnp.dot(h, wu_ref[0], preferred_element_type=F32)
    o_ref[...] = (g * _sigmoid(g) * up).astype(o_ref.dtype)


def _ffn_gate(h, wg_stack, wu_stack, layer, conv_w, conv_b, seq):
    t, d = h.shape
    kf = wg_stack.shape[2]
    tm = min(1024, seq)
    tf = 512
    wspec = pl.BlockSpec((1, d, tf), lambda i, j: (layer, 0, j))
    vmem = 2 * (tm * d * 2 + 2 * d * tf * 2 + tm * tf * 2) + 8 * tm * tf * 4
    return pl.pallas_call(
        functools.partial(_ffn_gate_kernel, tiles_per_batch=seq // tm),
        grid=(t // tm, kf // tf),
        in_specs=[pl.BlockSpec((tm, d), lambda i, j: (i, 0)), wspec, wspec,
                  pl.BlockSpec((FFN_CONV_WIDTH, tf), lambda i, j: (0, j)),
                  pl.BlockSpec((1, tf), lambda i, j: (0, j))],
        out_specs=pl.BlockSpec((tm, tf), lambda i, j: (i, j)),
        out_shape=jax.ShapeDtypeStruct((t, kf), BF16),
        scratch_shapes=[pltpu.VMEM((kf // tf, SUBLANES, tf), F32)],
        compiler_params=_cparams(("arbitrary", "arbitrary"), vmem),
        name="ffn_gate_up",
    )(h, wg_stack, wu_stack, conv_w, conv_b)


def kernel(x, c, w_ada, b_ada, g_mix, w_in, b_f, pool_w, pool_scale, lru_conv_w, lru_conv_b, lru_wa, lru_ba,
           lru_wi, lru_bi, lru_lambda, w_out, g_ffn, w_ffn_gate, w_ffn_up, ffn_conv_w, ffn_conv_b, w_ffn_down,
           final_g):
    batch, seq, d = x.shape
    depth = w_ada.shape[0]
    pool_width = pool_w.shape[1] * pool_w.shape[2]
    lru_width = lru_lambda.shape[1]
    n_heads = b_f.shape[1]
    attn_width = n_heads * HEAD_DIM
    assert w_in.shape[2] == pool_width + 3 * attn_width + n_heads + 2 * lru_width
    assert pool_width == lru_width == N_POOL_GROUPS * LANES and seq % SUBLANES == 0

    o_q = pool_width
    o_f = o_q + 3 * attn_width
    o_x = o_f + n_heads
    w_qkv = w_in[:, :, o_q:o_f].astype(BF16)
    w_pxy = jnp.concatenate([w_in[:, :, :o_q], w_in[:, :, o_x:]], axis=2).astype(BF16)
    w_f = jnp.pad(w_in[:, :, o_f:o_x], ((0, 0), (0, 0), (0, LANES - n_heads))).astype(BF16)
    bf_pad = jnp.pad(b_f, ((0, 0), (0, LANES - n_heads)))
    qkv_scale = jnp.concatenate([jnp.full((1, attn_width), HEAD_DIM ** -0.5 * LOG2E, F32),
                                 jnp.ones((1, 2 * attn_width), F32)], axis=1)
    pool_w16 = pool_w.astype(BF16)
    lru_wa16 = lru_wa.astype(BF16)
    lru_wi16 = lru_wi.astype(BF16)
    w_out16 = w_out.astype(BF16)
    w_gate16 = w_ffn_gate.astype(BF16)
    w_up16 = w_ffn_up.astype(BF16)
    w_down16 = w_ffn_down.astype(BF16)

    mod = _ada_mod(c, w_ada, b_ada)[:, :batch]

    def mod_chunk(layer, idx):
        return mod[layer, :, idx * d:(idx + 1) * d].reshape(batch, 1, d)

    x2 = x.reshape(batch * seq, d)
    h = _norm(x2, g_mix[0][None], mod_chunk(0, 1), mod_chunk(0, 0), seq)
    out = None
    for layer in range(depth):
        sh1, sc1, gt1, sh2, sc2, gt2 = (mod_chunk(layer, idx) for idx in range(6))
        zqkv = _matmul(h, w_qkv, layer, BF16, 1024, qkv_scale).reshape(batch, seq, 3 * attn_width)
        z32 = _matmul(h, w_pxy, layer, F32, 512).reshape(batch, seq, pool_width + 2 * lru_width)
        zf = _matmul(h, w_f, layer, F32, LANES).reshape(batch, seq, LANES)

        fbias = _forget_bias(zf, bf_pad[layer][None], n_heads)
        y_attn = _attention(zqkv, fbias, n_heads).reshape(batch * seq, attn_width)
        y_pool = _pool_mixer(z32, pool_w16, layer, pool_scale[layer][None], pool_width)
        lru_p = dict(conv_w=lru_conv_w[layer], conv_b=lru_conv_b[layer][None], wa=lru_wa16,
                     ba=lru_ba[layer][None], wi=lru_wi16, bi=lru_bi[layer][None], lam=lru_lambda[layer][None])
        y_lru = _lru_mixer(z32, layer, lru_p, lru_width, 1, 2)

        x2, h = _mix_out(y_pool.reshape(batch * seq, pool_width), y_attn, y_lru.reshape(batch * seq, lru_width),
                         w_out16, layer, x2, gt1, g_ffn[layer][None], sc2, sh2, seq)
        act = _ffn_gate(h, w_gate16, w_up16, layer, ffn_conv_w[layer], ffn_conv_b[layer][None], seq)
        if layer + 1 < depth:
            x2, h = _ffn_down(act, w_down16, layer, x2, gt2, g_mix[layer + 1][None], mod_chunk(layer + 1, 1),
                              mod_chunk(layer + 1, 0), seq, True, BF16)
        else:
            zeros = jnp.zeros((batch, 1, d), F32)
            (out,) = _ffn_down(act, w_down16, layer, x2, gt2, final_g[None], zeros, zeros, seq, False, F32)
    return out.reshape(batch, seq, d)
```

```python
import functools

import jax
import jax.numpy as jnp
from jax import lax
from jax.experimental import pallas as pl
from jax.experimental.pallas import tpu as pltpu

F32 = jnp.float32
BF16 = jnp.bfloat16

HEAD_DIM = 128
N_POOL_GROUPS = 4
POOL_WINDOWS = (2, 4, 8, 16)
POOL_HALO = 16
N_LRU_BLOCKS = 4
LRU_CONV_WIDTH = 4
LRU_C = 8.0
FFN_CONV_WIDTH = 3
EPS = 1e-6

LANES = 128
SUBLANES = 8
VMEM_LIMIT_CAP_V7X = 58 * 2**20
MASK_VALUE = -1e30
LOG2E = 1.4426950408889634


def _cparams(semantics, vmem_bytes):
    return pltpu.CompilerParams(dimension_semantics=semantics,
                                vmem_limit_bytes=int(min(VMEM_LIMIT_CAP_V7X, max(vmem_bytes, 16 * 2**20))))


def _sigmoid(x):
    return 0.5 * jnp.tanh(0.5 * x) + 0.5


def _norm_mod(x, g, sc, sh):
    ms = jnp.mean(x * x, axis=-1, keepdims=True)
    return (x * lax.rsqrt(ms + EPS) * g) * (1.0 + sc) + sh


def _split3_bf16(x):
    hi = x.astype(BF16)
    r1 = x - hi.astype(F32)
    mid = r1.astype(BF16)
    lo = (r1 - mid.astype(F32)).astype(BF16)
    return hi, mid, lo


def _ada_kernel(c_ref, w_ref, b_ref, o_ref, acc_ref):
    nb = c_ref.shape[0]
    n = w_ref.shape[3]
    k = pl.program_id(1)

    @pl.when(k == 0)
    def _():
        acc_ref[...] = jnp.zeros_like(acc_ref)

    ca = [c_ref[b] * _sigmoid(c_ref[b]) for b in range(nb)]

    def lane_chunk(jc, carry):
        sl = pl.ds(pl.multiple_of(jc * LANES, LANES), LANES)
        w = w_ref[0, :, :, sl]
        for b in range(nb):
            acc_ref[b, :, sl] += jnp.sum(w * ca[b], axis=0)
        return carry

    lax.fori_loop(0, n // LANES, lane_chunk, 0, unroll=4)

    @pl.when(k == pl.num_programs(1) - 1)
    def _():
        o_ref[...] = jnp.zeros_like(o_ref)
        for b in range(nb):
            o_ref[0, b:b + 1, :] = jnp.sum(acc_ref[b], axis=0, keepdims=True) + b_ref[0]


ADA_SLAB_ROWS = 256


def _ada_mod(c, w_ada, b_ada):
    depth, d, n = w_ada.shape
    batch = c.shape[0]
    dr = d // SUBLANES
    sr = ADA_SLAB_ROWS // SUBLANES
    c_lanes = jnp.broadcast_to(c[:, :, None], (batch, d, LANES)).reshape(batch, dr, SUBLANES, LANES)
    return pl.pallas_call(
        _ada_kernel,
        grid=(depth, dr // sr),
        in_specs=[pl.BlockSpec((batch, sr, SUBLANES, LANES), lambda l, k: (0, k, 0, 0)),
                  pl.BlockSpec((1, sr, SUBLANES, n), lambda l, k: (l, k, 0, 0)),
                  pl.BlockSpec((1, 1, n), lambda l, k: (l, 0, 0))],
        out_specs=pl.BlockSpec((1, SUBLANES, n), lambda l, k: (l, 0, 0)),
        out_shape=jax.ShapeDtypeStruct((depth, SUBLANES, n), F32),
        scratch_shapes=[pltpu.VMEM((batch, SUBLANES, n), F32)],
        compiler_params=_cparams(("parallel", "arbitrary"), 3 * ADA_SLAB_ROWS * n * 4),
        name="ada_mod",
    )(c_lanes, w_ada.reshape(depth, dr, SUBLANES, n), b_ada.reshape(depth, 1, n))


def _norm_kernel(x_ref, g_ref, sc_ref, sh_ref, h_ref):
    h_ref[...] = _norm_mod(x_ref[...], g_ref[...], sc_ref[0], sh_ref[0]).astype(h_ref.dtype)


def _norm(x2, g, sc, sh, seq):
    t, d = x2.shape
    tm = min(512, seq)
    tpb = seq // tm
    return pl.pallas_call(
        _norm_kernel,
        grid=(t // tm,),
        in_specs=[pl.BlockSpec((tm, d), lambda i: (i, 0)),
                  pl.BlockSpec((1, d), lambda i: (0, 0)),
                  pl.BlockSpec((1, 1, d), lambda i: (i // tpb, 0, 0)),
                  pl.BlockSpec((1, 1, d), lambda i: (i // tpb, 0, 0))],
        out_specs=pl.BlockSpec((tm, d), lambda i: (i, 0)),
        out_shape=jax.ShapeDtypeStruct((t, d), BF16),
        compiler_params=_cparams(("parallel",), 6 * tm * d * 4),
        name="norm_mod",
    )(x2, g, sc, sh)


def _mm_kernel(a_ref, w_ref, o_ref):
    o_ref[...] = jnp.dot(a_ref[...], w_ref[0], preferred_element_type=F32).astype(o_ref.dtype)


def _mm_scaled_kernel(a_ref, w_ref, cs_ref, o_ref):
    acc = jnp.dot(a_ref[...], w_ref[0], preferred_element_type=F32)
    o_ref[...] = (acc * cs_ref[...]).astype(o_ref.dtype)


def _matmul(a, w_stack, layer, out_dtype, tn, col_scale=None):
    t, k = a.shape
    n = w_stack.shape[2]
    tm = min(1024, t)
    in_specs = [pl.BlockSpec((tm, k), lambda i, j: (i, 0)),
                pl.BlockSpec((1, k, tn), lambda i, j: (layer, 0, j))]
    args = [a, w_stack]
    body = _mm_kernel
    if col_scale is not None:
        in_specs.append(pl.BlockSpec((1, tn), lambda i, j: (0, j)))
        args.append(col_scale)
        body = _mm_scaled_kernel
    vmem = 2 * (tm * k * 2 + k * tn * 2 + tm * tn * 4) + 2 * tm * tn * 4
    return pl.pallas_call(
        body,
        grid=(t // tm, n // tn),
        in_specs=in_specs,
        out_specs=pl.BlockSpec((tm, tn), lambda i, j: (i, j)),
        out_shape=jax.ShapeDtypeStruct((t, n), out_dtype),
        compiler_params=_cparams(("parallel", "parallel"), vmem),
        name="matmul_dmodel",
    )(*args)


N_SPLIT = 3


def _fcum_kernel(z_ref, b_ref, o_ref, carry_ref, tri_ref, place_ref):
    ts = z_ref.shape[1]
    w = z_ref.shape[2]
    wo = o_ref.shape[2]

    @pl.when((pl.program_id(0) == 0) & (pl.program_id(1) == 0))
    def _():
        r = lax.broadcasted_iota(jnp.int32, (ts, ts), 0)
        c = lax.broadcasted_iota(jnp.int32, (ts, ts), 1)
        tri_ref[...] = jnp.where(c <= r, 1.0, 0.0).astype(BF16)
        row = lax.broadcasted_iota(jnp.int32, (N_SPLIT * w, wo), 0)
        lane = lax.broadcasted_iota(jnp.int32, (N_SPLIT * w, wo), 1)
        target = jnp.zeros_like(row)
        for idx in range(N_SPLIT):
            in_part = (row >= idx * w) & (row < (idx + 1) * w)
            target = jnp.where(in_part, (row - idx * w) * HEAD_DIM + idx, target)
        place_ref[...] = jnp.where(lane == target, 1.0, 0.0).astype(BF16)

    @pl.when(pl.program_id(1) == 0)
    def _():
        carry_ref[...] = jnp.zeros_like(carry_ref)

    x = z_ref[0] + b_ref[...]
    lf = jnp.minimum(x, 0.0) - jnp.log(1.0 + jnp.exp(-jnp.abs(x)))
    parts = jnp.dot(tri_ref[...], jnp.concatenate(_split3_bf16(lf), axis=1), preferred_element_type=F32)
    cs = carry_ref[...] + parts[:, :w]
    for idx in range(1, N_SPLIT):
        cs = cs + parts[:, idx * w:(idx + 1) * w]
    carry_ref[...] = cs[ts - 1:ts, :]

    split = jnp.concatenate(_split3_bf16(cs * LOG2E), axis=1)
    o_ref[0] = jnp.dot(split, place_ref[...], preferred_element_type=F32).astype(o_ref.dtype)


def _forget_bias(z32, f_col, bf_pad, n_heads):
    b, s, _ = z32.shape
    w = LANES
    ts = min(512, s)
    wo = n_heads * HEAD_DIM
    return pl.pallas_call(
        _fcum_kernel,
        grid=(b, s // ts),
        in_specs=[pl.BlockSpec((1, ts, w), lambda bi, i: (bi, i, f_col)),
                  pl.BlockSpec((1, w), lambda bi, i: (0, 0))],
        out_specs=pl.BlockSpec((1, ts, wo), lambda bi, i: (bi, i, 0)),
        out_shape=jax.ShapeDtypeStruct((b, s, wo), BF16),
        scratch_shapes=[pltpu.VMEM((1, w), F32), pltpu.VMEM((ts, ts), BF16), pltpu.VMEM((N_SPLIT * w, wo), BF16)],
        compiler_params=_cparams(("arbitrary", "arbitrary"), 4 * ts * ts * 4 + 8 * ts * wo * 4),
        name="forget_bias",
    )(z32, bf_pad)


ONES_ROWS = 16


MAX_CHAINS = 8
BIG_CHUNK = 4


def _attn_kernel(q_ref, k_ref, v_ref, fb_ref, o_ref, vt_ref, acc_ref, m_ref):
    tq = q_ref.shape[1]
    tk = tq
    s_len = k_ref.shape[1]
    qi = pl.program_id(2)

    @pl.when(qi == 0)
    def _():
        vt_ref[HEAD_DIM:, :] = jnp.ones((ONES_ROWS, s_len), BF16)
        for c in range(s_len // tk):
            vt_ref[:HEAD_DIM, c * tk:(c + 1) * tk] = v_ref[0, c * tk:(c + 1) * tk, :].astype(F32).T.astype(BF16)

    lane = lax.broadcasted_iota(jnp.int32, (tq, HEAD_DIM), 1)
    minus_one = jnp.where(lane < 3, -1.0, 0.0).astype(BF16)
    q_aug = jnp.concatenate([q_ref[0], minus_one], axis=1)
    m_ref[...] = jnp.full(m_ref.shape, MASK_VALUE, F32)
    acc_ref[...] = jnp.zeros_like(acc_ref)

    def logits(k0, size):
        k0 = pl.multiple_of(k0, tk)
        k_aug = jnp.concatenate([k_ref[0, pl.ds(k0, size), :], fb_ref[0, pl.ds(k0, size), :]], axis=1)
        return lax.dot_general(k_aug, q_aug, (((1,), (1,)), ((), ())), preferred_element_type=F32)

    def absorb(st, k0, size, masked):
        k0 = pl.multiple_of(k0, tk)
        if masked:
            krow = lax.broadcasted_iota(jnp.int32, (size, tq), 0)
            qcol = lax.broadcasted_iota(jnp.int32, (size, tq), 1)
            st = jnp.where(krow <= qcol, st, MASK_VALUE)
        part = jnp.max(st.reshape(MAX_CHAINS, size // MAX_CHAINS, tq), axis=1)
        m_old = m_ref[...]
        m_new = jnp.maximum(m_old, jnp.max(part, axis=0, keepdims=True))
        m_ref[...] = m_new
        pt = jnp.exp2((st - m_new).astype(BF16))
        acc_ref[...] = jnp.exp2(m_old - m_new) * acc_ref[...] + jnp.dot(
            vt_ref[:, pl.ds(k0, size)], pt, preferred_element_type=F32)

    def pair(k0, size_a, size_b, mask_b):
        st_a = logits(k0, size_a)
        st_b = logits(k0 + size_a, size_b)
        absorb(st_a, k0, size_a, False)
        absorb(st_b, k0 + size_a, size_b, mask_b)

    def body(kc, carry):
        half = BIG_CHUNK * tk // 2
        pair(kc * (BIG_CHUNK * tk), half, half, False)
        return carry

    n_big = qi // BIG_CHUNK
    lax.fori_loop(0, n_big, body, 0)
    tail_k0 = n_big * (BIG_CHUNK * tk)
    for visible in range(BIG_CHUNK):
        @pl.when(qi - n_big * BIG_CHUNK == visible)
        def _(visible=visible):
            if visible == 0:
                absorb(logits(tail_k0, tk), tail_k0, tk, True)
            else:
                pair(tail_k0, visible * tk, tk, True)

    o_t = acc_ref[:HEAD_DIM, :] * (1.0 / acc_ref[HEAD_DIM:HEAD_DIM + 1, :])
    o_ref[0] = o_t.T.astype(o_ref.dtype)


def _attention(zqkv, fbias, n_heads):
    b, s, _ = zqkv.shape
    tq = min(512, s)
    assert s % tq == 0
    vmem = 2 * (3 * s * HEAD_DIM * 2) + (HEAD_DIM + ONES_ROWS) * s * 2 + 6 * BIG_CHUNK * tq * tq * 4
    kv_spec = lambda off: pl.BlockSpec((1, s, HEAD_DIM), lambda bi, h, i: (bi, 0, off + h))
    return pl.pallas_call(
        _attn_kernel,
        grid=(b, n_heads, s // tq),
        in_specs=[pl.BlockSpec((1, tq, HEAD_DIM), lambda bi, h, i: (bi, i, h)),
                  kv_spec(n_heads), kv_spec(2 * n_heads),
                  pl.BlockSpec((1, s, HEAD_DIM), lambda bi, h, i: (bi, 0, h))],
        out_specs=pl.BlockSpec((1, tq, HEAD_DIM), lambda bi, h, i: (bi, i, h)),
        out_shape=jax.ShapeDtypeStruct((b, s, n_heads * HEAD_DIM), BF16),
        scratch_shapes=[pltpu.VMEM((HEAD_DIM + ONES_ROWS, s), BF16),
                        pltpu.VMEM((HEAD_DIM + ONES_ROWS, tq), F32),
                        pltpu.VMEM((1, tq), F32)],
        compiler_params=_cparams(("parallel", "parallel", "arbitrary"), vmem),
        name="fox_attention",
    )(zqkv, zqkv, zqkv, fbias)


def _pool_kernel(u_ref, halo_ref, w_ref, sc_ref, o_ref):
    tm = u_ref.shape[1]
    i = pl.program_id(1)
    halo = jnp.where(i > 0, halo_ref[0], 0.0)
    ext = jnp.concatenate([halo, u_ref[0]], axis=0)
    pos = (i * tm + 1 + lax.broadcasted_iota(jnp.int32, (tm, 1), 0)).astype(F32)
    gd = LANES
    for g, win in enumerate(POOL_WINDOWS):
        e = ext[:, g * gd:(g + 1) * gd]
        ssum = e
        shift = 1
        while shift < win:
            ssum = ssum + pltpu.roll(ssum, shift, 0)
            shift *= 2
        mean = ssum[POOL_HALO:] * (1.0 / jnp.minimum(pos, float(win)))
        dlt = mean - e[POOL_HALO:]
        y = jnp.dot(dlt.astype(BF16), w_ref[0, g], preferred_element_type=F32)
        o_ref[0, :, g * gd:(g + 1) * gd] = (y * sc_ref[:, g * gd:(g + 1) * gd]).astype(o_ref.dtype)


def _pool_mixer(z32, pool_w_stack, layer, scale_row, width):
    b, s, _ = z32.shape
    tm = min(512, s)
    hb = tm // POOL_HALO
    return pl.pallas_call(
        _pool_kernel,
        grid=(b, s // tm),
        in_specs=[pl.BlockSpec((1, tm, width), lambda bi, i: (bi, i, 0)),
                  pl.BlockSpec((1, POOL_HALO, width), lambda bi, i: (bi, jnp.maximum(i * hb - 1, 0), 0)),
                  pl.BlockSpec((1, N_POOL_GROUPS, LANES, LANES), lambda bi, i: (layer, 0, 0, 0)),
                  pl.BlockSpec((1, width), lambda bi, i: (0, 0))],
        out_specs=pl.BlockSpec((1, tm, width), lambda bi, i: (bi, i, 0)),
        out_shape=jax.ShapeDtypeStruct((b, s, width), BF16),
        compiler_params=_cparams(("parallel", "parallel"), 16 * tm * width * 4),
        name="pool_mixer",
    )(z32, z32, pool_w_stack, scale_row)


def _lru_kernel(x_ref, halo_ref, y_ref, cw_ref, cb_ref, wa_ref, ba_ref, wi_ref, bi_ref, lam_ref,
                o_ref, h_ref, a_s, b_s, h_s):
    tm = x_ref.shape[1]
    i = pl.program_id(1)

    @pl.when(i == 0)
    def _():
        h_ref[...] = jnp.zeros_like(h_ref)

    halo = jnp.where(i > 0, halo_ref[0], 0.0)
    ext = jnp.concatenate([halo, x_ref[0]], axis=0)
    xc = cb_ref[...] + cw_ref[LRU_CONV_WIDTH - 1:LRU_CONV_WIDTH, :] * ext[SUBLANES:]
    for k in range(LRU_CONV_WIDTH - 1):
        shifted = pltpu.roll(ext, LRU_CONV_WIDTH - 1 - k, 0)[SUBLANES:]
        xc = xc + cw_ref[k:k + 1, :] * shifted

    lam = lam_ref[...]
    neg_softplus = -(jnp.maximum(-lam, 0.0) + jnp.log(1.0 + jnp.exp(-jnp.abs(lam))))
    gd = LANES
    n_groups = tm // SUBLANES
    sub_row = lax.broadcasted_iota(jnp.int32, (n_groups, SUBLANES, gd), 1)
    for blk in range(N_LRU_BLOCKS):
        sl = slice(blk * gd, (blk + 1) * gd)
        xb = xc[:, sl]
        xb16 = xb.astype(BF16)
        gate_r = _sigmoid(jnp.dot(xb16, wa_ref[0, blk], preferred_element_type=F32) + ba_ref[:, sl])
        gate_i = _sigmoid(jnp.dot(xb16, wi_ref[0, blk], preferred_element_type=F32) + bi_ref[:, sl])
        log_a = LRU_C * gate_r * neg_softplus[:, sl]
        a = jnp.exp(log_a)
        b = jnp.sqrt(1.0 - a * a) * (gate_i * xb)
        a = a.reshape(n_groups, SUBLANES, gd)
        b = b.reshape(n_groups, SUBLANES, gd)
        dist = 1
        while dist < SUBLANES:
            has_prev = sub_row >= dist
            b = b + a * jnp.where(has_prev, pltpu.roll(b, dist, 1), 0.0)
            a = a * jnp.where(has_prev, pltpu.roll(a, dist, 1), 1.0)
            dist *= 2
        a_s[blk] = a
        b_s[blk] = b

    def group(gi, hs):
        out = []
        for blk in range(N_LRU_BLOCKS):
            rows = b_s[blk, gi] + a_s[blk, gi] * hs[blk]
            h_s[blk, gi] = rows
            out.append(rows[SUBLANES - 1:SUBLANES, :])
        return tuple(out)

    hs = lax.fori_loop(0, n_groups, group,
                       tuple(h_ref[:, blk * gd:(blk + 1) * gd] for blk in range(N_LRU_BLOCKS)), unroll=8)

    for blk in range(N_LRU_BLOCKS):
        sl = slice(blk * gd, (blk + 1) * gd)
        h_ref[:, sl] = hs[blk]
        y = y_ref[0, :, sl]
        gelu = 0.5 * y * (1.0 + jnp.tanh(0.7978845608028654 * (y + 0.044715 * (y * y * y))))
        o_ref[0, :, sl] = (h_s[blk].reshape(tm, gd) * gelu).astype(o_ref.dtype)


def _lru_mixer(z32, layer, p, width, x_col, y_col):
    b, s, _ = z32.shape
    tm = min(512, s)
    hb = tm // SUBLANES
    row = lambda bi, i: (0, 0)
    return pl.pallas_call(
        _lru_kernel,
        grid=(b, s // tm),
        in_specs=[pl.BlockSpec((1, tm, width), lambda bi, i: (bi, i, x_col)),
                  pl.BlockSpec((1, SUBLANES, width), lambda bi, i: (bi, jnp.maximum(i * hb - 1, 0), x_col)),
                  pl.BlockSpec((1, tm, width), lambda bi, i: (bi, i, y_col)),
                  pl.BlockSpec((LRU_CONV_WIDTH, width), row),
                  pl.BlockSpec((1, width), row),
                  pl.BlockSpec((1, N_LRU_BLOCKS, LANES, LANES), lambda bi, i: (layer, 0, 0, 0)),
                  pl.BlockSpec((1, width), row),
                  pl.BlockSpec((1, N_LRU_BLOCKS, LANES, LANES), lambda bi, i: (layer, 0, 0, 0)),
                  pl.BlockSpec((1, width), row),
                  pl.BlockSpec((1, width), row)],
        out_specs=pl.BlockSpec((1, tm, width), lambda bi, i: (bi, i, 0)),
        out_shape=jax.ShapeDtypeStruct((b, s, width), BF16),
        scratch_shapes=[pltpu.VMEM((1, width), F32)]
        + [pltpu.VMEM((N_LRU_BLOCKS, tm // SUBLANES, SUBLANES, LANES), F32)] * 3,
        compiler_params=_cparams(("arbitrary", "arbitrary"), 24 * tm * width * 4),
        name="rg_lru",
    )(z32, z32, z32, p["conv_w"], p["conv_b"], p["wa"], p["ba"], p["wi"], p["bi"], p["lam"])


COL_BLOCK = 512


def _col_blocks(d):
    return [slice(c, c + COL_BLOCK) for c in range(0, d, COL_BLOCK)]


def _residual_cols(acc, cols, x_ref, gt_ref, xn_refs):
    xn = x_ref[:, cols] + gt_ref[0, :, cols] * acc
    for ref in xn_refs:
        ref[:, cols] = xn
    return jnp.sum(xn * xn, axis=-1, keepdims=True)


def _norm_rows(sumsq, xn_ref, g_ref, sc_ref, sh_ref, h_ref):
    rs = lax.rsqrt(sumsq * (1.0 / xn_ref.shape[1]) + EPS)
    gain = g_ref[...] * (1.0 + sc_ref[0])
    h_ref[...] = (xn_ref[...] * rs * gain + sh_ref[0]).astype(h_ref.dtype)


def _mix_out_kernel(yp_ref, ya_ref, yl_ref, w_ref, x_ref, gt_ref, g_ref, sc_ref, sh_ref, xo_ref, h_ref):
    kp = yp_ref.shape[1]
    ka = ya_ref.shape[1]
    sumsq = jnp.zeros((x_ref.shape[0], 1), F32)
    for cols in _col_blocks(x_ref.shape[1]):
        acc = jnp.dot(yp_ref[...], w_ref[0, 0:kp, cols], preferred_element_type=F32)
        acc = acc + jnp.dot(ya_ref[...], w_ref[0, kp:kp + ka, cols], preferred_element_type=F32)
        acc = acc + jnp.dot(yl_ref[...], w_ref[0, kp + ka:, cols], preferred_element_type=F32)
        sumsq = sumsq + _residual_cols(acc, cols, x_ref, gt_ref, (xo_ref,))
    _norm_rows(sumsq, xo_ref, g_ref, sc_ref, sh_ref, h_ref)


def _mix_out(yp, ya, yl, w_stack, layer, x2, gt, g, sc, sh, seq):
    t, d = x2.shape
    tm = min(512, seq)
    tpb = seq // tm
    per_b = lambda i: (i // tpb, 0, 0)
    lhs = lambda y: pl.BlockSpec((tm, y.shape[1]), lambda i: (i, 0))
    vmem = 2 * d * d * 2 + 2 * (tm * d * 2 + 3 * tm * d * 4) + 3 * tm * d * 4
    return pl.pallas_call(
        _mix_out_kernel,
        grid=(t // tm,),
        in_specs=[lhs(yp), lhs(ya), lhs(yl),
                  pl.BlockSpec((1, d, d), lambda i: (layer, 0, 0)),
                  pl.BlockSpec((tm, d), lambda i: (i, 0)),
                  pl.BlockSpec((1, 1, d), per_b),
                  pl.BlockSpec((1, d), lambda i: (0, 0)),
                  pl.BlockSpec((1, 1, d), per_b),
                  pl.BlockSpec((1, 1, d), per_b)],
        out_specs=[pl.BlockSpec((tm, d), lambda i: (i, 0)), pl.BlockSpec((tm, d), lambda i: (i, 0))],
        out_shape=[jax.ShapeDtypeStruct((t, d), F32), jax.ShapeDtypeStruct((t, d), BF16)],
        compiler_params=_cparams(("parallel",), vmem),
        name="mix_out_proj",
    )(yp, ya, yl, w_stack, x2, gt, g, sc, sh)


def _ffn_down_kernel(a_ref, w_ref, x_ref, gt_ref, g_ref, sc_ref, sh_ref, *out_refs):
    xn_ref, h_ref = out_refs[0], out_refs[-1]
    sumsq = jnp.zeros((x_ref.shape[0], 1), F32)
    for cols in _col_blocks(x_ref.shape[1]):
        acc = jnp.dot(a_ref[...], w_ref[0, :, cols], preferred_element_type=F32)
        sumsq = sumsq + _residual_cols(acc, cols, x_ref, gt_ref, (xn_ref,))
    _norm_rows(sumsq, xn_ref, g_ref, sc_ref, sh_ref, h_ref)


def _ffn_down(act, w_stack, layer, x2, gt, g, sc, sh, seq, emit_x, h_dtype):
    t, d = x2.shape
    kf = act.shape[1]
    tm = min(512, seq)
    tpb = seq // tm
    per_b = lambda i: (i // tpb, 0, 0)
    row_tile = pl.BlockSpec((tm, d), lambda i: (i, 0))
    out_specs = [row_tile]
    out_shape = [jax.ShapeDtypeStruct((t, d), h_dtype)]
    if emit_x:
        out_specs = [row_tile, row_tile]
        out_shape = [jax.ShapeDtypeStruct((t, d), F32)] + out_shape
    else:
        assert h_dtype == F32
    vmem = kf * d * 2 + 2 * (tm * kf * 2 + 3 * tm * d * 4) + 2 * tm * COL_BLOCK * 4
    return pl.pallas_call(
        _ffn_down_kernel,
        grid=(t // tm,),
        in_specs=[pl.BlockSpec((tm, kf), lambda i: (i, 0)),
                  pl.BlockSpec((1, kf, d), lambda i: (layer, 0, 0), pipeline_mode=pl.Buffered(1)),
                  row_tile,
                  pl.BlockSpec((1, 1, d), per_b),
                  pl.BlockSpec((1, d), lambda i: (0, 0)),
                  pl.BlockSpec((1, 1, d), per_b),
                  pl.BlockSpec((1, 1, d), per_b)],
        out_specs=out_specs,
        out_shape=out_shape,
        compiler_params=_cparams(("parallel",), vmem),
        name="ffn_down_proj",
    )(act, w_stack, x2, gt, g, sc, sh)


def _ffn_gate_kernel(h_ref, wg_ref, wu_ref, cw_ref, cb_ref, o_ref, carry_ref, *, tiles_per_batch):
    tm = h_ref.shape[0]
    i = pl.program_id(0)
    j = pl.program_id(1)

    @pl.when(i % tiles_per_batch == 0)
    def _():
        carry_ref[j] = jnp.zeros(carry_ref.shape[1:], F32)

    h = h_ref[...]
    u = jnp.dot(h, wg_ref[0], preferred_element_type=F32)
    ext = jnp.concatenate([carry_ref[j], u], axis=0)
    carry_ref[j] = u[tm - SUBLANES:, :]
    g = cb_ref[...] + cw_ref[FFN_CONV_WIDTH - 1:FFN_CONV_WIDTH, :] * u
    for k in range(FFN_CONV_WIDTH - 1):
        shifted = pltpu.roll(ext, FFN_CONV_WIDTH - 1 - k, 0)[SUBLANES:]
        g = g + cw_ref[k:k + 1, :] * shifted
    up = jnp.dot(h, wu_ref[0], preferred_element_type=F32)
    o_ref[...] = (g * _sigmoid(g) * up).astype(o_ref.dtype)


def _ffn_gate(h, wg_stack, wu_stack, layer, conv_w, conv_b, seq):
    t, d = h.shape
    kf = wg_stack.shape[2]
    tm = min(1024, seq)
    tf = 512
    wspec = pl.BlockSpec((1, d, tf), lambda i, j: (layer, 0, j))
    vmem = 2 * (tm * d * 2 + 2 * d * tf * 2 + tm * tf * 2) + 8 * tm * tf * 4
    return pl.pallas_call(
        functools.partial(_ffn_gate_kernel, tiles_per_batch=seq // tm),
        grid=(t // tm, kf // tf),
        in_specs=[pl.BlockSpec((tm, d), lambda i, j: (i, 0)), wspec, wspec,
                  pl.BlockSpec((FFN_CONV_WIDTH, tf), lambda i, j: (0, j)),
                  pl.BlockSpec((1, tf), lambda i, j: (0, j))],
        out_specs=pl.BlockSpec((tm, tf), lambda i, j: (i, j)),
        out_shape=jax.ShapeDtypeStruct((t, kf), BF16),
        scratch_shapes=[pltpu.VMEM((kf // tf, SUBLANES, tf), F32)],
        compiler_params=_cparams(("arbitrary", "arbitrary"), vmem),
        name="ffn_gate_up",
    )(h, wg_stack, wu_stack, conv_w, conv_b)


def kernel(x, c, w_ada, b_ada, g_mix, w_in, b_f, pool_w, pool_scale, lru_conv_w, lru_conv_b, lru_wa, lru_ba,
           lru_wi, lru_bi, lru_lambda, w_out, g_ffn, w_ffn_gate, w_ffn_up, ffn_conv_w, ffn_conv_b, w_ffn_down,
           final_g):
    batch, seq, d = x.shape
    depth = w_ada.shape[0]
    pool_width = pool_w.shape[1] * pool_w.shape[2]
    lru_width = lru_lambda.shape[1]
    n_heads = b_f.shape[1]
    attn_width = n_heads * HEAD_DIM
    assert w_in.shape[2] == pool_width + 3 * attn_width + n_heads + 2 * lru_width
    assert pool_width == lru_width == N_POOL_GROUPS * LANES and seq % SUBLANES == 0

    o_q = pool_width
    o_f = o_q + 3 * attn_width
    o_x = o_f + n_heads
    w_qkv = w_in[:, :, o_q:o_f].astype(BF16)
    w_f = jnp.pad(w_in[:, :, o_f:o_x], ((0, 0), (0, 0), (0, LANES - n_heads)))
    w_pxyf = jnp.concatenate([w_in[:, :, :o_q], w_in[:, :, o_x:], w_f], axis=2).astype(BF16)
    f_col = (pool_width + 2 * lru_width) // LANES
    bf_pad = jnp.pad(b_f, ((0, 0), (0, LANES - n_heads)))
    qkv_scale = jnp.concatenate([jnp.full((1, attn_width), HEAD_DIM ** -0.5 * LOG2E, F32),
                                 jnp.ones((1, 2 * attn_width), F32)], axis=1)
    pool_w16 = pool_w.astype(BF16)
    lru_wa16 = lru_wa.astype(BF16)
    lru_wi16 = lru_wi.astype(BF16)
    w_out16 = w_out.astype(BF16)
    w_gate16 = w_ffn_gate.astype(BF16)
    w_up16 = w_ffn_up.astype(BF16)
    w_down16 = w_ffn_down.astype(BF16)

    mod = _ada_mod(c, w_ada, b_ada)[:, :batch]

    def mod_chunk(layer, idx):
        return mod[layer, :, idx * d:(idx + 1) * d].reshape(batch, 1, d)

    x2 = x.reshape(batch * seq, d)
    h = _norm(x2, g_mix[0][None], mod_chunk(0, 1), mod_chunk(0, 0), seq)
    out = None
    for layer in range(depth):
        sh1, sc1, gt1, sh2, sc2, gt2 = (mod_chunk(layer, idx) for idx in range(6))
        zqkv = _matmul(h, w_qkv, layer, BF16, 3 * attn_width // 2, qkv_scale).reshape(batch, seq, 3 * attn_width)
        z32 = _matmul(h, w_pxyf, layer, F32, w_pxyf.shape[2]).reshape(batch, seq, w_pxyf.shape[2])

        fbias = _forget_bias(z32, f_col, bf_pad[layer][None], n_heads)
        y_attn = _attention(zqkv, fbias, n_heads).reshape(batch * seq, attn_width)
        y_pool = _pool_mixer(z32, pool_w16, layer, pool_scale[layer][None], pool_width)
        lru_p = dict(conv_w=lru_conv_w[layer], conv_b=lru_conv_b[layer][None], wa=lru_wa16,
                     ba=lru_ba[layer][None], wi=lru_wi16, bi=lru_bi[layer][None], lam=lru_lambda[layer][None])
        y_lru = _lru_mixer(z32, layer, lru_p, lru_width, 1, 2)

        x2, h = _mix_out(y_pool.reshape(batch * seq, pool_width), y_attn, y_lru.reshape(batch * seq, lru_width),
                         w_out16, layer, x2, gt1, g_ffn[layer][None], sc2, sh2, seq)
        act = _ffn_gate(h, w_gate16, w_up16, layer, ffn_conv_w[layer], ffn_conv_b[layer][None], seq)
        if layer + 1 < depth:
            x2, h = _ffn_down(act, w_down16, layer, x2, gt2, g_mix[layer + 1][None], mod_chunk(layer + 1, 1),
                              mod_chunk(layer + 1, 0), seq, True, BF16)
        else:
            zeros = jnp.zeros((batch, 1, d), F32)
            (out,) = _ffn_down(act, w_down16, layer, x2, gt2, final_g[None], zeros, zeros, seq, False, F32)
    return out.reshape(batch, seq, d)
```

```python
import functools

import jax
import jax.numpy as jnp
from jax import lax
from jax.experimental import pallas as pl
from jax.experimental.pallas import tpu as pltpu

F32 = jnp.float32
BF16 = jnp.bfloat16

HEAD_DIM = 128
N_POOL_GROUPS = 4
POOL_WINDOWS = (2, 4, 8, 16)
POOL_HALO = 16
N_LRU_BLOCKS = 4
LRU_CONV_WIDTH = 4
LRU_C = 8.0
FFN_CONV_WIDTH = 3
EPS = 1e-6

LANES = 128
SUBLANES = 8
VMEM_LIMIT_CAP_V7X = 58 * 2**20
MASK_VALUE = -1e30
LOG2E = 1.4426950408889634


def _cparams(semantics, vmem_bytes):
    return pltpu.CompilerParams(dimension_semantics=semantics,
                                vmem_limit_bytes=int(min(VMEM_LIMIT_CAP_V7X, max(vmem_bytes, 16 * 2**20))))


def _sigmoid(x):
    return 0.5 * jnp.tanh(0.5 * x) + 0.5


def _norm_mod(x, g, sc, sh):
    ms = jnp.mean(x * x, axis=-1, keepdims=True)
    return (x * lax.rsqrt(ms + EPS) * g) * (1.0 + sc) + sh


def _split3_bf16(x):
    hi = x.astype(BF16)
    r1 = x - hi.astype(F32)
    mid = r1.astype(BF16)
    lo = (r1 - mid.astype(F32)).astype(BF16)
    return hi, mid, lo


def _ada_kernel(c_ref, w_ref, b_ref, o_ref, acc_ref):
    nb = c_ref.shape[0]
    n = w_ref.shape[3]
    k = pl.program_id(1)

    @pl.when(k == 0)
    def _():
        acc_ref[...] = jnp.zeros_like(acc_ref)

    ca = [c_ref[b] * _sigmoid(c_ref[b]) for b in range(nb)]

    def lane_chunk(jc, carry):
        sl = pl.ds(pl.multiple_of(jc * LANES, LANES), LANES)
        w = w_ref[0, :, :, sl]
        for b in range(nb):
            acc_ref[b, :, sl] += jnp.sum(w * ca[b], axis=0)
        return carry

    lax.fori_loop(0, n // LANES, lane_chunk, 0, unroll=4)

    @pl.when(k == pl.num_programs(1) - 1)
    def _():
        o_ref[...] = jnp.zeros_like(o_ref)
        for b in range(nb):
            o_ref[0, b:b + 1, :] = jnp.sum(acc_ref[b], axis=0, keepdims=True) + b_ref[0]


ADA_SLAB_ROWS = 256


def _ada_mod(c, w_ada, b_ada):
    depth, d, n = w_ada.shape
    batch = c.shape[0]
    dr = d // SUBLANES
    sr = ADA_SLAB_ROWS // SUBLANES
    c_lanes = jnp.broadcast_to(c[:, :, None], (batch, d, LANES)).reshape(batch, dr, SUBLANES, LANES)
    return pl.pallas_call(
        _ada_kernel,
        grid=(depth, dr // sr),
        in_specs=[pl.BlockSpec((batch, sr, SUBLANES, LANES), lambda l, k: (0, k, 0, 0)),
                  pl.BlockSpec((1, sr, SUBLANES, n), lambda l, k: (l, k, 0, 0)),
                  pl.BlockSpec((1, 1, n), lambda l, k: (l, 0, 0))],
        out_specs=pl.BlockSpec((1, SUBLANES, n), lambda l, k: (l, 0, 0)),
        out_shape=jax.ShapeDtypeStruct((depth, SUBLANES, n), F32),
        scratch_shapes=[pltpu.VMEM((batch, SUBLANES, n), F32)],
        compiler_params=_cparams(("parallel", "arbitrary"), 3 * ADA_SLAB_ROWS * n * 4),
        name="ada_mod",
    )(c_lanes, w_ada.reshape(depth, dr, SUBLANES, n), b_ada.reshape(depth, 1, n))


def _norm_kernel(x_ref, g_ref, sc_ref, sh_ref, h_ref):
    h_ref[...] = _norm_mod(x_ref[...], g_ref[...], sc_ref[0], sh_ref[0]).astype(h_ref.dtype)


def _norm(x2, g, sc, sh, seq):
    t, d = x2.shape
    tm = min(512, seq)
    tpb = seq // tm
    return pl.pallas_call(
        _norm_kernel,
        grid=(t // tm,),
        in_specs=[pl.BlockSpec((tm, d), lambda i: (i, 0)),
                  pl.BlockSpec((1, d), lambda i: (0, 0)),
                  pl.BlockSpec((1, 1, d), lambda i: (i // tpb, 0, 0)),
                  pl.BlockSpec((1, 1, d), lambda i: (i // tpb, 0, 0))],
        out_specs=pl.BlockSpec((tm, d), lambda i: (i, 0)),
        out_shape=jax.ShapeDtypeStruct((t, d), BF16),
        compiler_params=_cparams(("parallel",), 6 * tm * d * 4),
        name="norm_mod",
    )(x2, g, sc, sh)


def _mm_kernel(a_ref, w_ref, o_ref):
    o_ref[...] = jnp.dot(a_ref[...], w_ref[0], preferred_element_type=F32).astype(o_ref.dtype)


def _mm_heads_kernel(a_ref, w_ref, cs_ref, o_ref):
    acc = jnp.dot(a_ref[...], w_ref[0], preferred_element_type=F32) * cs_ref[...]
    for hh in range(o_ref.shape[0]):
        o_ref[hh] = acc[:, hh * HEAD_DIM:(hh + 1) * HEAD_DIM].astype(o_ref.dtype)


def _matmul(a, w_stack, layer, out_dtype, tn):
    t, k = a.shape
    n = w_stack.shape[2]
    tm = min(1024, t)
    vmem = 2 * (tm * k * 2 + k * tn * 2 + tm * tn * 4) + 2 * tm * tn * 4
    return pl.pallas_call(
        _mm_kernel,
        grid=(t // tm, n // tn),
        in_specs=[pl.BlockSpec((tm, k), lambda i, j: (i, 0)),
                  pl.BlockSpec((1, k, tn), lambda i, j: (layer, 0, j))],
        out_specs=pl.BlockSpec((tm, tn), lambda i, j: (i, j)),
        out_shape=jax.ShapeDtypeStruct((t, n), out_dtype),
        compiler_params=_cparams(("parallel", "parallel"), vmem),
        name="matmul_dmodel",
    )(a, w_stack)


def _matmul_heads(a, w_stack, layer, tn, col_scale):
    t, k = a.shape
    n = w_stack.shape[2]
    tm = min(1024, t)
    hpb = tn // HEAD_DIM
    vmem = 2 * (tm * k * 2 + k * tn * 2 + tm * tn * 2) + 2 * tm * tn * 4
    return pl.pallas_call(
        _mm_heads_kernel,
        grid=(t // tm, n // tn),
        in_specs=[pl.BlockSpec((tm, k), lambda i, j: (i, 0)),
                  pl.BlockSpec((1, k, tn), lambda i, j: (layer, 0, j)),
                  pl.BlockSpec((1, tn), lambda i, j: (0, j))],
        out_specs=pl.BlockSpec((hpb, tm, HEAD_DIM), lambda i, j: (j, i, 0)),
        out_shape=jax.ShapeDtypeStruct((n // HEAD_DIM, t, HEAD_DIM), BF16),
        compiler_params=_cparams(("parallel", "parallel"), vmem),
        name="matmul_heads",
    )(a, w_stack, col_scale)


N_SPLIT = 3


def _fcum_kernel(z_ref, b_ref, o_ref, carry_ref, tri_ref, place_ref):
    ts = z_ref.shape[1]
    w = z_ref.shape[2]
    wo = place_ref.shape[1]

    @pl.when((pl.program_id(0) == 0) & (pl.program_id(1) == 0))
    def _():
        r = lax.broadcasted_iota(jnp.int32, (ts, ts), 0)
        c = lax.broadcasted_iota(jnp.int32, (ts, ts), 1)
        tri_ref[...] = jnp.where(c <= r, 1.0, 0.0).astype(BF16)
        row = lax.broadcasted_iota(jnp.int32, (N_SPLIT * w, wo), 0)
        lane = lax.broadcasted_iota(jnp.int32, (N_SPLIT * w, wo), 1)
        target = jnp.zeros_like(row)
        for idx in range(N_SPLIT):
            in_part = (row >= idx * w) & (row < (idx + 1) * w)
            target = jnp.where(in_part, (row - idx * w) * HEAD_DIM + idx, target)
        place_ref[...] = jnp.where(lane == target, 1.0, 0.0).astype(BF16)

    @pl.when(pl.program_id(1) == 0)
    def _():
        carry_ref[...] = jnp.zeros_like(carry_ref)

    x = z_ref[0] + b_ref[...]
    lf = jnp.minimum(x, 0.0) - jnp.log(1.0 + jnp.exp(-jnp.abs(x)))
    parts = jnp.dot(tri_ref[...], jnp.concatenate(_split3_bf16(lf), axis=1), preferred_element_type=F32)
    cs = carry_ref[...] + parts[:, :w]
    for idx in range(1, N_SPLIT):
        cs = cs + parts[:, idx * w:(idx + 1) * w]
    carry_ref[...] = cs[ts - 1:ts, :]

    split = jnp.concatenate(_split3_bf16(cs * LOG2E), axis=1)
    bias = jnp.dot(split, place_ref[...], preferred_element_type=F32)
    for hh in range(o_ref.shape[0]):
        o_ref[hh, 0] = bias[:, hh * HEAD_DIM:(hh + 1) * HEAD_DIM].astype(o_ref.dtype)


def _forget_bias(z32, f_col, bf_pad, n_heads):
    b, s, _ = z32.shape
    w = LANES
    ts = min(512, s)
    wo = n_heads * HEAD_DIM
    return pl.pallas_call(
        _fcum_kernel,
        grid=(b, s // ts),
        in_specs=[pl.BlockSpec((1, ts, w), lambda bi, i: (bi, i, f_col)),
                  pl.BlockSpec((1, w), lambda bi, i: (0, 0))],
        out_specs=pl.BlockSpec((n_heads, 1, ts, HEAD_DIM), lambda bi, i: (0, bi, i, 0)),
        out_shape=jax.ShapeDtypeStruct((n_heads, b, s, HEAD_DIM), BF16),
        scratch_shapes=[pltpu.VMEM((1, w), F32), pltpu.VMEM((ts, ts), BF16), pltpu.VMEM((N_SPLIT * w, wo), BF16)],
        compiler_params=_cparams(("arbitrary", "arbitrary"), 4 * ts * ts * 4 + 8 * ts * wo * 4),
        name="forget_bias",
    )(z32, bf_pad)


ONES_ROWS = 16


MAX_CHAINS = 8
BIG_CHUNK = 4


def _attn_kernel(q_ref, k_ref, v_ref, fb_ref, o_ref, vt_ref, acc_ref, m_ref, *, tq):
    tk = tq
    s_len = k_ref.shape[1]
    step = pl.program_id(2)

    @pl.when(step == 0)
    def _():
        vt_ref[HEAD_DIM:, :] = jnp.ones((ONES_ROWS, s_len), BF16)
        for c in range(s_len // tk):
            vt_ref[:HEAD_DIM, c * tk:(c + 1) * tk] = v_ref[0, c * tk:(c + 1) * tk, :].astype(F32).T.astype(BF16)

    for sub in range(q_ref.shape[1] // tq):
        rows = slice(sub * tq, (sub + 1) * tq)
        o_ref[0, rows, :] = _attn_query_tile(step * (q_ref.shape[1] // tq) + sub, q_ref[0, rows, :], k_ref, fb_ref,
                                             vt_ref, acc_ref, m_ref).astype(o_ref.dtype)


def _attn_query_tile(qi, q, k_ref, fb_ref, vt_ref, acc_ref, m_ref):
    tq = q.shape[0]
    tk = tq
    lane = lax.broadcasted_iota(jnp.int32, (tq, HEAD_DIM), 1)
    minus_one = jnp.where(lane < 3, -1.0, 0.0).astype(BF16)
    q_aug = jnp.concatenate([q, minus_one], axis=1)
    m_ref[...] = jnp.full(m_ref.shape, MASK_VALUE, F32)
    acc_ref[...] = jnp.zeros_like(acc_ref)

    def logits(k0, size):
        k0 = pl.multiple_of(k0, tk)
        k_aug = jnp.concatenate([k_ref[0, pl.ds(k0, size), :], fb_ref[0, 0, pl.ds(k0, size), :]], axis=1)
        return lax.dot_general(k_aug, q_aug, (((1,), (1,)), ((), ())), preferred_element_type=F32)

    def absorb(st, k0, size, masked):
        k0 = pl.multiple_of(k0, tk)
        if masked:
            krow = lax.broadcasted_iota(jnp.int32, (size, tq), 0)
            qcol = lax.broadcasted_iota(jnp.int32, (size, tq), 1)
            st = jnp.where(krow <= qcol, st, MASK_VALUE)
        part = jnp.max(st.reshape(MAX_CHAINS, size // MAX_CHAINS, tq), axis=1)
        m_old = m_ref[...]
        m_new = jnp.maximum(m_old, jnp.max(part, axis=0, keepdims=True))
        m_ref[...] = m_new
        pt = jnp.exp2((st - m_new).astype(BF16))
        acc_ref[...] = jnp.exp2(m_old - m_new) * acc_ref[...] + jnp.dot(
            vt_ref[:, pl.ds(k0, size)], pt, preferred_element_type=F32)

    def pair(k0, size_a, size_b, mask_b):
        st_a = logits(k0, size_a)
        st_b = logits(k0 + size_a, size_b)
        absorb(st_a, k0, size_a, False)
        absorb(st_b, k0 + size_a, size_b, mask_b)

    def body(kc, carry):
        half = BIG_CHUNK * tk // 2
        pair(kc * (BIG_CHUNK * tk), half, half, False)
        return carry

    n_big = qi // BIG_CHUNK
    lax.fori_loop(0, n_big, body, 0)
    tail_k0 = n_big * (BIG_CHUNK * tk)
    for visible in range(BIG_CHUNK):
        @pl.when(qi - n_big * BIG_CHUNK == visible)
        def _(visible=visible):
            if visible == 0:
                absorb(logits(tail_k0, tk), tail_k0, tk, True)
            else:
                pair(tail_k0, visible * tk, tk, True)

    o_t = acc_ref[:HEAD_DIM, :] * (1.0 / acc_ref[HEAD_DIM:HEAD_DIM + 1, :])
    return o_t.T


def _attention(zqkv, fbias, batch, seq):
    n_heads = fbias.shape[0]
    b, s = batch, seq
    tq = min(512, s)
    tiles_per_step = 2 if s % (2 * tq) == 0 else 1
    tqs = tq * tiles_per_step
    spb = s // tqs
    assert s % tqs == 0
    vmem = 2 * (3 * s * HEAD_DIM * 2) + (HEAD_DIM + ONES_ROWS) * s * 2 + 6 * BIG_CHUNK * tq * tq * 4
    kv_spec = lambda off: pl.BlockSpec((1, s, HEAD_DIM), lambda bi, h, i: (off + h, bi, 0))
    return pl.pallas_call(
        functools.partial(_attn_kernel, tq=tq),
        grid=(b, n_heads, spb),
        in_specs=[pl.BlockSpec((1, tqs, HEAD_DIM), lambda bi, h, i: (h, bi * spb + i, 0)),
                  kv_spec(n_heads), kv_spec(2 * n_heads),
                  pl.BlockSpec((1, 1, s, HEAD_DIM), lambda bi, h, i: (h, bi, 0, 0))],
        out_specs=pl.BlockSpec((1, tqs, HEAD_DIM), lambda bi, h, i: (bi, i, h)),
        out_shape=jax.ShapeDtypeStruct((b, s, n_heads * HEAD_DIM), BF16),
        scratch_shapes=[pltpu.VMEM((HEAD_DIM + ONES_ROWS, s), BF16),
                        pltpu.VMEM((HEAD_DIM + ONES_ROWS, tq), F32),
                        pltpu.VMEM((1, tq), F32)],
        compiler_params=_cparams(("parallel", "parallel", "arbitrary"), vmem),
        name="fox_attention",
    )(zqkv, zqkv, zqkv, fbias)


def _pool_kernel(u_ref, halo_ref, w_ref, sc_ref, o_ref):
    tm = u_ref.shape[1]
    i = pl.program_id(1)
    halo = jnp.where(i > 0, halo_ref[0], 0.0)
    ext = jnp.concatenate([halo, u_ref[0]], axis=0)
    pos = (i * tm + 1 + lax.broadcasted_iota(jnp.int32, (tm, 1), 0)).astype(F32)
    gd = LANES
    for g, win in enumerate(POOL_WINDOWS):
        e = ext[:, g * gd:(g + 1) * gd]
        ssum = e
        shift = 1
        while shift < win:
            ssum = ssum + pltpu.roll(ssum, shift, 0)
            shift *= 2
        mean = ssum[POOL_HALO:] * (1.0 / jnp.minimum(pos, float(win)))
        dlt = mean - e[POOL_HALO:]
        y = jnp.dot(dlt.astype(BF16), w_ref[0, g], preferred_element_type=F32)
        o_ref[0, :, g * gd:(g + 1) * gd] = (y * sc_ref[:, g * gd:(g + 1) * gd]).astype(o_ref.dtype)


def _pool_mixer(z32, pool_w_stack, layer, scale_row, width):
    b, s, _ = z32.shape
    tm = min(512, s)
    hb = tm // POOL_HALO
    return pl.pallas_call(
        _pool_kernel,
        grid=(b, s // tm),
        in_specs=[pl.BlockSpec((1, tm, width), lambda bi, i: (bi, i, 0)),
                  pl.BlockSpec((1, POOL_HALO, width), lambda bi, i: (bi, jnp.maximum(i * hb - 1, 0), 0)),
                  pl.BlockSpec((1, N_POOL_GROUPS, LANES, LANES), lambda bi, i: (layer, 0, 0, 0)),
                  pl.BlockSpec((1, width), lambda bi, i: (0, 0))],
        out_specs=pl.BlockSpec((1, tm, width), lambda bi, i: (bi, i, 0)),
        out_shape=jax.ShapeDtypeStruct((b, s, width), BF16),
        compiler_params=_cparams(("parallel", "parallel"), 16 * tm * width * 4),
        name="pool_mixer",
    )(z32, z32, pool_w_stack, scale_row)


def _lru_kernel(x_ref, halo_ref, y_ref, cw_ref, cb_ref, wa_ref, ba_ref, wi_ref, bi_ref, lam_ref,
                o_ref, h_ref, a_s, b_s, h_s):
    tm = x_ref.shape[1]
    i = pl.program_id(1)

    @pl.when(i == 0)
    def _():
        h_ref[...] = jnp.zeros_like(h_ref)

    halo = jnp.where(i > 0, halo_ref[0], 0.0)
    ext = jnp.concatenate([halo, x_ref[0]], axis=0)
    xc = cb_ref[...] + cw_ref[LRU_CONV_WIDTH - 1:LRU_CONV_WIDTH, :] * ext[SUBLANES:]
    for k in range(LRU_CONV_WIDTH - 1):
        shifted = pltpu.roll(ext, LRU_CONV_WIDTH - 1 - k, 0)[SUBLANES:]
        xc = xc + cw_ref[k:k + 1, :] * shifted

    lam = lam_ref[...]
    neg_softplus = -(jnp.maximum(-lam, 0.0) + jnp.log(1.0 + jnp.exp(-jnp.abs(lam))))
    gd = LANES
    n_groups = tm // SUBLANES
    sub_row = lax.broadcasted_iota(jnp.int32, (n_groups, SUBLANES, gd), 1)
    for blk in range(N_LRU_BLOCKS):
        sl = slice(blk * gd, (blk + 1) * gd)
        xb = xc[:, sl]
        xb16 = xb.astype(BF16)
        gate_r = _sigmoid(jnp.dot(xb16, wa_ref[0, blk], preferred_element_type=F32) + ba_ref[:, sl])
        gate_i = _sigmoid(jnp.dot(xb16, wi_ref[0, blk], preferred_element_type=F32) + bi_ref[:, sl])
        log_a = LRU_C * gate_r * neg_softplus[:, sl]
        a = jnp.exp(log_a)
        b = jnp.sqrt(1.0 - a * a) * (gate_i * xb)
        a = a.reshape(n_groups, SUBLANES, gd)
        b = b.reshape(n_groups, SUBLANES, gd)
        dist = 1
        while dist < SUBLANES:
            has_prev = sub_row >= dist
            b = b + a * jnp.where(has_prev, pltpu.roll(b, dist, 1), 0.0)
            a = a * jnp.where(has_prev, pltpu.roll(a, dist, 1), 1.0)
            dist *= 2
        a_s[blk] = a
        b_s[blk] = b

    def group(gi, hs):
        out = []
        for blk in range(N_LRU_BLOCKS):
            rows = b_s[blk, gi] + a_s[blk, gi] * hs[blk]
            h_s[blk, gi] = rows
            out.append(rows[SUBLANES - 1:SUBLANES, :])
        return tuple(out)

    hs = lax.fori_loop(0, n_groups, group,
                       tuple(h_ref[:, blk * gd:(blk + 1) * gd] for blk in range(N_LRU_BLOCKS)), unroll=8)

    for blk in range(N_LRU_BLOCKS):
        sl = slice(blk * gd, (blk + 1) * gd)
        h_ref[:, sl] = hs[blk]
        y = y_ref[0, :, sl]
        gelu = 0.5 * y * (1.0 + jnp.tanh(0.7978845608028654 * (y + 0.044715 * (y * y * y))))
        o_ref[0, :, sl] = (h_s[blk].reshape(tm, gd) * gelu).astype(o_ref.dtype)


def _lru_mixer(z32, layer, p, width, x_col, y_col):
    b, s, _ = z32.shape
    tm = min(512, s)
    hb = tm // SUBLANES
    row = lambda bi, i: (0, 0)
    return pl.pallas_call(
        _lru_kernel,
        grid=(b, s // tm),
        in_specs=[pl.BlockSpec((1, tm, width), lambda bi, i: (bi, i, x_col)),
                  pl.BlockSpec((1, SUBLANES, width), lambda bi, i: (bi, jnp.maximum(i * hb - 1, 0), x_col)),
                  pl.BlockSpec((1, tm, width), lambda bi, i: (bi, i, y_col)),
                  pl.BlockSpec((LRU_CONV_WIDTH, width), row),
                  pl.BlockSpec((1, width), row),
                  pl.BlockSpec((1, N_LRU_BLOCKS, LANES, LANES), lambda bi, i: (layer, 0, 0, 0)),
                  pl.BlockSpec((1, width), row),
                  pl.BlockSpec((1, N_LRU_BLOCKS, LANES, LANES), lambda bi, i: (layer, 0, 0, 0)),
                  pl.BlockSpec((1, width), row),
                  pl.BlockSpec((1, width), row)],
        out_specs=pl.BlockSpec((1, tm, width), lambda bi, i: (bi, i, 0)),
        out_shape=jax.ShapeDtypeStruct((b, s, width), BF16),
        scratch_shapes=[pltpu.VMEM((1, width), F32)]
        + [pltpu.VMEM((N_LRU_BLOCKS, tm // SUBLANES, SUBLANES, LANES), F32)] * 3,
        compiler_params=_cparams(("arbitrary", "arbitrary"), 24 * tm * width * 4),
        name="rg_lru",
    )(z32, z32, z32, p["conv_w"], p["conv_b"], p["wa"], p["ba"], p["wi"], p["bi"], p["lam"])


COL_BLOCK = 512


def _col_blocks(d):
    return [slice(c, c + COL_BLOCK) for c in range(0, d, COL_BLOCK)]


def _residual_cols(acc, cols, x_ref, gt_ref, xn_refs):
    xn = x_ref[:, cols] + gt_ref[0, :, cols] * acc
    for ref in xn_refs:
        ref[:, cols] = xn
    return jnp.sum(xn * xn, axis=-1, keepdims=True)


def _norm_rows(sumsq, xn_ref, g_ref, sc_ref, sh_ref, h_ref):
    rs = lax.rsqrt(sumsq * (1.0 / xn_ref.shape[1]) + EPS)
    gain = g_ref[...] * (1.0 + sc_ref[0])
    h_ref[...] = (xn_ref[...] * rs * gain + sh_ref[0]).astype(h_ref.dtype)


def _mix_out_kernel(yp_ref, ya_ref, yl_ref, w_ref, x_ref, gt_ref, g_ref, sc_ref, sh_ref, xo_ref, h_ref):
    kp = yp_ref.shape[1]
    ka = ya_ref.shape[1]
    sumsq = jnp.zeros((x_ref.shape[0], 1), F32)
    for cols in _col_blocks(x_ref.shape[1]):
        acc = jnp.dot(yp_ref[...], w_ref[0, 0:kp, cols], preferred_element_type=F32)
        acc = acc + jnp.dot(ya_ref[...], w_ref[0, kp:kp + ka, cols], preferred_element_type=F32)
        acc = acc + jnp.dot(yl_ref[...], w_ref[0, kp + ka:, cols], preferred_element_type=F32)
        sumsq = sumsq + _residual_cols(acc, cols, x_ref, gt_ref, (xo_ref,))
    _norm_rows(sumsq, xo_ref, g_ref, sc_ref, sh_ref, h_ref)


def _mix_out(yp, ya, yl, w_stack, layer, x2, gt, g, sc, sh, seq):
    t, d = x2.shape
    tm = min(512, seq)
    tpb = seq // tm
    per_b = lambda i: (i // tpb, 0, 0)
    lhs = lambda y: pl.BlockSpec((tm, y.shape[1]), lambda i: (i, 0))
    vmem = 2 * d * d * 2 + 2 * (tm * d * 2 + 3 * tm * d * 4) + 3 * tm * d * 4
    return pl.pallas_call(
        _mix_out_kernel,
        grid=(t // tm,),
        in_specs=[lhs(yp), lhs(ya), lhs(yl),
                  pl.BlockSpec((1, d, d), lambda i: (layer, 0, 0)),
                  pl.BlockSpec((tm, d), lambda i: (i, 0)),
                  pl.BlockSpec((1, 1, d), per_b),
                  pl.BlockSpec((1, d), lambda i: (0, 0)),
                  pl.BlockSpec((1, 1, d), per_b),
                  pl.BlockSpec((1, 1, d), per_b)],
        out_specs=[pl.BlockSpec((tm, d), lambda i: (i, 0)), pl.BlockSpec((tm, d), lambda i: (i, 0))],
        out_shape=[jax.ShapeDtypeStruct((t, d), F32), jax.ShapeDtypeStruct((t, d), BF16)],
        compiler_params=_cparams(("parallel",), vmem),
        name="mix_out_proj",
    )(yp, ya, yl, w_stack, x2, gt, g, sc, sh)


def _ffn_down_kernel(a_ref, w_ref, x_ref, gt_ref, g_ref, sc_ref, sh_ref, *out_refs):
    xn_ref, h_ref = out_refs[0], out_refs[-1]
    sumsq = jnp.zeros((x_ref.shape[0], 1), F32)
    for cols in _col_blocks(x_ref.shape[1]):
        acc = jnp.dot(a_ref[...], w_ref[0, :, cols], preferred_element_type=F32)
        sumsq = sumsq + _residual_cols(acc, cols, x_ref, gt_ref, (xn_ref,))
    _norm_rows(sumsq, xn_ref, g_ref, sc_ref, sh_ref, h_ref)


def _ffn_down(act, w_stack, layer, x2, gt, g, sc, sh, seq, emit_x, h_dtype):
    t, d = x2.shape
    kf = act.shape[1]
    tm = min(512, seq)
    tpb = seq // tm
    per_b = lambda i: (i // tpb, 0, 0)
    row_tile = pl.BlockSpec((tm, d), lambda i: (i, 0))
    out_specs = [row_tile]
    out_shape = [jax.ShapeDtypeStruct((t, d), h_dtype)]
    if emit_x:
        out_specs = [row_tile, row_tile]
        out_shape = [jax.ShapeDtypeStruct((t, d), F32)] + out_shape
    else:
        assert h_dtype == F32
    vmem = kf * d * 2 + 2 * (tm * kf * 2 + 3 * tm * d * 4) + 2 * tm * COL_BLOCK * 4
    return pl.pallas_call(
        _ffn_down_kernel,
        grid=(t // tm,),
        in_specs=[pl.BlockSpec((tm, kf), lambda i: (i, 0)),
                  pl.BlockSpec((1, kf, d), lambda i: (layer, 0, 0), pipeline_mode=pl.Buffered(1)),
                  row_tile,
                  pl.BlockSpec((1, 1, d), per_b),
                  pl.BlockSpec((1, d), lambda i: (0, 0)),
                  pl.BlockSpec((1, 1, d), per_b),
                  pl.BlockSpec((1, 1, d), per_b)],
        out_specs=out_specs,
        out_shape=out_shape,
        compiler_params=_cparams(("parallel",), vmem),
        name="ffn_down_proj",
    )(act, w_stack, x2, gt, g, sc, sh)


def _ffn_gate_kernel(h_ref, wg_ref, wu_ref, cw_ref, cb_ref, o_ref, carry_ref, *, tiles_per_batch):
    tm = h_ref.shape[0]
    i = pl.program_id(0)
    j = pl.program_id(1)

    @pl.when(i % tiles_per_batch == 0)
    def _():
        carry_ref[j] = jnp.zeros(carry_ref.shape[1:], F32)

    h = h_ref[...]
    u = jnp.dot(h, wg_ref[0], preferred_element_type=F32)
    ext = jnp.concatenate([carry_ref[j], u], axis=0)
    carry_ref[j] = u[tm - SUBLANES:, :]
    g = cb_ref[...] + cw_ref[FFN_CONV_WIDTH - 1:FFN_CONV_WIDTH, :] * u
    for k in range(FFN_CONV_WIDTH - 1):
        shifted = pltpu.roll(ext, FFN_CONV_WIDTH - 1 - k, 0)[SUBLANES:]
        g = g + cw_ref[k:k + 1, :] * shifted
    up = jnp.dot(h, wu_ref[0], preferred_element_type=F32)
    o_ref[...] = (g * _sigmoid(g) * up).astype(o_ref.dtype)


def _ffn_gate(h, wg_stack, wu_stack, layer, conv_w, conv_b, seq):
    t, d = h.shape
    kf = wg_stack.shape[2]
    tm = min(1024, seq)
    tf = 512
    wspec = pl.BlockSpec((1, d, tf), lambda i, j: (layer, 0, j))
    vmem = 2 * (tm * d * 2 + 2 * d * tf * 2 + tm * tf * 2) + 8 * tm * tf * 4
    return pl.pallas_call(
        functools.partial(_ffn_gate_kernel, tiles_per_batch=seq // tm),
        grid=(t // tm, kf // tf),
        in_specs=[pl.BlockSpec((tm, d), lambda i, j: (i, 0)), wspec, wspec,
                  pl.BlockSpec((FFN_CONV_WIDTH, tf), lambda i, j: (0, j)),
                  pl.BlockSpec((1, tf), lambda i, j: (0, j))],
        out_specs=pl.BlockSpec((tm, tf), lambda i, j: (i, j)),
        out_shape=jax.ShapeDtypeStruct((t, kf), BF16),
        scratch_shapes=[pltpu.VMEM((kf // tf, SUBLANES, tf), F32)],
        compiler_params=_cparams(("arbitrary", "arbitrary"), vmem),
        name="ffn_gate_up",
    )(h, wg_stack, wu_stack, conv_w, conv_b)


def kernel(x, c, w_ada, b_ada, g_mix, w_in, b_f, pool_w, pool_scale, lru_conv_w, lru_conv_b, lru_wa, lru_ba,
           lru_wi, lru_bi, lru_lambda, w_out, g_ffn, w_ffn_gate, w_ffn_up, ffn_conv_w, ffn_conv_b, w_ffn_down,
           final_g):
    batch, seq, d = x.shape
    depth = w_ada.shape[0]
    pool_width = pool_w.shape[1] * pool_w.shape[2]
    lru_width = lru_lambda.shape[1]
    n_heads = b_f.shape[1]
    attn_width = n_heads * HEAD_DIM
    assert w_in.shape[2] == pool_width + 3 * attn_width + n_heads + 2 * lru_width
    assert pool_width == lru_width == N_POOL_GROUPS * LANES and seq % SUBLANES == 0

    o_q = pool_width
    o_f = o_q + 3 * attn_width
    o_x = o_f + n_heads
    w_qkv = w_in[:, :, o_q:o_f].astype(BF16)
    w_f = jnp.pad(w_in[:, :, o_f:o_x], ((0, 0), (0, 0), (0, LANES - n_heads)))
    w_pxyf = jnp.concatenate([w_in[:, :, :o_q], w_in[:, :, o_x:], w_f], axis=2).astype(BF16)
    f_col = (pool_width + 2 * lru_width) // LANES
    bf_pad = jnp.pad(b_f, ((0, 0), (0, LANES - n_heads)))
    qkv_scale = jnp.concatenate([jnp.full((1, attn_width), HEAD_DIM ** -0.5 * LOG2E, F32),
                                 jnp.ones((1, 2 * attn_width), F32)], axis=1)
    pool_w16 = pool_w.astype(BF16)
    lru_wa16 = lru_wa.astype(BF16)
    lru_wi16 = lru_wi.astype(BF16)
    w_out16 = w_out.astype(BF16)
    w_gate16 = w_ffn_gate.astype(BF16)
    w_up16 = w_ffn_up.astype(BF16)
    w_down16 = w_ffn_down.astype(BF16)

    mod = _ada_mod(c, w_ada, b_ada)[:, :batch]

    def mod_chunk(layer, idx):
        return mod[layer, :, idx * d:(idx + 1) * d].reshape(batch, 1, d)

    x2 = x.reshape(batch * seq, d)
    h = _norm(x2, g_mix[0][None], mod_chunk(0, 1), mod_chunk(0, 0), seq)
    out = None
    for layer in range(depth):
        sh1, sc1, gt1, sh2, sc2, gt2 = (mod_chunk(layer, idx) for idx in range(6))
        zqkv = _matmul_heads(h, w_qkv, layer, 3 * attn_width // 2, qkv_scale)
        z32 = _matmul(h, w_pxyf, layer, F32, w_pxyf.shape[2]).reshape(batch, seq, w_pxyf.shape[2])

        fbias = _forget_bias(z32, f_col, bf_pad[layer][None], n_heads)
        y_attn = _attention(zqkv, fbias, batch, seq).reshape(batch * seq, attn_width)
        y_pool = _pool_mixer(z32, pool_w16, layer, pool_scale[layer][None], pool_width)
        lru_p = dict(conv_w=lru_conv_w[layer], conv_b=lru_conv_b[layer][None], wa=lru_wa16,
                     ba=lru_ba[layer][None], wi=lru_wi16, bi=lru_bi[layer][None], lam=lru_lambda[layer][None])
        y_lru = _lru_mixer(z32, layer, lru_p, lru_width, 1, 2)

        x2, h = _mix_out(y_pool.reshape(batch * seq, pool_width), y_attn, y_lru.reshape(batch * seq, lru_width),
                         w_out16, layer, x2, gt1, g_ffn[layer][None], sc2, sh2, seq)
        act = _ffn_gate(h, w_gate16, w_up16, layer, ffn_conv_w[layer], ffn_conv_b[layer][None], seq)
        if layer + 1 < depth:
            x2, h = _ffn_down(act, w_down16, layer, x2, gt2, g_mix[layer + 1][None], mod_chunk(layer + 1, 1),
                              mod_chunk(layer + 1, 0), seq, True, BF16)
        else:
            zeros = jnp.zeros((batch, 1, d), F32)
            (out,) = _ffn_down(act, w_down16, layer, x2, gt2, final_g[None], zeros, zeros, seq, False, F32)
    return out.reshape(batch, seq, d)
```

```python
import functools

import jax
import jax.numpy as jnp
from jax import lax
from jax.experimental import pallas as pl
from jax.experimental.pallas import tpu as pltpu

F32 = jnp.float32
BF16 = jnp.bfloat16

HEAD_DIM = 128
N_POOL_GROUPS = 4
POOL_WINDOWS = (2, 4, 8, 16)
POOL_HALO = 16
N_LRU_BLOCKS = 4
LRU_CONV_WIDTH = 4
LRU_C = 8.0
FFN_CONV_WIDTH = 3
EPS = 1e-6

LANES = 128
SUBLANES = 8
VMEM_LIMIT_CAP_V7X = 58 * 2**20
MASK_VALUE = -1e30
LOG2E = 1.4426950408889634


def _cparams(semantics, vmem_bytes):
    return pltpu.CompilerParams(dimension_semantics=semantics,
                                vmem_limit_bytes=int(min(VMEM_LIMIT_CAP_V7X, max(vmem_bytes, 16 * 2**20))))


def _sigmoid(x):
    return 0.5 * jnp.tanh(0.5 * x) + 0.5


def _norm_mod(x, g, sc, sh):
    ms = jnp.mean(x * x, axis=-1, keepdims=True)
    return (x * lax.rsqrt(ms + EPS) * g) * (1.0 + sc) + sh


def _split3_bf16(x):
    hi = x.astype(BF16)
    r1 = x - hi.astype(F32)
    mid = r1.astype(BF16)
    lo = (r1 - mid.astype(F32)).astype(BF16)
    return hi, mid, lo


def _ada_kernel(c_ref, w_ref, b_ref, o_ref, acc_ref):
    nb = c_ref.shape[0]
    n = w_ref.shape[3]
    k = pl.program_id(1)

    @pl.when(k == 0)
    def _():
        acc_ref[...] = jnp.zeros_like(acc_ref)

    ca = [c_ref[b] * _sigmoid(c_ref[b]) for b in range(nb)]

    def lane_chunk(jc, carry):
        sl = pl.ds(pl.multiple_of(jc * LANES, LANES), LANES)
        w = w_ref[0, :, :, sl]
        for b in range(nb):
            acc_ref[b, :, sl] += jnp.sum(w * ca[b], axis=0)
        return carry

    lax.fori_loop(0, n // LANES, lane_chunk, 0, unroll=4)

    @pl.when(k == pl.num_programs(1) - 1)
    def _():
        o_ref[...] = jnp.zeros_like(o_ref)
        for b in range(nb):
            o_ref[0, b:b + 1, :] = jnp.sum(acc_ref[b], axis=0, keepdims=True) + b_ref[0]


ADA_SLAB_ROWS = 256


def _ada_mod(c, w_ada, b_ada):
    depth, d, n = w_ada.shape
    batch = c.shape[0]
    dr = d // SUBLANES
    sr = ADA_SLAB_ROWS // SUBLANES
    c_lanes = jnp.broadcast_to(c[:, :, None], (batch, d, LANES)).reshape(batch, dr, SUBLANES, LANES)
    return pl.pallas_call(
        _ada_kernel,
        grid=(depth, dr // sr),
        in_specs=[pl.BlockSpec((batch, sr, SUBLANES, LANES), lambda l, k: (0, k, 0, 0)),
                  pl.BlockSpec((1, sr, SUBLANES, n), lambda l, k: (l, k, 0, 0)),
                  pl.BlockSpec((1, 1, n), lambda l, k: (l, 0, 0))],
        out_specs=pl.BlockSpec((1, SUBLANES, n), lambda l, k: (l, 0, 0)),
        out_shape=jax.ShapeDtypeStruct((depth, SUBLANES, n), F32),
        scratch_shapes=[pltpu.VMEM((batch, SUBLANES, n), F32)],
        compiler_params=_cparams(("parallel", "arbitrary"), 3 * ADA_SLAB_ROWS * n * 4),
        name="ada_mod",
    )(c_lanes, w_ada.reshape(depth, dr, SUBLANES, n), b_ada.reshape(depth, 1, n))


def _norm_kernel(x_ref, g_ref, sc_ref, sh_ref, h_ref):
    h_ref[...] = _norm_mod(x_ref[...], g_ref[...], sc_ref[0], sh_ref[0]).astype(h_ref.dtype)


def _norm(x2, g, sc, sh, seq):
    t, d = x2.shape
    tm = min(512, seq)
    tpb = seq // tm
    return pl.pallas_call(
        _norm_kernel,
        grid=(t // tm,),
        in_specs=[pl.BlockSpec((tm, d), lambda i: (i, 0)),
                  pl.BlockSpec((1, d), lambda i: (0, 0)),
                  pl.BlockSpec((1, 1, d), lambda i: (i // tpb, 0, 0)),
                  pl.BlockSpec((1, 1, d), lambda i: (i // tpb, 0, 0))],
        out_specs=pl.BlockSpec((tm, d), lambda i: (i, 0)),
        out_shape=jax.ShapeDtypeStruct((t, d), BF16),
        compiler_params=_cparams(("parallel",), 6 * tm * d * 4),
        name="norm_mod",
    )(x2, g, sc, sh)


def _mm_kernel(a_ref, w_ref, o_ref):
    o_ref[...] = jnp.dot(a_ref[...], w_ref[0], preferred_element_type=F32).astype(o_ref.dtype)


def _mm_heads_kernel(a_ref, w_ref, cs_ref, o_ref):
    acc = jnp.dot(a_ref[...], w_ref[0], preferred_element_type=F32) * cs_ref[...]
    for hh in range(o_ref.shape[0]):
        o_ref[hh] = acc[:, hh * HEAD_DIM:(hh + 1) * HEAD_DIM].astype(o_ref.dtype)


def _matmul(a, w_stack, layer, out_dtype, tn):
    t, k = a.shape
    n = w_stack.shape[2]
    tm = min(1024, t)
    vmem = 2 * (tm * k * 2 + k * tn * 2 + tm * tn * 4) + 2 * tm * tn * 4
    return pl.pallas_call(
        _mm_kernel,
        grid=(t // tm, n // tn),
        in_specs=[pl.BlockSpec((tm, k), lambda i, j: (i, 0)),
                  pl.BlockSpec((1, k, tn), lambda i, j: (layer, 0, j))],
        out_specs=pl.BlockSpec((tm, tn), lambda i, j: (i, j)),
        out_shape=jax.ShapeDtypeStruct((t, n), out_dtype),
        compiler_params=_cparams(("parallel", "parallel"), vmem),
        name="matmul_dmodel",
    )(a, w_stack)


def _matmul_heads(a, w_stack, layer, tn, col_scale):
    t, k = a.shape
    n = w_stack.shape[2]
    tm = min(1024, t)
    hpb = tn // HEAD_DIM
    vmem = 2 * (tm * k * 2 + k * tn * 2 + tm * tn * 2) + 2 * tm * tn * 4
    return pl.pallas_call(
        _mm_heads_kernel,
        grid=(t // tm, n // tn),
        in_specs=[pl.BlockSpec((tm, k), lambda i, j: (i, 0)),
                  pl.BlockSpec((1, k, tn), lambda i, j: (layer, 0, j)),
                  pl.BlockSpec((1, tn), lambda i, j: (0, j))],
        out_specs=pl.BlockSpec((hpb, tm, HEAD_DIM), lambda i, j: (j, i, 0)),
        out_shape=jax.ShapeDtypeStruct((n // HEAD_DIM, t, HEAD_DIM), BF16),
        compiler_params=_cparams(("parallel", "parallel"), vmem),
        name="matmul_heads",
    )(a, w_stack, col_scale)


N_SPLIT = 3


def _fcum_kernel(z_ref, b_ref, o_ref, carry_ref, tri_ref, place_ref):
    ts = z_ref.shape[1]
    w = z_ref.shape[2]
    wo = place_ref.shape[1]

    @pl.when((pl.program_id(0) == 0) & (pl.program_id(1) == 0))
    def _():
        r = lax.broadcasted_iota(jnp.int32, (ts, ts), 0)
        c = lax.broadcasted_iota(jnp.int32, (ts, ts), 1)
        tri_ref[...] = jnp.where(c <= r, 1.0, 0.0).astype(BF16)
        row = lax.broadcasted_iota(jnp.int32, (N_SPLIT * w, wo), 0)
        lane = lax.broadcasted_iota(jnp.int32, (N_SPLIT * w, wo), 1)
        target = jnp.zeros_like(row)
        for idx in range(N_SPLIT):
            in_part = (row >= idx * w) & (row < (idx + 1) * w)
            target = jnp.where(in_part, (row - idx * w) * HEAD_DIM + idx, target)
        place_ref[...] = jnp.where(lane == target, 1.0, 0.0).astype(BF16)

    @pl.when(pl.program_id(1) == 0)
    def _():
        carry_ref[...] = jnp.zeros_like(carry_ref)

    x = z_ref[0] + b_ref[...]
    lf = jnp.minimum(x, 0.0) - jnp.log(1.0 + jnp.exp(-jnp.abs(x)))
    parts = jnp.dot(tri_ref[...], jnp.concatenate(_split3_bf16(lf), axis=1), preferred_element_type=F32)
    cs = carry_ref[...] + parts[:, :w]
    for idx in range(1, N_SPLIT):
        cs = cs + parts[:, idx * w:(idx + 1) * w]
    carry_ref[...] = cs[ts - 1:ts, :]

    split = jnp.concatenate(_split3_bf16(cs * LOG2E), axis=1)
    bias = jnp.dot(split, place_ref[...], preferred_element_type=F32)
    for hh in range(o_ref.shape[0]):
        o_ref[hh, 0] = bias[:, hh * HEAD_DIM:(hh + 1) * HEAD_DIM].astype(o_ref.dtype)


def _forget_bias(z32, f_col, bf_pad, n_heads):
    b, s, _ = z32.shape
    w = LANES
    ts = min(512, s)
    wo = n_heads * HEAD_DIM
    return pl.pallas_call(
        _fcum_kernel,
        grid=(b, s // ts),
        in_specs=[pl.BlockSpec((1, ts, w), lambda bi, i: (bi, i, f_col)),
                  pl.BlockSpec((1, w), lambda bi, i: (0, 0))],
        out_specs=pl.BlockSpec((n_heads, 1, ts, HEAD_DIM), lambda bi, i: (0, bi, i, 0)),
        out_shape=jax.ShapeDtypeStruct((n_heads, b, s, HEAD_DIM), BF16),
        scratch_shapes=[pltpu.VMEM((1, w), F32), pltpu.VMEM((ts, ts), BF16), pltpu.VMEM((N_SPLIT * w, wo), BF16)],
        compiler_params=_cparams(("arbitrary", "arbitrary"), 4 * ts * ts * 4 + 8 * ts * wo * 4),
        name="forget_bias",
    )(z32, bf_pad)


ONES_ROWS = 16


MAX_CHAINS = 8
BIG_CHUNK = 4


def _attn_kernel(q_ref, k_ref, v_ref, fb_ref, o_ref, vt_ref, acc_ref, m_ref, *, tq):
    tk = tq
    s_len = k_ref.shape[1]
    step = pl.program_id(2)

    @pl.when(step == 0)
    def _():
        vt_ref[HEAD_DIM:, :] = jnp.ones((ONES_ROWS, s_len), BF16)
        for c in range(s_len // tk):
            vt_ref[:HEAD_DIM, c * tk:(c + 1) * tk] = v_ref[0, c * tk:(c + 1) * tk, :].astype(F32).T.astype(BF16)

    for sub in range(q_ref.shape[1] // tq):
        rows = slice(sub * tq, (sub + 1) * tq)
        o_ref[0, rows, :] = _attn_query_tile(step * (q_ref.shape[1] // tq) + sub, q_ref[0, rows, :], k_ref, fb_ref,
                                             vt_ref, acc_ref, m_ref).astype(o_ref.dtype)


def _attn_query_tile(qi, q, k_ref, fb_ref, vt_ref, acc_ref, m_ref):
    tq = q.shape[0]
    tk = tq
    lane = lax.broadcasted_iota(jnp.int32, (tq, HEAD_DIM), 1)
    minus_one = jnp.where(lane < 3, -1.0, 0.0).astype(BF16)
    q_aug = jnp.concatenate([q, minus_one], axis=1)
    m_ref[...] = jnp.full(m_ref.shape, MASK_VALUE, F32)
    acc_ref[...] = jnp.zeros_like(acc_ref)

    def logits(k0, size):
        k0 = pl.multiple_of(k0, tk)
        k_aug = jnp.concatenate([k_ref[0, pl.ds(k0, size), :], fb_ref[0, 0, pl.ds(k0, size), :]], axis=1)
        return lax.dot_general(k_aug, q_aug, (((1,), (1,)), ((), ())), preferred_element_type=F32)

    def absorb(st, k0, size, masked):
        k0 = pl.multiple_of(k0, tk)
        if masked:
            krow = lax.broadcasted_iota(jnp.int32, (size, tq), 0)
            qcol = lax.broadcasted_iota(jnp.int32, (size, tq), 1)
            st = jnp.where(krow <= qcol, st, MASK_VALUE)
        part = jnp.max(st.reshape(MAX_CHAINS, size // MAX_CHAINS, tq), axis=1)
        m_old = m_ref[...]
        m_new = jnp.maximum(m_old, jnp.max(part, axis=0, keepdims=True))
        m_ref[...] = m_new
        pt = jnp.exp2((st - m_new).astype(BF16))
        acc_ref[...] = jnp.exp2(m_old - m_new) * acc_ref[...] + jnp.dot(
            vt_ref[:, pl.ds(k0, size)], pt, preferred_element_type=F32)

    def pair(k0, size_a, size_b, mask_b):
        st_a = logits(k0, size_a)
        st_b = logits(k0 + size_a, size_b)
        absorb(st_a, k0, size_a, False)
        absorb(st_b, k0 + size_a, size_b, mask_b)

    def body(kc, carry):
        half = BIG_CHUNK * tk // 2
        pair(kc * (BIG_CHUNK * tk), half, half, False)
        return carry

    n_big = qi // BIG_CHUNK
    lax.fori_loop(0, n_big, body, 0)
    tail_k0 = n_big * (BIG_CHUNK * tk)
    for visible in range(BIG_CHUNK):
        @pl.when(qi - n_big * BIG_CHUNK == visible)
        def _(visible=visible):
            if visible == 0:
                absorb(logits(tail_k0, tk), tail_k0, tk, True)
            else:
                pair(tail_k0, visible * tk, tk, True)

    o_t = acc_ref[:HEAD_DIM, :] * (1.0 / acc_ref[HEAD_DIM:HEAD_DIM + 1, :])
    return o_t.T


def _attention(zqkv, fbias, batch, seq):
    n_heads = fbias.shape[0]
    b, s = batch, seq
    tq = min(512, s)
    tiles_per_step = 2 if s % (2 * tq) == 0 else 1
    tqs = tq * tiles_per_step
    spb = s // tqs
    assert s % tqs == 0
    vmem = 2 * (3 * s * HEAD_DIM * 2) + (HEAD_DIM + ONES_ROWS) * s * 2 + 6 * BIG_CHUNK * tq * tq * 4
    kv_spec = lambda off: pl.BlockSpec((1, s, HEAD_DIM), lambda bi, h, i: (off + h, bi, 0))
    return pl.pallas_call(
        functools.partial(_attn_kernel, tq=tq),
        grid=(b, n_heads, spb),
        in_specs=[pl.BlockSpec((1, tqs, HEAD_DIM), lambda bi, h, i: (h, bi * spb + i, 0)),
                  kv_spec(n_heads), kv_spec(2 * n_heads),
                  pl.BlockSpec((1, 1, s, HEAD_DIM), lambda bi, h, i: (h, bi, 0, 0))],
        out_specs=pl.BlockSpec((1, tqs, HEAD_DIM), lambda bi, h, i: (bi, i, h)),
        out_shape=jax.ShapeDtypeStruct((b, s, n_heads * HEAD_DIM), BF16),
        scratch_shapes=[pltpu.VMEM((HEAD_DIM + ONES_ROWS, s), BF16),
                        pltpu.VMEM((HEAD_DIM + ONES_ROWS, tq), F32),
                        pltpu.VMEM((1, tq), F32)],
        compiler_params=_cparams(("parallel", "parallel", "arbitrary"), vmem),
        name="fox_attention",
    )(zqkv, zqkv, zqkv, fbias)


def _pool_kernel(u_ref, halo_ref, w_ref, sc_ref, o_ref):
    tm = u_ref.shape[1]
    i = pl.program_id(1)
    halo = jnp.where(i > 0, halo_ref[0], 0.0)
    ext = jnp.concatenate([halo, u_ref[0]], axis=0)
    pos = (i * tm + 1 + lax.broadcasted_iota(jnp.int32, (tm, 1), 0)).astype(F32)
    gd = LANES
    for g, win in enumerate(POOL_WINDOWS):
        e = ext[:, g * gd:(g + 1) * gd]
        ssum = e
        shift = 1
        while shift < win:
            ssum = ssum + pltpu.roll(ssum, shift, 0)
            shift *= 2
        mean = ssum[POOL_HALO:] * (1.0 / jnp.minimum(pos, float(win)))
        dlt = mean - e[POOL_HALO:]
        y = jnp.dot(dlt.astype(BF16), w_ref[0, g], preferred_element_type=F32)
        o_ref[0, :, g * gd:(g + 1) * gd] = (y * sc_ref[:, g * gd:(g + 1) * gd]).astype(o_ref.dtype)


def _pool_mixer(z32, pool_w_stack, layer, scale_row, width):
    b, s, _ = z32.shape
    tm = min(512, s)
    hb = tm // POOL_HALO
    return pl.pallas_call(
        _pool_kernel,
        grid=(b, s // tm),
        in_specs=[pl.BlockSpec((1, tm, width), lambda bi, i: (bi, i, 0)),
                  pl.BlockSpec((1, POOL_HALO, width), lambda bi, i: (bi, jnp.maximum(i * hb - 1, 0), 0)),
                  pl.BlockSpec((1, N_POOL_GROUPS, LANES, LANES), lambda bi, i: (layer, 0, 0, 0)),
                  pl.BlockSpec((1, width), lambda bi, i: (0, 0))],
        out_specs=pl.BlockSpec((1, tm, width), lambda bi, i: (bi, i, 0)),
        out_shape=jax.ShapeDtypeStruct((b, s, width), BF16),
        compiler_params=_cparams(("parallel", "parallel"), 16 * tm * width * 4),
        name="pool_mixer",
    )(z32, z32, pool_w_stack, scale_row)


def _lru_kernel(x_ref, halo_ref, y_ref, cw_ref, cb_ref, wa_ref, ba_ref, wi_ref, bi_ref, lam_ref,
                o_ref, h_ref, a_s, b_s, h_s):
    tm = x_ref.shape[1]
    i = pl.program_id(1)

    @pl.when(i == 0)
    def _():
        h_ref[...] = jnp.zeros_like(h_ref)

    halo = jnp.where(i > 0, halo_ref[0], 0.0)
    ext = jnp.concatenate([halo, x_ref[0]], axis=0)
    xc = cb_ref[...] + cw_ref[LRU_CONV_WIDTH - 1:LRU_CONV_WIDTH, :] * ext[SUBLANES:]
    for k in range(LRU_CONV_WIDTH - 1):
        shifted = pltpu.roll(ext, LRU_CONV_WIDTH - 1 - k, 0)[SUBLANES:]
        xc = xc + cw_ref[k:k + 1, :] * shifted

    lam = lam_ref[...]
    neg_softplus = -(jnp.maximum(-lam, 0.0) + jnp.log(1.0 + jnp.exp(-jnp.abs(lam))))
    gd = LANES
    n_groups = tm // SUBLANES
    sub_row = lax.broadcasted_iota(jnp.int32, (n_groups, SUBLANES, gd), 1)
    for blk in range(N_LRU_BLOCKS):
        sl = slice(blk * gd, (blk + 1) * gd)
        xb = xc[:, sl]
        xb16 = xb.astype(BF16)
        gate_r = _sigmoid(jnp.dot(xb16, wa_ref[0, blk], preferred_element_type=F32) + ba_ref[:, sl])
        gate_i = _sigmoid(jnp.dot(xb16, wi_ref[0, blk], preferred_element_type=F32) + bi_ref[:, sl])
        log_a = LRU_C * gate_r * neg_softplus[:, sl]
        a = jnp.exp(log_a)
        b = jnp.sqrt(1.0 - a * a) * (gate_i * xb)
        a = a.reshape(n_groups, SUBLANES, gd)
        b = b.reshape(n_groups, SUBLANES, gd)
        dist = 1
        while dist < SUBLANES:
            has_prev = sub_row >= dist
            b = b + a * jnp.where(has_prev, pltpu.roll(b, dist, 1), 0.0)
            a = a * jnp.where(has_prev, pltpu.roll(a, dist, 1), 1.0)
            dist *= 2
        a_s[blk] = a
        b_s[blk] = b

    def group(gi, hs):
        out = []
        for blk in range(N_LRU_BLOCKS):
            rows = b_s[blk, gi] + a_s[blk, gi] * hs[blk]
            h_s[blk, gi] = rows
            out.append(rows[SUBLANES - 1:SUBLANES, :])
        return tuple(out)

    hs = lax.fori_loop(0, n_groups, group,
                       tuple(h_ref[:, blk * gd:(blk + 1) * gd] for blk in range(N_LRU_BLOCKS)), unroll=8)

    for blk in range(N_LRU_BLOCKS):
        sl = slice(blk * gd, (blk + 1) * gd)
        h_ref[:, sl] = hs[blk]
        y = y_ref[0, :, sl]
        gelu = 0.5 * y * (1.0 + jnp.tanh(0.7978845608028654 * (y + 0.044715 * (y * y * y))))
        o_ref[0, :, sl] = (h_s[blk].reshape(tm, gd) * gelu).astype(o_ref.dtype)


def _lru_mixer(z32, layer, p, width, x_col, y_col):
    b, s, _ = z32.shape
    tm = min(512, s)
    hb = tm // SUBLANES
    row = lambda bi, i: (0, 0)
    return pl.pallas_call(
        _lru_kernel,
        grid=(b, s // tm),
        in_specs=[pl.BlockSpec((1, tm, width), lambda bi, i: (bi, i, x_col)),
                  pl.BlockSpec((1, SUBLANES, width), lambda bi, i: (bi, jnp.maximum(i * hb - 1, 0), x_col)),
                  pl.BlockSpec((1, tm, width), lambda bi, i: (bi, i, y_col)),
                  pl.BlockSpec((LRU_CONV_WIDTH, width), row),
                  pl.BlockSpec((1, width), row),
                  pl.BlockSpec((1, N_LRU_BLOCKS, LANES, LANES), lambda bi, i: (layer, 0, 0, 0)),
                  pl.BlockSpec((1, width), row),
                  pl.BlockSpec((1, N_LRU_BLOCKS, LANES, LANES), lambda bi, i: (layer, 0, 0, 0)),
                  pl.BlockSpec((1, width), row),
                  pl.BlockSpec((1, width), row)],
        out_specs=pl.BlockSpec((1, tm, width), lambda bi, i: (bi, i, 0)),
        out_shape=jax.ShapeDtypeStruct((b, s, width), BF16),
        scratch_shapes=[pltpu.VMEM((1, width), F32)]
        + [pltpu.VMEM((N_LRU_BLOCKS, tm // SUBLANES, SUBLANES, LANES), F32)] * 3,
        compiler_params=_cparams(("arbitrary", "arbitrary"), 24 * tm * width * 4),
        name="rg_lru",
    )(z32, z32, z32, p["conv_w"], p["conv_b"], p["wa"], p["ba"], p["wi"], p["bi"], p["lam"])


COL_BLOCK = 512


def _col_blocks(d):
    return [slice(c, c + COL_BLOCK) for c in range(0, d, COL_BLOCK)]


def _residual_cols(acc, cols, x_ref, gt_ref, xn_refs):
    xn = x_ref[:, cols] + gt_ref[0, :, cols] * acc
    for ref in xn_refs:
        ref[:, cols] = xn
    return jnp.sum(xn * xn, axis=-1, keepdims=True)


def _norm_rows(sumsq, xn_ref, g_ref, sc_ref, sh_ref, h_ref):
    rs = lax.rsqrt(sumsq * (1.0 / xn_ref.shape[1]) + EPS)
    gain = g_ref[...] * (1.0 + sc_ref[0])
    h_ref[...] = (xn_ref[...] * rs * gain + sh_ref[0]).astype(h_ref.dtype)


def _mix_out_kernel(yp_ref, ya_ref, yl_ref, w_ref, x_ref, gt_ref, g_ref, sc_ref, sh_ref, xo_ref, h_ref):
    kp = yp_ref.shape[1]
    ka = ya_ref.shape[1]
    sumsq = jnp.zeros((x_ref.shape[0], 1), F32)
    for cols in _col_blocks(x_ref.shape[1]):
        acc = jnp.dot(yp_ref[...], w_ref[0, 0:kp, cols], preferred_element_type=F32)
        acc = acc + jnp.dot(ya_ref[...], w_ref[0, kp:kp + ka, cols], preferred_element_type=F32)
        acc = acc + jnp.dot(yl_ref[...], w_ref[0, kp + ka:, cols], preferred_element_type=F32)
        sumsq = sumsq + _residual_cols(acc, cols, x_ref, gt_ref, (xo_ref,))
    _norm_rows(sumsq, xo_ref, g_ref, sc_ref, sh_ref, h_ref)


def _mix_out(yp, ya, yl, w_stack, layer, x2, gt, g, sc, sh, seq):
    t, d = x2.shape
    tm = min(512, seq)
    tpb = seq // tm
    per_b = lambda i: (i // tpb, 0, 0)
    lhs = lambda y: pl.BlockSpec((tm, y.shape[1]), lambda i: (i, 0))
    vmem = 2 * d * d * 2 + 2 * (tm * d * 2 + 3 * tm * d * 4) + 3 * tm * d * 4
    return pl.pallas_call(
        _mix_out_kernel,
        grid=(t // tm,),
        in_specs=[lhs(yp), lhs(ya), lhs(yl),
                  pl.BlockSpec((1, d, d), lambda i: (layer, 0, 0)),
                  pl.BlockSpec((tm, d), lambda i: (i, 0)),
                  pl.BlockSpec((1, 1, d), per_b),
                  pl.BlockSpec((1, d), lambda i: (0, 0)),
                  pl.BlockSpec((1, 1, d), per_b),
                  pl.BlockSpec((1, 1, d), per_b)],
        out_specs=[pl.BlockSpec((tm, d), lambda i: (i, 0)), pl.BlockSpec((tm, d), lambda i: (i, 0))],
        out_shape=[jax.ShapeDtypeStruct((t, d), F32), jax.ShapeDtypeStruct((t, d), BF16)],
        compiler_params=_cparams(("parallel",), vmem),
        name="mix_out_proj",
    )(yp, ya, yl, w_stack, x2, gt, g, sc, sh)


def _ffn_down_kernel(a_ref, w_ref, x_ref, gt_ref, g_ref, sc_ref, sh_ref, *out_refs):
    xn_ref, h_ref = out_refs[0], out_refs[-1]
    sumsq = jnp.zeros((x_ref.shape[0], 1), F32)
    for cols in _col_blocks(x_ref.shape[1]):
        acc = jnp.dot(a_ref[...], w_ref[0, :, cols], preferred_element_type=F32)
        sumsq = sumsq + _residual_cols(acc, cols, x_ref, gt_ref, (xn_ref,))
    _norm_rows(sumsq, xn_ref, g_ref, sc_ref, sh_ref, h_ref)


def _ffn_down(act, w_stack, layer, x2, gt, g, sc, sh, seq, emit_x, h_dtype):
    t, d = x2.shape
    kf = act.shape[1]
    tm = min(512, seq)
    tpb = seq // tm
    per_b = lambda i: (i // tpb, 0, 0)
    row_tile = pl.BlockSpec((tm, d), lambda i: (i, 0))
    out_specs = [row_tile]
    out_shape = [jax.ShapeDtypeStruct((t, d), h_dtype)]
    if emit_x:
        out_specs = [row_tile, row_tile]
        out_shape = [jax.ShapeDtypeStruct((t, d), F32)] + out_shape
    else:
        assert h_dtype == F32
    vmem = kf * d * 2 + 2 * (tm * kf * 2 + 3 * tm * d * 4) + 2 * tm * COL_BLOCK * 4
    return pl.pallas_call(
        _ffn_down_kernel,
        grid=(t // tm,),
        in_specs=[pl.BlockSpec((tm, kf), lambda i: (i, 0)),
                  pl.BlockSpec((1, kf, d), lambda i: (layer, 0, 0), pipeline_mode=pl.Buffered(1)),
                  row_tile,
                  pl.BlockSpec((1, 1, d), per_b),
                  pl.BlockSpec((1, d), lambda i: (0, 0)),
                  pl.BlockSpec((1, 1, d), per_b),
                  pl.BlockSpec((1, 1, d), per_b)],
        out_specs=out_specs,
        out_shape=out_shape,
        compiler_params=_cparams(("parallel",), vmem),
        name="ffn_down_proj",
    )(act, w_stack, x2, gt, g, sc, sh)


def _ffn_gate_kernel(h_ref, wg_ref, wu_ref, cw_ref, cb_ref, o_ref, carry_ref, *, tiles_per_batch):
    tm = h_ref.shape[0]
    i = pl.program_id(0)
    j = pl.program_id(1)

    @pl.when(i % tiles_per_batch == 0)
    def _():
        carry_ref[j] = jnp.zeros(carry_ref.shape[1:], F32)

    h = h_ref[...]
    u = jnp.dot(h, wg_ref[0].astype(BF16), preferred_element_type=F32)
    ext = jnp.concatenate([carry_ref[j], u], axis=0)
    carry_ref[j] = u[tm - SUBLANES:, :]
    g = cb_ref[...] + cw_ref[FFN_CONV_WIDTH - 1:FFN_CONV_WIDTH, :] * u
    for k in range(FFN_CONV_WIDTH - 1):
        shifted = pltpu.roll(ext, FFN_CONV_WIDTH - 1 - k, 0)[SUBLANES:]
        g = g + cw_ref[k:k + 1, :] * shifted
    up = jnp.dot(h, wu_ref[0].astype(BF16), preferred_element_type=F32)
    o_ref[...] = (g * _sigmoid(g) * up).astype(o_ref.dtype)


def _ffn_gate(h, wg_stack, wu_stack, layer, conv_w, conv_b, seq):
    t, d = h.shape
    kf = wg_stack.shape[2]
    tm = min(1024, seq)
    tf = 512
    wspec = pl.BlockSpec((1, d, tf), lambda i, j: (layer, 0, j))
    vmem = 2 * (tm * d * 2 + 2 * d * tf * 4 + tm * tf * 2) + 2 * d * tf * 2 + 8 * tm * tf * 4
    return pl.pallas_call(
        functools.partial(_ffn_gate_kernel, tiles_per_batch=seq // tm),
        grid=(t // tm, kf // tf),
        in_specs=[pl.BlockSpec((tm, d), lambda i, j: (i, 0)), wspec, wspec,
                  pl.BlockSpec((FFN_CONV_WIDTH, tf), lambda i, j: (0, j)),
                  pl.BlockSpec((1, tf), lambda i, j: (0, j))],
        out_specs=pl.BlockSpec((tm, tf), lambda i, j: (i, j)),
        out_shape=jax.ShapeDtypeStruct((t, kf), BF16),
        scratch_shapes=[pltpu.VMEM((kf // tf, SUBLANES, tf), F32)],
        compiler_params=_cparams(("arbitrary", "arbitrary"), vmem),
        name="ffn_gate_up",
    )(h, wg_stack, wu_stack, conv_w, conv_b)


def kernel(x, c, w_ada, b_ada, g_mix, w_in, b_f, pool_w, pool_scale, lru_conv_w, lru_conv_b, lru_wa, lru_ba,
           lru_wi, lru_bi, lru_lambda, w_out, g_ffn, w_ffn_gate, w_ffn_up, ffn_conv_w, ffn_conv_b, w_ffn_down,
           final_g):
    batch, seq, d = x.shape
    depth = w_ada.shape[0]
    pool_width = pool_w.shape[1] * pool_w.shape[2]
    lru_width = lru_lambda.shape[1]
    n_heads = b_f.shape[1]
    attn_width = n_heads * HEAD_DIM
    assert w_in.shape[2] == pool_width + 3 * attn_width + n_heads + 2 * lru_width
    assert pool_width == lru_width == N_POOL_GROUPS * LANES and seq % SUBLANES == 0

    o_q = pool_width
    o_f = o_q + 3 * attn_width
    o_x = o_f + n_heads
    w_qkv = w_in[:, :, o_q:o_f].astype(BF16)
    w_f = jnp.pad(w_in[:, :, o_f:o_x], ((0, 0), (0, 0), (0, LANES - n_heads)))
    w_pxyf = jnp.concatenate([w_in[:, :, :o_q], w_in[:, :, o_x:], w_f], axis=2).astype(BF16)
    f_col = (pool_width + 2 * lru_width) // LANES
    bf_pad = jnp.pad(b_f, ((0, 0), (0, LANES - n_heads)))
    qkv_scale = jnp.concatenate([jnp.full((1, attn_width), HEAD_DIM ** -0.5 * LOG2E, F32),
                                 jnp.ones((1, 2 * attn_width), F32)], axis=1)
    pool_w16 = pool_w.astype(BF16)
    lru_wa16 = lru_wa.astype(BF16)
    lru_wi16 = lru_wi.astype(BF16)
    w_out16 = w_out.astype(BF16)
    w_down16 = w_ffn_down.astype(BF16)

    mod = _ada_mod(c, w_ada, b_ada)[:, :batch]

    def mod_chunk(layer, idx):
        return mod[layer, :, idx * d:(idx + 1) * d].reshape(batch, 1, d)

    x2 = x.reshape(batch * seq, d)
    h = _norm(x2, g_mix[0][None], mod_chunk(0, 1), mod_chunk(0, 0), seq)
    out = None
    for layer in range(depth):
        sh1, sc1, gt1, sh2, sc2, gt2 = (mod_chunk(layer, idx) for idx in range(6))
        zqkv = _matmul_heads(h, w_qkv, layer, 3 * attn_width // 2, qkv_scale)
        z32 = _matmul(h, w_pxyf, layer, F32, w_pxyf.shape[2]).reshape(batch, seq, w_pxyf.shape[2])

        fbias = _forget_bias(z32, f_col, bf_pad[layer][None], n_heads)
        y_attn = _attention(zqkv, fbias, batch, seq).reshape(batch * seq, attn_width)
        y_pool = _pool_mixer(z32, pool_w16, layer, pool_scale[layer][None], pool_width)
        lru_p = dict(conv_w=lru_conv_w[layer], conv_b=lru_conv_b[layer][None], wa=lru_wa16,
                     ba=lru_ba[layer][None], wi=lru_wi16, bi=lru_bi[layer][None], lam=lru_lambda[layer][None])
        y_lru = _lru_mixer(z32, layer, lru_p, lru_width, 1, 2)

        x2, h = _mix_out(y_pool.reshape(batch * seq, pool_width), y_attn, y_lru.reshape(batch * seq, lru_width),
                         w_out16, layer, x2, gt1, g_ffn[layer][None], sc2, sh2, seq)
        act = _ffn_gate(h, w_ffn_gate, w_ffn_up, layer, ffn_conv_w[layer], ffn_conv_b[layer][None], seq)
        if layer + 1 < depth:
            x2, h = _ffn_down(act, w_down16, layer, x2, gt2, g_mix[layer + 1][None], mod_chunk(layer + 1, 1),
                              mod_chunk(layer + 1, 0), seq, True, BF16)
        else:
            zeros = jnp.zeros((batch, 1, d), F32)
            (out,) = _ffn_down(act, w_down16, layer, x2, gt2, final_g[None], zeros, zeros, seq, False, F32)
    return out.reshape(batch, seq, d)
```

```python
import functools

import jax
import jax.numpy as jnp
from jax import lax
from jax.experimental import pallas as pl
from jax.experimental.pallas import tpu as pltpu

F32 = jnp.float32
BF16 = jnp.bfloat16

HEAD_DIM = 128
N_POOL_GROUPS = 4
POOL_WINDOWS = (2, 4, 8, 16)
POOL_HALO = 16
N_LRU_BLOCKS = 4
LRU_CONV_WIDTH = 4
LRU_C = 8.0
FFN_CONV_WIDTH = 3
EPS = 1e-6

LANES = 128
SUBLANES = 8
VMEM_LIMIT_CAP_V7X = 58 * 2**20
MASK_VALUE = -1e30
LOG2E = 1.4426950408889634


def _cparams(semantics, vmem_bytes):
    return pltpu.CompilerParams(dimension_semantics=semantics,
                                vmem_limit_bytes=int(min(VMEM_LIMIT_CAP_V7X, max(vmem_bytes, 16 * 2**20))))


def _sigmoid(x):
    return 0.5 * jnp.tanh(0.5 * x) + 0.5


def _norm_mod(x, g, sc, sh):
    ms = jnp.mean(x * x, axis=-1, keepdims=True)
    return (x * lax.rsqrt(ms + EPS) * g) * (1.0 + sc) + sh


def _split3_bf16(x):
    hi = x.astype(BF16)
    r1 = x - hi.astype(F32)
    mid = r1.astype(BF16)
    lo = (r1 - mid.astype(F32)).astype(BF16)
    return hi, mid, lo


def _ada_kernel(c_ref, w_ref, b_ref, o_ref, acc_ref):
    nb = c_ref.shape[0]
    n = w_ref.shape[3]
    k = pl.program_id(1)

    @pl.when(k == 0)
    def _():
        acc_ref[...] = jnp.zeros_like(acc_ref)

    ca = [c_ref[b] * _sigmoid(c_ref[b]) for b in range(nb)]

    def lane_chunk(jc, carry):
        sl = pl.ds(pl.multiple_of(jc * LANES, LANES), LANES)
        w = w_ref[0, :, :, sl]
        for b in range(nb):
            acc_ref[b, :, sl] += jnp.sum(w * ca[b], axis=0)
        return carry

    lax.fori_loop(0, n // LANES, lane_chunk, 0, unroll=4)

    @pl.when(k == pl.num_programs(1) - 1)
    def _():
        o_ref[...] = jnp.zeros_like(o_ref)
        for b in range(nb):
            o_ref[0, b:b + 1, :] = jnp.sum(acc_ref[b], axis=0, keepdims=True) + b_ref[0]


ADA_SLAB_ROWS = 256


def _ada_mod(c, w_ada, b_ada):
    depth, d, n = w_ada.shape
    batch = c.shape[0]
    dr = d // SUBLANES
    sr = ADA_SLAB_ROWS // SUBLANES
    c_lanes = jnp.broadcast_to(c[:, :, None], (batch, d, LANES)).reshape(batch, dr, SUBLANES, LANES)
    return pl.pallas_call(
        _ada_kernel,
        grid=(depth, dr // sr),
        in_specs=[pl.BlockSpec((batch, sr, SUBLANES, LANES), lambda l, k: (0, k, 0, 0)),
                  pl.BlockSpec((1, sr, SUBLANES, n), lambda l, k: (l, k, 0, 0)),
                  pl.BlockSpec((1, 1, n), lambda l, k: (l, 0, 0))],
        out_specs=pl.BlockSpec((1, SUBLANES, n), lambda l, k: (l, 0, 0)),
        out_shape=jax.ShapeDtypeStruct((depth, SUBLANES, n), F32),
        scratch_shapes=[pltpu.VMEM((batch, SUBLANES, n), F32)],
        compiler_params=_cparams(("parallel", "arbitrary"), 3 * ADA_SLAB_ROWS * n * 4),
        name="ada_mod",
    )(c_lanes, w_ada.reshape(depth, dr, SUBLANES, n), b_ada.reshape(depth, 1, n))


def _norm_kernel(x_ref, g_ref, sc_ref, sh_ref, h_ref):
    h_ref[...] = _norm_mod(x_ref[...], g_ref[...], sc_ref[0], sh_ref[0]).astype(h_ref.dtype)


def _norm(x2, g, sc, sh, seq):
    t, d = x2.shape
    tm = min(512, seq)
    tpb = seq // tm
    return pl.pallas_call(
        _norm_kernel,
        grid=(t // tm,),
        in_specs=[pl.BlockSpec((tm, d), lambda i: (i, 0)),
                  pl.BlockSpec((1, d), lambda i: (0, 0)),
                  pl.BlockSpec((1, 1, d), lambda i: (i // tpb, 0, 0)),
                  pl.BlockSpec((1, 1, d), lambda i: (i // tpb, 0, 0))],
        out_specs=pl.BlockSpec((tm, d), lambda i: (i, 0)),
        out_shape=jax.ShapeDtypeStruct((t, d), BF16),
        compiler_params=_cparams(("parallel",), 6 * tm * d * 4),
        name="norm_mod",
    )(x2, g, sc, sh)


def _mm_kernel(a_ref, w_ref, o_ref):
    o_ref[...] = jnp.dot(a_ref[...], w_ref[0], preferred_element_type=F32).astype(o_ref.dtype)


def _mm_heads_kernel(a_ref, w_ref, cs_ref, o_ref):
    acc = jnp.dot(a_ref[...], w_ref[0], preferred_element_type=F32) * cs_ref[...]
    for hh in range(o_ref.shape[0]):
        o_ref[hh] = acc[:, hh * HEAD_DIM:(hh + 1) * HEAD_DIM].astype(o_ref.dtype)


def _matmul(a, w_stack, layer, out_dtype, tn):
    t, k = a.shape
    n = w_stack.shape[2]
    tm = min(1024, t)
    vmem = 2 * (tm * k * 2 + k * tn * 2 + tm * tn * 4) + 2 * tm * tn * 4
    return pl.pallas_call(
        _mm_kernel,
        grid=(t // tm, n // tn),
        in_specs=[pl.BlockSpec((tm, k), lambda i, j: (i, 0)),
                  pl.BlockSpec((1, k, tn), lambda i, j: (layer, 0, j))],
        out_specs=pl.BlockSpec((tm, tn), lambda i, j: (i, j)),
        out_shape=jax.ShapeDtypeStruct((t, n), out_dtype),
        compiler_params=_cparams(("parallel", "parallel"), vmem),
        name="matmul_dmodel",
    )(a, w_stack)


def _matmul_heads(a, w_stack, layer, tn, col_scale):
    t, k = a.shape
    n = w_stack.shape[2]
    tm = min(1024, t)
    hpb = tn // HEAD_DIM
    vmem = 2 * (tm * k * 2 + k * tn * 2 + tm * tn * 2) + 2 * tm * tn * 4
    return pl.pallas_call(
        _mm_heads_kernel,
        grid=(t // tm, n // tn),
        in_specs=[pl.BlockSpec((tm, k), lambda i, j: (i, 0)),
                  pl.BlockSpec((1, k, tn), lambda i, j: (layer, 0, j)),
                  pl.BlockSpec((1, tn), lambda i, j: (0, j))],
        out_specs=pl.BlockSpec((hpb, tm, HEAD_DIM), lambda i, j: (j, i, 0)),
        out_shape=jax.ShapeDtypeStruct((n // HEAD_DIM, t, HEAD_DIM), BF16),
        compiler_params=_cparams(("parallel", "parallel"), vmem),
        name="matmul_heads",
    )(a, w_stack, col_scale)


N_SPLIT = 3


def _fcum_kernel(z_ref, b_ref, o_ref, carry_ref, tri_ref, place_ref):
    ts = z_ref.shape[1]
    w = z_ref.shape[2]
    wo = place_ref.shape[1]

    @pl.when((pl.program_id(0) == 0) & (pl.program_id(1) == 0))
    def _():
        r = lax.broadcasted_iota(jnp.int32, (ts, ts), 0)
        c = lax.broadcasted_iota(jnp.int32, (ts, ts), 1)
        tri_ref[...] = jnp.where(c <= r, 1.0, 0.0).astype(BF16)
        row = lax.broadcasted_iota(jnp.int32, (N_SPLIT * w, wo), 0)
        lane = lax.broadcasted_iota(jnp.int32, (N_SPLIT * w, wo), 1)
        target = jnp.zeros_like(row)
        for idx in range(N_SPLIT):
            in_part = (row >= idx * w) & (row < (idx + 1) * w)
            target = jnp.where(in_part, (row - idx * w) * HEAD_DIM + idx, target)
        place_ref[...] = jnp.where(lane == target, 1.0, 0.0).astype(BF16)

    @pl.when(pl.program_id(1) == 0)
    def _():
        carry_ref[...] = jnp.zeros_like(carry_ref)

    x = z_ref[0] + b_ref[...]
    lf = jnp.minimum(x, 0.0) - jnp.log(1.0 + jnp.exp(-jnp.abs(x)))
    parts = jnp.dot(tri_ref[...], jnp.concatenate(_split3_bf16(lf), axis=1), preferred_element_type=F32)
    cs = carry_ref[...] + parts[:, :w]
    for idx in range(1, N_SPLIT):
        cs = cs + parts[:, idx * w:(idx + 1) * w]
    carry_ref[...] = cs[ts - 1:ts, :]

    split = jnp.concatenate(_split3_bf16(cs * LOG2E), axis=1)
    bias = jnp.dot(split, place_ref[...], preferred_element_type=F32)
    for hh in range(o_ref.shape[0]):
        o_ref[hh, 0] = bias[:, hh * HEAD_DIM:(hh + 1) * HEAD_DIM].astype(o_ref.dtype)


def _forget_bias(z32, f_col, bf_pad, n_heads):
    b, s, _ = z32.shape
    w = LANES
    ts = min(512, s)
    wo = n_heads * HEAD_DIM
    return pl.pallas_call(
        _fcum_kernel,
        grid=(b, s // ts),
        in_specs=[pl.BlockSpec((1, ts, w), lambda bi, i: (bi, i, f_col)),
                  pl.BlockSpec((1, w), lambda bi, i: (0, 0))],
        out_specs=pl.BlockSpec((n_heads, 1, ts, HEAD_DIM), lambda bi, i: (0, bi, i, 0)),
        out_shape=jax.ShapeDtypeStruct((n_heads, b, s, HEAD_DIM), BF16),
        scratch_shapes=[pltpu.VMEM((1, w), F32), pltpu.VMEM((ts, ts), BF16), pltpu.VMEM((N_SPLIT * w, wo), BF16)],
        compiler_params=_cparams(("arbitrary", "arbitrary"), 4 * ts * ts * 4 + 8 * ts * wo * 4),
        name="forget_bias",
    )(z32, bf_pad)


ONES_ROWS = 16


MAX_CHAINS = 8
BIG_CHUNK = 4
STALE_MAX_GUARD = 60.0


def _attn_kernel(q_ref, k_ref, v_ref, fb_ref, o_ref, vt_ref, acc_ref, m_ref, *, tq):
    tk = tq
    s_len = k_ref.shape[1]
    step = pl.program_id(2)

    @pl.when(step == 0)
    def _():
        vt_ref[HEAD_DIM:, :] = jnp.ones((ONES_ROWS, s_len), BF16)
        for c in range(s_len // tk):
            vt_ref[:HEAD_DIM, c * tk:(c + 1) * tk] = v_ref[0, c * tk:(c + 1) * tk, :].astype(F32).T.astype(BF16)

    for sub in range(q_ref.shape[1] // tq):
        rows = slice(sub * tq, (sub + 1) * tq)
        o_ref[0, rows, :] = _attn_query_tile(step * (q_ref.shape[1] // tq) + sub, q_ref[0, rows, :], k_ref, fb_ref,
                                             vt_ref, acc_ref, m_ref).astype(o_ref.dtype)


def _attn_query_tile(qi, q, k_ref, fb_ref, vt_ref, acc_ref, m_ref):
    tq = q.shape[0]
    tk = tq
    lane = lax.broadcasted_iota(jnp.int32, (tq, HEAD_DIM), 1)
    minus_one = jnp.where(lane < 3, -1.0, 0.0).astype(BF16)
    q_aug = jnp.concatenate([q, minus_one], axis=1)
    m_ref[...] = jnp.full(m_ref.shape, MASK_VALUE, F32)
    acc_ref[...] = jnp.zeros_like(acc_ref)

    def logits(k0, size):
        k0 = pl.multiple_of(k0, tk)
        k_aug = jnp.concatenate([k_ref[0, pl.ds(k0, size), :], fb_ref[0, 0, pl.ds(k0, size), :]], axis=1)
        return lax.dot_general(k_aug, q_aug, (((1,), (1,)), ((), ())), preferred_element_type=F32)

    def absorb(st, k0, size, masked):
        k0 = pl.multiple_of(k0, tk)
        if masked:
            krow = lax.broadcasted_iota(jnp.int32, (size, tq), 0)
            qcol = lax.broadcasted_iota(jnp.int32, (size, tq), 1)
            st = jnp.where(krow <= qcol, st, MASK_VALUE)
        part = jnp.max(st.reshape(MAX_CHAINS, size // MAX_CHAINS, tq), axis=1)
        m_old = m_ref[...]
        m_new = jnp.maximum(m_old, jnp.max(part, axis=0, keepdims=True))
        m_ref[...] = m_new
        pt = jnp.exp2((st - m_new).astype(BF16))
        acc_ref[...] = jnp.exp2(m_old - m_new) * acc_ref[...] + jnp.dot(
            vt_ref[:, pl.ds(k0, size)], pt, preferred_element_type=F32)

    def absorb_one_pass(st, k0, size):
        k0 = pl.multiple_of(k0, tk)
        m_old = m_ref[...]
        part = jnp.max(st.reshape(MAX_CHAINS, size // MAX_CHAINS, tq), axis=1)
        cmax = jnp.max(part, axis=0, keepdims=True)
        pt = jnp.exp2(st - m_old).astype(BF16)
        pv = jnp.dot(vt_ref[:, pl.ds(k0, size)], pt, preferred_element_type=F32)
        safe = jnp.max(cmax - m_old) <= STALE_MAX_GUARD

        @pl.when(safe)
        def _():
            m_new = jnp.maximum(m_old, cmax)
            m_ref[...] = m_new
            acc_ref[...] = (acc_ref[...] + pv) * jnp.exp2(m_old - m_new)

        @pl.when(jnp.logical_not(safe))
        def _():
            absorb(logits(k0, size), k0, size, False)

    def pair(k0, size_a, size_b):
        st_a = logits(k0, size_a)
        st_b = logits(k0 + size_a, size_b)
        absorb_one_pass(st_a, k0, size_a)
        absorb_one_pass(st_b, k0 + size_a, size_b)

    absorb(logits(qi * tk, tk), qi * tk, tk, True)

    def body(kc, carry):
        half = BIG_CHUNK * tk // 2
        pair(kc * (BIG_CHUNK * tk), half, half)
        return carry

    n_big = qi // BIG_CHUNK
    lax.fori_loop(0, n_big, body, 0)
    rest_k0 = n_big * (BIG_CHUNK * tk)
    for visible in range(1, BIG_CHUNK):
        @pl.when(qi - n_big * BIG_CHUNK == visible)
        def _(visible=visible):
            absorb_one_pass(logits(rest_k0, visible * tk), rest_k0, visible * tk)

    o_t = acc_ref[:HEAD_DIM, :] * (1.0 / acc_ref[HEAD_DIM:HEAD_DIM + 1, :])
    return o_t.T


def _attention(zqkv, fbias, batch, seq):
    n_heads = fbias.shape[0]
    b, s = batch, seq
    tq = min(512, s)
    tiles_per_step = 2 if s % (2 * tq) == 0 else 1
    tqs = tq * tiles_per_step
    spb = s // tqs
    assert s % tqs == 0
    vmem = 2 * (3 * s * HEAD_DIM * 2) + (HEAD_DIM + ONES_ROWS) * s * 2 + 6 * BIG_CHUNK * tq * tq * 4
    kv_spec = lambda off: pl.BlockSpec((1, s, HEAD_DIM), lambda bi, h, i: (off + h, bi, 0))
    return pl.pallas_call(
        functools.partial(_attn_kernel, tq=tq),
        grid=(b, n_heads, spb),
        in_specs=[pl.BlockSpec((1, tqs, HEAD_DIM), lambda bi, h, i: (h, bi * spb + i, 0)),
                  kv_spec(n_heads), kv_spec(2 * n_heads),
                  pl.BlockSpec((1, 1, s, HEAD_DIM), lambda bi, h, i: (h, bi, 0, 0))],
        out_specs=pl.BlockSpec((1, tqs, HEAD_DIM), lambda bi, h, i: (bi, i, h)),
        out_shape=jax.ShapeDtypeStruct((b, s, n_heads * HEAD_DIM), BF16),
        scratch_shapes=[pltpu.VMEM((HEAD_DIM + ONES_ROWS, s), BF16),
                        pltpu.VMEM((HEAD_DIM + ONES_ROWS, tq), F32),
                        pltpu.VMEM((1, tq), F32)],
        compiler_params=_cparams(("parallel", "parallel", "arbitrary"), vmem),
        name="fox_attention",
    )(zqkv, zqkv, zqkv, fbias)


def _pool_kernel(u_ref, halo_ref, w_ref, sc_ref, o_ref):
    tm = u_ref.shape[1]
    i = pl.program_id(1)
    halo = jnp.where(i > 0, halo_ref[0], 0.0)
    ext = jnp.concatenate([halo, u_ref[0]], axis=0)
    pos = (i * tm + 1 + lax.broadcasted_iota(jnp.int32, (tm, 1), 0)).astype(F32)
    gd = LANES
    for g, win in enumerate(POOL_WINDOWS):
        e = ext[:, g * gd:(g + 1) * gd]
        ssum = e
        shift = 1
        while shift < win:
            ssum = ssum + pltpu.roll(ssum, shift, 0)
            shift *= 2
        mean = ssum[POOL_HALO:] * (1.0 / jnp.minimum(pos, float(win)))
        dlt = mean - e[POOL_HALO:]
        y = jnp.dot(dlt.astype(BF16), w_ref[0, g], preferred_element_type=F32)
        o_ref[0, :, g * gd:(g + 1) * gd] = (y * sc_ref[:, g * gd:(g + 1) * gd]).astype(o_ref.dtype)


def _pool_mixer(z32, pool_w_stack, layer, scale_row, width):
    b, s, _ = z32.shape
    tm = min(512, s)
    hb = tm // POOL_HALO
    return pl.pallas_call(
        _pool_kernel,
        grid=(b, s // tm),
        in_specs=[pl.BlockSpec((1, tm, width), lambda bi, i: (bi, i, 0)),
                  pl.BlockSpec((1, POOL_HALO, width), lambda bi, i: (bi, jnp.maximum(i * hb - 1, 0), 0)),
                  pl.BlockSpec((1, N_POOL_GROUPS, LANES, LANES), lambda bi, i: (layer, 0, 0, 0)),
                  pl.BlockSpec((1, width), lambda bi, i: (0, 0))],
        out_specs=pl.BlockSpec((1, tm, width), lambda bi, i: (bi, i, 0)),
        out_shape=jax.ShapeDtypeStruct((b, s, width), BF16),
        compiler_params=_cparams(("parallel", "parallel"), 16 * tm * width * 4),
        name="pool_mixer",
    )(z32, z32, pool_w_stack, scale_row)


def _lru_kernel(x_ref, halo_ref, y_ref, cw_ref, cb_ref, wa_ref, ba_ref, wi_ref, bi_ref, lam_ref,
                o_ref, h_ref, a_s, b_s, h_s):
    tm = x_ref.shape[1]
    i = pl.program_id(1)

    @pl.when(i == 0)
    def _():
        h_ref[...] = jnp.zeros_like(h_ref)

    halo = jnp.where(i > 0, halo_ref[0], 0.0)
    ext = jnp.concatenate([halo, x_ref[0]], axis=0)
    xc = cb_ref[...] + cw_ref[LRU_CONV_WIDTH - 1:LRU_CONV_WIDTH, :] * ext[SUBLANES:]
    for k in range(LRU_CONV_WIDTH - 1):
        shifted = pltpu.roll(ext, LRU_CONV_WIDTH - 1 - k, 0)[SUBLANES:]
        xc = xc + cw_ref[k:k + 1, :] * shifted

    lam = lam_ref[...]
    neg_softplus = -(jnp.maximum(-lam, 0.0) + jnp.log(1.0 + jnp.exp(-jnp.abs(lam))))
    gd = LANES
    n_groups = tm // SUBLANES
    sub_row = lax.broadcasted_iota(jnp.int32, (n_groups, SUBLANES, gd), 1)
    for blk in range(N_LRU_BLOCKS):
        sl = slice(blk * gd, (blk + 1) * gd)
        xb = xc[:, sl]
        xb16 = xb.astype(BF16)
        gate_r = _sigmoid(jnp.dot(xb16, wa_ref[0, blk], preferred_element_type=F32) + ba_ref[:, sl])
        gate_i = _sigmoid(jnp.dot(xb16, wi_ref[0, blk], preferred_element_type=F32) + bi_ref[:, sl])
        log_a = LRU_C * gate_r * neg_softplus[:, sl]
        a = jnp.exp(log_a)
        b = jnp.sqrt(1.0 - a * a) * (gate_i * xb)
        a = a.reshape(n_groups, SUBLANES, gd)
        b = b.reshape(n_groups, SUBLANES, gd)
        dist = 1
        while dist < SUBLANES:
            has_prev = sub_row >= dist
            b = b + a * jnp.where(has_prev, pltpu.roll(b, dist, 1), 0.0)
            a = a * jnp.where(has_prev, pltpu.roll(a, dist, 1), 1.0)
            dist *= 2
        a_s[blk] = a
        b_s[blk] = b

    def group(gi, hs):
        out = []
        for blk in range(N_LRU_BLOCKS):
            rows = b_s[blk, gi] + a_s[blk, gi] * hs[blk]
            h_s[blk, gi] = rows
            out.append(rows[SUBLANES - 1:SUBLANES, :])
        return tuple(out)

    hs = lax.fori_loop(0, n_groups, group,
                       tuple(h_ref[:, blk * gd:(blk + 1) * gd] for blk in range(N_LRU_BLOCKS)), unroll=8)

    for blk in range(N_LRU_BLOCKS):
        sl = slice(blk * gd, (blk + 1) * gd)
        h_ref[:, sl] = hs[blk]
        y = y_ref[0, :, sl]
        gelu = 0.5 * y * (1.0 + jnp.tanh(0.7978845608028654 * (y + 0.044715 * (y * y * y))))
        o_ref[0, :, sl] = (h_s[blk].reshape(tm, gd) * gelu).astype(o_ref.dtype)


def _lru_mixer(z32, layer, p, width, x_col, y_col):
    b, s, _ = z32.shape
    tm = min(512, s)
    hb = tm // SUBLANES
    row = lambda bi, i: (0, 0)
    return pl.pallas_call(
        _lru_kernel,
        grid=(b, s // tm),
        in_specs=[pl.BlockSpec((1, tm, width), lambda bi, i: (bi, i, x_col)),
                  pl.BlockSpec((1, SUBLANES, width), lambda bi, i: (bi, jnp.maximum(i * hb - 1, 0), x_col)),
                  pl.BlockSpec((1, tm, width), lambda bi, i: (bi, i, y_col)),
                  pl.BlockSpec((LRU_CONV_WIDTH, width), row),
                  pl.BlockSpec((1, width), row),
                  pl.BlockSpec((1, N_LRU_BLOCKS, LANES, LANES), lambda bi, i: (layer, 0, 0, 0)),
                  pl.BlockSpec((1, width), row),
                  pl.BlockSpec((1, N_LRU_BLOCKS, LANES, LANES), lambda bi, i: (layer, 0, 0, 0)),
                  pl.BlockSpec((1, width), row),
                  pl.BlockSpec((1, width), row)],
        out_specs=pl.BlockSpec((1, tm, width), lambda bi, i: (bi, i, 0)),
        out_shape=jax.ShapeDtypeStruct((b, s, width), BF16),
        scratch_shapes=[pltpu.VMEM((1, width), F32)]
        + [pltpu.VMEM((N_LRU_BLOCKS, tm // SUBLANES, SUBLANES, LANES), F32)] * 3,
        compiler_params=_cparams(("arbitrary", "arbitrary"), 24 * tm * width * 4),
        name="rg_lru",
    )(z32, z32, z32, p["conv_w"], p["conv_b"], p["wa"], p["ba"], p["wi"], p["bi"], p["lam"])


COL_BLOCK = 512


def _col_blocks(d):
    return [slice(c, c + COL_BLOCK) for c in range(0, d, COL_BLOCK)]


def _residual_cols(acc, cols, x_ref, gt_ref, xn_refs):
    xn = x_ref[:, cols] + gt_ref[0, :, cols] * acc
    for ref in xn_refs:
        ref[:, cols] = xn
    return jnp.sum(xn * xn, axis=-1, keepdims=True)


def _norm_rows(sumsq, xn_ref, g_ref, sc_ref, sh_ref, h_ref):
    rs = lax.rsqrt(sumsq * (1.0 / xn_ref.shape[1]) + EPS)
    gain = g_ref[...] * (1.0 + sc_ref[0])
    h_ref[...] = (xn_ref[...] * rs * gain + sh_ref[0]).astype(h_ref.dtype)


def _mix_out_kernel(yp_ref, ya_ref, yl_ref, w_ref, x_ref, gt_ref, g_ref, sc_ref, sh_ref, xo_ref, h_ref):
    kp = yp_ref.shape[1]
    ka = ya_ref.shape[1]
    sumsq = jnp.zeros((x_ref.shape[0], 1), F32)
    for cols in _col_blocks(x_ref.shape[1]):
        acc = jnp.dot(yp_ref[...], w_ref[0, 0:kp, cols], preferred_element_type=F32)
        acc = acc + jnp.dot(ya_ref[...], w_ref[0, kp:kp + ka, cols], preferred_element_type=F32)
        acc = acc + jnp.dot(yl_ref[...], w_ref[0, kp + ka:, cols], preferred_element_type=F32)
        sumsq = sumsq + _residual_cols(acc, cols, x_ref, gt_ref, (xo_ref,))
    _norm_rows(sumsq, xo_ref, g_ref, sc_ref, sh_ref, h_ref)


def _mix_out(yp, ya, yl, w_stack, layer, x2, gt, g, sc, sh, seq):
    t, d = x2.shape
    tm = min(512, seq)
    tpb = seq // tm
    per_b = lambda i: (i // tpb, 0, 0)
    lhs = lambda y: pl.BlockSpec((tm, y.shape[1]), lambda i: (i, 0))
    vmem = 2 * d * d * 2 + 2 * (tm * d * 2 + 3 * tm * d * 4) + 3 * tm * d * 4
    return pl.pallas_call(
        _mix_out_kernel,
        grid=(t // tm,),
        in_specs=[lhs(yp), lhs(ya), lhs(yl),
                  pl.BlockSpec((1, d, d), lambda i: (layer, 0, 0)),
                  pl.BlockSpec((tm, d), lambda i: (i, 0)),
                  pl.BlockSpec((1, 1, d), per_b),
                  pl.BlockSpec((1, d), lambda i: (0, 0)),
                  pl.BlockSpec((1, 1, d), per_b),
                  pl.BlockSpec((1, 1, d), per_b)],
        out_specs=[pl.BlockSpec((tm, d), lambda i: (i, 0)), pl.BlockSpec((tm, d), lambda i: (i, 0))],
        out_shape=[jax.ShapeDtypeStruct((t, d), F32), jax.ShapeDtypeStruct((t, d), BF16)],
        compiler_params=_cparams(("parallel",), vmem),
        name="mix_out_proj",
    )(yp, ya, yl, w_stack, x2, gt, g, sc, sh)


def _ffn_down_kernel(a_ref, w_ref, x_ref, gt_ref, g_ref, sc_ref, sh_ref, *out_refs):
    xn_ref, h_ref = out_refs[0], out_refs[-1]
    sumsq = jnp.zeros((x_ref.shape[0], 1), F32)
    for cols in _col_blocks(x_ref.shape[1]):
        acc = jnp.dot(a_ref[...], w_ref[0, :, cols], preferred_element_type=F32)
        sumsq = sumsq + _residual_cols(acc, cols, x_ref, gt_ref, (xn_ref,))
    _norm_rows(sumsq, xn_ref, g_ref, sc_ref, sh_ref, h_ref)


def _ffn_down(act, w_stack, layer, x2, gt, g, sc, sh, seq, emit_x, h_dtype):
    t, d = x2.shape
    kf = act.shape[1]
    tm = min(512, seq)
    tpb = seq // tm
    per_b = lambda i: (i // tpb, 0, 0)
    row_tile = pl.BlockSpec((tm, d), lambda i: (i, 0))
    out_specs = [row_tile]
    out_shape = [jax.ShapeDtypeStruct((t, d), h_dtype)]
    if emit_x:
        out_specs = [row_tile, row_tile]
        out_shape = [jax.ShapeDtypeStruct((t, d), F32)] + out_shape
    else:
        assert h_dtype == F32
    vmem = kf * d * 2 + 2 * (tm * kf * 2 + 3 * tm * d * 4) + 2 * tm * COL_BLOCK * 4
    return pl.pallas_call(
        _ffn_down_kernel,
        grid=(t // tm,),
        in_specs=[pl.BlockSpec((tm, kf), lambda i: (i, 0)),
                  pl.BlockSpec((1, kf, d), lambda i: (layer, 0, 0), pipeline_mode=pl.Buffered(1)),
                  row_tile,
                  pl.BlockSpec((1, 1, d), per_b),
                  pl.BlockSpec((1, d), lambda i: (0, 0)),
                  pl.BlockSpec((1, 1, d), per_b),
                  pl.BlockSpec((1, 1, d), per_b)],
        out_specs=out_specs,
        out_shape=out_shape,
        compiler_params=_cparams(("parallel",), vmem),
        name="ffn_down_proj",
    )(act, w_stack, x2, gt, g, sc, sh)


def _ffn_gate_kernel(h_ref, wg_ref, wu_ref, cw_ref, cb_ref, o_ref, carry_ref, *, tiles_per_batch):
    tm = h_ref.shape[0]
    i = pl.program_id(0)
    j = pl.program_id(1)

    @pl.when(i % tiles_per_batch == 0)
    def _():
        carry_ref[j] = jnp.zeros(carry_ref.shape[1:], F32)

    h = h_ref[...]
    u = jnp.dot(h, wg_ref[0].astype(BF16), preferred_element_type=F32)
    ext = jnp.concatenate([carry_ref[j], u], axis=0)
    carry_ref[j] = u[tm - SUBLANES:, :]
    g = cb_ref[...] + cw_ref[FFN_CONV_WIDTH - 1:FFN_CONV_WIDTH, :] * u
    for k in range(FFN_CONV_WIDTH - 1):
        shifted = pltpu.roll(ext, FFN_CONV_WIDTH - 1 - k, 0)[SUBLANES:]
        g = g + cw_ref[k:k + 1, :] * shifted
    up = jnp.dot(h, wu_ref[0].astype(BF16), preferred_element_type=F32)
    o_ref[...] = (g * _sigmoid(g) * up).astype(o_ref.dtype)


def _ffn_gate(h, wg_stack, wu_stack, layer, conv_w, conv_b, seq):
    t, d = h.shape
    kf = wg_stack.shape[2]
    tm = min(1024, seq)
    tf = 512
    wspec = pl.BlockSpec((1, d, tf), lambda i, j: (layer, 0, j))
    vmem = 2 * (tm * d * 2 + 2 * d * tf * 4 + tm * tf * 2) + 2 * d * tf * 2 + 8 * tm * tf * 4
    return pl.pallas_call(
        functools.partial(_ffn_gate_kernel, tiles_per_batch=seq // tm),
        grid=(t // tm, kf // tf),
        in_specs=[pl.BlockSpec((tm, d), lambda i, j: (i, 0)), wspec, wspec,
                  pl.BlockSpec((FFN_CONV_WIDTH, tf), lambda i, j: (0, j)),
                  pl.BlockSpec((1, tf), lambda i, j: (0, j))],
        out_specs=pl.BlockSpec((tm, tf), lambda i, j: (i, j)),
        out_shape=jax.ShapeDtypeStruct((t, kf), BF16),
        scratch_shapes=[pltpu.VMEM((kf // tf, SUBLANES, tf), F32)],
        compiler_params=_cparams(("arbitrary", "arbitrary"), vmem),
        name="ffn_gate_up",
    )(h, wg_stack, wu_stack, conv_w, conv_b)


def kernel(x, c, w_ada, b_ada, g_mix, w_in, b_f, pool_w, pool_scale, lru_conv_w, lru_conv_b, lru_wa, lru_ba,
           lru_wi, lru_bi, lru_lambda, w_out, g_ffn, w_ffn_gate, w_ffn_up, ffn_conv_w, ffn_conv_b, w_ffn_down,
           final_g):
    batch, seq, d = x.shape
    depth = w_ada.shape[0]
    pool_width = pool_w.shape[1] * pool_w.shape[2]
    lru_width = lru_lambda.shape[1]
    n_heads = b_f.shape[1]
    attn_width = n_heads * HEAD_DIM
    assert w_in.shape[2] == pool_width + 3 * attn_width + n_heads + 2 * lru_width
    assert pool_width == lru_width == N_POOL_GROUPS * LANES and seq % SUBLANES == 0

    o_q = pool_width
    o_f = o_q + 3 * attn_width
    o_x = o_f + n_heads
    w_qkv = w_in[:, :, o_q:o_f].astype(BF16)
    w_f = jnp.pad(w_in[:, :, o_f:o_x], ((0, 0), (0, 0), (0, LANES - n_heads)))
    w_pxyf = jnp.concatenate([w_in[:, :, :o_q], w_in[:, :, o_x:], w_f], axis=2).astype(BF16)
    f_col = (pool_width + 2 * lru_width) // LANES
    bf_pad = jnp.pad(b_f, ((0, 0), (0, LANES - n_heads)))
    qkv_scale = jnp.concatenate([jnp.full((1, attn_width), HEAD_DIM ** -0.5 * LOG2E, F32),
                                 jnp.ones((1, 2 * attn_width), F32)], axis=1)
    pool_w16 = pool_w.astype(BF16)
    lru_wa16 = lru_wa.astype(BF16)
    lru_wi16 = lru_wi.astype(BF16)
    w_out16 = w_out.astype(BF16)
    w_down16 = w_ffn_down.astype(BF16)

    mod = _ada_mod(c, w_ada, b_ada)[:, :batch]

    def mod_chunk(layer, idx):
        return mod[layer, :, idx * d:(idx + 1) * d].reshape(batch, 1, d)

    x2 = x.reshape(batch * seq, d)
    h = _norm(x2, g_mix[0][None], mod_chunk(0, 1), mod_chunk(0, 0), seq)
    out = None
    for layer in range(depth):
        sh1, sc1, gt1, sh2, sc2, gt2 = (mod_chunk(layer, idx) for idx in range(6))
        zqkv = _matmul_heads(h, w_qkv, layer, 3 * attn_width // 2, qkv_scale)
        z32 = _matmul(h, w_pxyf, layer, F32, w_pxyf.shape[2]).reshape(batch, seq, w_pxyf.shape[2])

        fbias = _forget_bias(z32, f_col, bf_pad[layer][None], n_heads)
        y_attn = _attention(zqkv, fbias, batch, seq).reshape(batch * seq, attn_width)
        y_pool = _pool_mixer(z32, pool_w16, layer, pool_scale[layer][None], pool_width)
        lru_p = dict(conv_w=lru_conv_w[layer], conv_b=lru_conv_b[layer][None], wa=lru_wa16,
                     ba=lru_ba[layer][None], wi=lru_wi16, bi=lru_bi[layer][None], lam=lru_lambda[layer][None])
        y_lru = _lru_mixer(z32, layer, lru_p, lru_width, 1, 2)

        x2, h = _mix_out(y_pool.reshape(batch * seq, pool_width), y_attn, y_lru.reshape(batch * seq, lru_width),
                         w_out16, layer, x2, gt1, g_ffn[layer][None], sc2, sh2, seq)
        act = _ffn_gate(h, w_ffn_gate, w_ffn_up, layer, ffn_conv_w[layer], ffn_conv_b[layer][None], seq)
        if layer + 1 < depth:
            x2, h = _ffn_down(act, w_down16, layer, x2, gt2, g_mix[layer + 1][None], mod_chunk(layer + 1, 1),
                              mod_chunk(layer + 1, 0), seq, True, BF16)
        else:
            zeros = jnp.zeros((batch, 1, d), F32)
            (out,) = _ffn_down(act, w_down16, layer, x2, gt2, final_g[None], zeros, zeros, seq, False, F32)
    return out.reshape(batch, seq, d)
```

```python
import functools

import jax
import jax.numpy as jnp
from jax import lax
from jax.experimental import pallas as pl
from jax.experimental.pallas import tpu as pltpu

F32 = jnp.float32
BF16 = jnp.bfloat16

HEAD_DIM = 128
N_POOL_GROUPS = 4
POOL_WINDOWS = (2, 4, 8, 16)
POOL_HALO = 16
N_LRU_BLOCKS = 4
LRU_CONV_WIDTH = 4
LRU_C = 8.0
FFN_CONV_WIDTH = 3
EPS = 1e-6

LANES = 128
SUBLANES = 8
VMEM_LIMIT_CAP_V7X = 58 * 2**20
MASK_VALUE = -1e30
LOG2E = 1.4426950408889634


def _cparams(semantics, vmem_bytes):
    return pltpu.CompilerParams(dimension_semantics=semantics,
                                vmem_limit_bytes=int(min(VMEM_LIMIT_CAP_V7X, max(vmem_bytes, 16 * 2**20))))


def _sigmoid(x):
    return 0.5 * jnp.tanh(0.5 * x) + 0.5


def _norm_mod(x, g, sc, sh):
    ms = jnp.mean(x * x, axis=-1, keepdims=True)
    return (x * lax.rsqrt(ms + EPS) * g) * (1.0 + sc) + sh


def _split3_bf16(x):
    hi = x.astype(BF16)
    r1 = x - hi.astype(F32)
    mid = r1.astype(BF16)
    lo = (r1 - mid.astype(F32)).astype(BF16)
    return hi, mid, lo


def _ada_kernel(c_ref, w_ref, b_ref, o_ref, acc_ref):
    nb = c_ref.shape[0]
    n = w_ref.shape[3]
    k = pl.program_id(1)

    @pl.when(k == 0)
    def _():
        acc_ref[...] = jnp.zeros_like(acc_ref)

    ca = [c_ref[b] * _sigmoid(c_ref[b]) for b in range(nb)]

    def lane_chunk(jc, carry):
        sl = pl.ds(pl.multiple_of(jc * LANES, LANES), LANES)
        w = w_ref[0, :, :, sl]
        for b in range(nb):
            acc_ref[b, :, sl] += jnp.sum(w * ca[b], axis=0)
        return carry

    lax.fori_loop(0, n // LANES, lane_chunk, 0, unroll=4)

    @pl.when(k == pl.num_programs(1) - 1)
    def _():
        o_ref[...] = jnp.zeros_like(o_ref)
        for b in range(nb):
            o_ref[0, b:b + 1, :] = jnp.sum(acc_ref[b], axis=0, keepdims=True) + b_ref[0]


ADA_SLAB_ROWS = 256


def _ada_mod(c, w_ada, b_ada):
    depth, d, n = w_ada.shape
    batch = c.shape[0]
    dr = d // SUBLANES
    sr = ADA_SLAB_ROWS // SUBLANES
    c_lanes = jnp.broadcast_to(c[:, :, None], (batch, d, LANES)).reshape(batch, dr, SUBLANES, LANES)
    return pl.pallas_call(
        _ada_kernel,
        grid=(depth, dr // sr),
        in_specs=[pl.BlockSpec((batch, sr, SUBLANES, LANES), lambda l, k: (0, k, 0, 0)),
                  pl.BlockSpec((1, sr, SUBLANES, n), lambda l, k: (l, k, 0, 0)),
                  pl.BlockSpec((1, 1, n), lambda l, k: (l, 0, 0))],
        out_specs=pl.BlockSpec((1, SUBLANES, n), lambda l, k: (l, 0, 0)),
        out_shape=jax.ShapeDtypeStruct((depth, SUBLANES, n), F32),
        scratch_shapes=[pltpu.VMEM((batch, SUBLANES, n), F32)],
        compiler_params=_cparams(("parallel", "arbitrary"), 3 * ADA_SLAB_ROWS * n * 4),
        name="ada_mod",
    )(c_lanes, w_ada.reshape(depth, dr, SUBLANES, n), b_ada.reshape(depth, 1, n))


def _norm_kernel(x_ref, g_ref, sc_ref, sh_ref, h_ref):
    h_ref[...] = _norm_mod(x_ref[...], g_ref[...], sc_ref[0], sh_ref[0]).astype(h_ref.dtype)


def _norm(x2, g, sc, sh, seq):
    t, d = x2.shape
    tm = min(512, seq)
    tpb = seq // tm
    return pl.pallas_call(
        _norm_kernel,
        grid=(t // tm,),
        in_specs=[pl.BlockSpec((tm, d), lambda i: (i, 0)),
                  pl.BlockSpec((1, d), lambda i: (0, 0)),
                  pl.BlockSpec((1, 1, d), lambda i: (i // tpb, 0, 0)),
                  pl.BlockSpec((1, 1, d), lambda i: (i // tpb, 0, 0))],
        out_specs=pl.BlockSpec((tm, d), lambda i: (i, 0)),
        out_shape=jax.ShapeDtypeStruct((t, d), BF16),
        compiler_params=_cparams(("parallel",), 6 * tm * d * 4),
        name="norm_mod",
    )(x2, g, sc, sh)


def _mm_kernel(a_ref, w_ref, o_ref):
    o_ref[...] = jnp.dot(a_ref[...], w_ref[0], preferred_element_type=F32).astype(o_ref.dtype)


def _mm_heads_kernel(a_ref, w_ref, cs_ref, o_ref):
    acc = jnp.dot(a_ref[...], w_ref[0], preferred_element_type=F32) * cs_ref[...]
    for hh in range(o_ref.shape[0]):
        o_ref[hh] = acc[:, hh * HEAD_DIM:(hh + 1) * HEAD_DIM].astype(o_ref.dtype)


def _matmul(a, w_stack, layer, out_dtype, tn):
    t, k = a.shape
    n = w_stack.shape[2]
    tm = min(1024, t)
    vmem = 2 * (tm * k * 2 + k * tn * 2 + tm * tn * 4) + 2 * tm * tn * 4
    return pl.pallas_call(
        _mm_kernel,
        grid=(t // tm, n // tn),
        in_specs=[pl.BlockSpec((tm, k), lambda i, j: (i, 0)),
                  pl.BlockSpec((1, k, tn), lambda i, j: (layer, 0, j))],
        out_specs=pl.BlockSpec((tm, tn), lambda i, j: (i, j)),
        out_shape=jax.ShapeDtypeStruct((t, n), out_dtype),
        compiler_params=_cparams(("parallel", "parallel"), vmem),
        name="matmul_dmodel",
    )(a, w_stack)


def _matmul_heads(a, w_stack, layer, tn, col_scale):
    t, k = a.shape
    n = w_stack.shape[2]
    tm = min(1024, t)
    hpb = tn // HEAD_DIM
    vmem = 2 * (tm * k * 2 + k * tn * 2 + tm * tn * 2) + 2 * tm * tn * 4
    return pl.pallas_call(
        _mm_heads_kernel,
        grid=(t // tm, n // tn),
        in_specs=[pl.BlockSpec((tm, k), lambda i, j: (i, 0)),
                  pl.BlockSpec((1, k, tn), lambda i, j: (layer, 0, j)),
                  pl.BlockSpec((1, tn), lambda i, j: (0, j))],
        out_specs=pl.BlockSpec((hpb, tm, HEAD_DIM), lambda i, j: (j, i, 0)),
        out_shape=jax.ShapeDtypeStruct((n // HEAD_DIM, t, HEAD_DIM), BF16),
        compiler_params=_cparams(("parallel", "parallel"), vmem),
        name="matmul_heads",
    )(a, w_stack, col_scale)


N_SPLIT = 3


def _fcum_kernel(z_ref, b_ref, o_ref, carry_ref, tri_ref, place_ref):
    ts = z_ref.shape[1]
    w = z_ref.shape[2]
    wo = place_ref.shape[1]

    @pl.when((pl.program_id(0) == 0) & (pl.program_id(1) == 0))
    def _():
        r = lax.broadcasted_iota(jnp.int32, (ts, ts), 0)
        c = lax.broadcasted_iota(jnp.int32, (ts, ts), 1)
        tri_ref[...] = jnp.where(c <= r, 1.0, 0.0).astype(BF16)
        row = lax.broadcasted_iota(jnp.int32, (N_SPLIT * w, wo), 0)
        lane = lax.broadcasted_iota(jnp.int32, (N_SPLIT * w, wo), 1)
        target = jnp.zeros_like(row)
        for idx in range(N_SPLIT):
            in_part = (row >= idx * w) & (row < (idx + 1) * w)
            target = jnp.where(in_part, (row - idx * w) * HEAD_DIM + idx, target)
        place_ref[...] = jnp.where(lane == target, 1.0, 0.0).astype(BF16)

    @pl.when(pl.program_id(1) == 0)
    def _():
        carry_ref[...] = jnp.zeros_like(carry_ref)

    x = z_ref[0] + b_ref[...]
    lf = jnp.minimum(x, 0.0) - jnp.log(1.0 + jnp.exp(-jnp.abs(x)))
    parts = jnp.dot(tri_ref[...], jnp.concatenate(_split3_bf16(lf), axis=1), preferred_element_type=F32)
    cs = carry_ref[...] + parts[:, :w]
    for idx in range(1, N_SPLIT):
        cs = cs + parts[:, idx * w:(idx + 1) * w]
    carry_ref[...] = cs[ts - 1:ts, :]

    split = jnp.concatenate(_split3_bf16(cs * LOG2E), axis=1)
    bias = jnp.dot(split, place_ref[...], preferred_element_type=F32)
    for hh in range(o_ref.shape[0]):
        o_ref[hh, 0] = bias[:, hh * HEAD_DIM:(hh + 1) * HEAD_DIM].astype(o_ref.dtype)


def _forget_bias(z32, f_col, bf_pad, n_heads):
    b, s, _ = z32.shape
    w = LANES
    ts = min(512, s)
    wo = n_heads * HEAD_DIM
    return pl.pallas_call(
        _fcum_kernel,
        grid=(b, s // ts),
        in_specs=[pl.BlockSpec((1, ts, w), lambda bi, i: (bi, i, f_col)),
                  pl.BlockSpec((1, w), lambda bi, i: (0, 0))],
        out_specs=pl.BlockSpec((n_heads, 1, ts, HEAD_DIM), lambda bi, i: (0, bi, i, 0)),
        out_shape=jax.ShapeDtypeStruct((n_heads, b, s, HEAD_DIM), BF16),
        scratch_shapes=[pltpu.VMEM((1, w), F32), pltpu.VMEM((ts, ts), BF16), pltpu.VMEM((N_SPLIT * w, wo), BF16)],
        compiler_params=_cparams(("arbitrary", "arbitrary"), 4 * ts * ts * 4 + 8 * ts * wo * 4),
        name="forget_bias",
    )(z32, bf_pad)


ONES_ROWS = 16


MAX_CHAINS = 8
BIG_CHUNK = 4
STALE_MAX_GUARD = 60.0


def _attn_kernel(q_ref, k_ref, v_ref, fb_ref, o_ref, vt_ref, acc_ref, m_ref, *, tq):
    tk = tq
    s_len = k_ref.shape[1]
    step = pl.program_id(2)

    @pl.when(step == 0)
    def _():
        vt_ref[HEAD_DIM:, :] = jnp.ones((ONES_ROWS, s_len), BF16)
        for c in range(s_len // tk):
            vt_ref[:HEAD_DIM, c * tk:(c + 1) * tk] = v_ref[0, c * tk:(c + 1) * tk, :].astype(F32).T.astype(BF16)

    for sub in range(q_ref.shape[1] // tq):
        rows = slice(sub * tq, (sub + 1) * tq)
        o_ref[0, rows, :] = _attn_query_tile(step * (q_ref.shape[1] // tq) + sub, q_ref[0, rows, :], k_ref, fb_ref,
                                             vt_ref, acc_ref, m_ref).astype(o_ref.dtype)


def _attn_query_tile(qi, q, k_ref, fb_ref, vt_ref, acc_ref, m_ref):
    tq = q.shape[0]
    tk = tq
    lane = lax.broadcasted_iota(jnp.int32, (tq, HEAD_DIM), 1)
    minus_one = jnp.where(lane < 3, -1.0, 0.0).astype(BF16)
    q_aug = jnp.concatenate([q, minus_one], axis=1)
    m_ref[...] = jnp.full(m_ref.shape, MASK_VALUE, F32)
    acc_ref[...] = jnp.zeros_like(acc_ref)

    def logits(k0, size):
        k0 = pl.multiple_of(k0, tk)
        k_aug = jnp.concatenate([k_ref[0, pl.ds(k0, size), :], fb_ref[0, 0, pl.ds(k0, size), :]], axis=1)
        return lax.dot_general(k_aug, q_aug, (((1,), (1,)), ((), ())), preferred_element_type=F32)

    def absorb(st, k0, size, masked):
        k0 = pl.multiple_of(k0, tk)
        if masked:
            krow = lax.broadcasted_iota(jnp.int32, (size, tq), 0)
            qcol = lax.broadcasted_iota(jnp.int32, (size, tq), 1)
            st = jnp.where(krow <= qcol, st, MASK_VALUE)
        part = jnp.max(st.reshape(MAX_CHAINS, size // MAX_CHAINS, tq), axis=1)
        m_old = m_ref[...]
        m_new = jnp.maximum(m_old, jnp.max(part, axis=0, keepdims=True))
        m_ref[...] = m_new
        pt = jnp.exp2((st - m_new).astype(BF16))
        acc_ref[...] = jnp.exp2(m_old - m_new) * acc_ref[...] + jnp.dot(
            vt_ref[:, pl.ds(k0, size)], pt, preferred_element_type=F32)

    def absorb_one_pass(st, k0, size):
        k0 = pl.multiple_of(k0, tk)
        m_old = m_ref[...]
        part = jnp.max(st.reshape(MAX_CHAINS, size // MAX_CHAINS, tq), axis=1)
        cmax = jnp.max(part, axis=0, keepdims=True)
        pt = jnp.exp2(st - m_old).astype(BF16)
        pv = jnp.dot(vt_ref[:, pl.ds(k0, size)], pt, preferred_element_type=F32)
        safe = jnp.max(cmax - m_old) <= STALE_MAX_GUARD

        @pl.when(safe)
        def _():
            m_new = jnp.maximum(m_old, cmax)
            m_ref[...] = m_new
            acc_ref[...] = (acc_ref[...] + pv) * jnp.exp2(m_old - m_new)

        @pl.when(jnp.logical_not(safe))
        def _():
            absorb(logits(k0, size), k0, size, False)

    n_big = qi // BIG_CHUNK
    tail_k0 = n_big * (BIG_CHUNK * tk)
    for visible in range(BIG_CHUNK):
        @pl.when(qi - n_big * BIG_CHUNK == visible)
        def _(visible=visible):
            diag_k0 = tail_k0 + visible * tk
            if visible == 0:
                absorb(logits(diag_k0, tk), diag_k0, tk, True)
            else:
                st_a = logits(tail_k0, visible * tk)
                st_b = logits(diag_k0, tk)
                absorb(st_a, tail_k0, visible * tk, False)
                absorb(st_b, diag_k0, tk, True)

    def body(kc, carry):
        k0 = kc * (BIG_CHUNK * tk)
        absorb_one_pass(logits(k0, BIG_CHUNK * tk), k0, BIG_CHUNK * tk)
        return carry

    lax.fori_loop(0, n_big, body, 0)
    o_t = acc_ref[:HEAD_DIM, :] * (1.0 / acc_ref[HEAD_DIM:HEAD_DIM + 1, :])
    return o_t.T


def _attention(zqkv, fbias, batch, seq):
    n_heads = fbias.shape[0]
    b, s = batch, seq
    tq = min(512, s)
    tiles_per_step = 2 if s % (2 * tq) == 0 else 1
    tqs = tq * tiles_per_step
    spb = s // tqs
    assert s % tqs == 0
    vmem = 2 * (3 * s * HEAD_DIM * 2) + (HEAD_DIM + ONES_ROWS) * s * 2 + 6 * BIG_CHUNK * tq * tq * 4
    kv_spec = lambda off: pl.BlockSpec((1, s, HEAD_DIM), lambda bi, h, i: (off + h, bi, 0))
    return pl.pallas_call(
        functools.partial(_attn_kernel, tq=tq),
        grid=(b, n_heads, spb),
        in_specs=[pl.BlockSpec((1, tqs, HEAD_DIM), lambda bi, h, i: (h, bi * spb + i, 0)),
                  kv_spec(n_heads), kv_spec(2 * n_heads),
                  pl.BlockSpec((1, 1, s, HEAD_DIM), lambda bi, h, i: (h, bi, 0, 0))],
        out_specs=pl.BlockSpec((1, tqs, HEAD_DIM), lambda bi, h, i: (bi, i, h)),
        out_shape=jax.ShapeDtypeStruct((b, s, n_heads * HEAD_DIM), BF16),
        scratch_shapes=[pltpu.VMEM((HEAD_DIM + ONES_ROWS, s), BF16),
                        pltpu.VMEM((HEAD_DIM + ONES_ROWS, tq), F32),
                        pltpu.VMEM((1, tq), F32)],
        compiler_params=_cparams(("parallel", "parallel", "arbitrary"), vmem),
        name="fox_attention",
    )(zqkv, zqkv, zqkv, fbias)


def _pool_kernel(u_ref, halo_ref, w_ref, sc_ref, o_ref):
    tm = u_ref.shape[1]
    i = pl.program_id(1)
    halo = jnp.where(i > 0, halo_ref[0], 0.0)
    ext = jnp.concatenate([halo, u_ref[0]], axis=0)
    pos = (i * tm + 1 + lax.broadcasted_iota(jnp.int32, (tm, 1), 0)).astype(F32)
    gd = LANES
    for g, win in enumerate(POOL_WINDOWS):
        e = ext[:, g * gd:(g + 1) * gd]
        ssum = e
        shift = 1
        while shift < win:
            ssum = ssum + pltpu.roll(ssum, shift, 0)
            shift *= 2
        mean = ssum[POOL_HALO:] * (1.0 / jnp.minimum(pos, float(win)))
        dlt = mean - e[POOL_HALO:]
        y = jnp.dot(dlt.astype(BF16), w_ref[0, g], preferred_element_type=F32)
        o_ref[0, :, g * gd:(g + 1) * gd] = (y * sc_ref[:, g * gd:(g + 1) * gd]).astype(o_ref.dtype)


def _pool_mixer(z32, pool_w_stack, layer, scale_row, width):
    b, s, _ = z32.shape
    tm = min(512, s)
    hb = tm // POOL_HALO
    return pl.pallas_call(
        _pool_kernel,
        grid=(b, s // tm),
        in_specs=[pl.BlockSpec((1, tm, width), lambda bi, i: (bi, i, 0)),
                  pl.BlockSpec((1, POOL_HALO, width), lambda bi, i: (bi, jnp.maximum(i * hb - 1, 0), 0)),
                  pl.BlockSpec((1, N_POOL_GROUPS, LANES, LANES), lambda bi, i: (layer, 0, 0, 0)),
                  pl.BlockSpec((1, width), lambda bi, i: (0, 0))],
        out_specs=pl.BlockSpec((1, tm, width), lambda bi, i: (bi, i, 0)),
        out_shape=jax.ShapeDtypeStruct((b, s, width), BF16),
        compiler_params=_cparams(("parallel", "parallel"), 16 * tm * width * 4),
        name="pool_mixer",
    )(z32, z32, pool_w_stack, scale_row)


def _lru_kernel(x_ref, halo_ref, y_ref, cw_ref, cb_ref, wa_ref, ba_ref, wi_ref, bi_ref, lam_ref,
                o_ref, h_ref, a_s, b_s, h_s):
    tm = x_ref.shape[1]
    i = pl.program_id(1)

    @pl.when(i == 0)
    def _():
        h_ref[...] = jnp.zeros_like(h_ref)

    halo = jnp.where(i > 0, halo_ref[0], 0.0)
    ext = jnp.concatenate([halo, x_ref[0]], axis=0)
    xc = cb_ref[...] + cw_ref[LRU_CONV_WIDTH - 1:LRU_CONV_WIDTH, :] * ext[SUBLANES:]
    for k in range(LRU_CONV_WIDTH - 1):
        shifted = pltpu.roll(ext, LRU_CONV_WIDTH - 1 - k, 0)[SUBLANES:]
        xc = xc + cw_ref[k:k + 1, :] * shifted

    lam = lam_ref[...]
    neg_softplus = -(jnp.maximum(-lam, 0.0) + jnp.log(1.0 + jnp.exp(-jnp.abs(lam))))
    gd = LANES
    n_groups = tm // SUBLANES
    sub_row = lax.broadcasted_iota(jnp.int32, (n_groups, SUBLANES, gd), 1)
    for blk in range(N_LRU_BLOCKS):
        sl = slice(blk * gd, (blk + 1) * gd)
        xb = xc[:, sl]
        xb16 = xb.astype(BF16)
        gate_r = _sigmoid(jnp.dot(xb16, wa_ref[0, blk], preferred_element_type=F32) + ba_ref[:, sl])
        gate_i = _sigmoid(jnp.dot(xb16, wi_ref[0, blk], preferred_element_type=F32) + bi_ref[:, sl])
        log_a = LRU_C * gate_r * neg_softplus[:, sl]
        a = jnp.exp(log_a)
        b = jnp.sqrt(1.0 - a * a) * (gate_i * xb)
        a = a.reshape(n_groups, SUBLANES, gd)
        b = b.reshape(n_groups, SUBLANES, gd)
        dist = 1
        while dist < SUBLANES:
            has_prev = sub_row >= dist
            b = b + a * jnp.where(has_prev, pltpu.roll(b, dist, 1), 0.0)
            a = a * jnp.where(has_prev, pltpu.roll(a, dist, 1), 1.0)
            dist *= 2
        a_s[blk] = a
        b_s[blk] = b

    def group(gi, hs):
        out = []
        for blk in range(N_LRU_BLOCKS):
            rows = b_s[blk, gi] + a_s[blk, gi] * hs[blk]
            h_s[blk, gi] = rows
            out.append(rows[SUBLANES - 1:SUBLANES, :])
        return tuple(out)

    hs = lax.fori_loop(0, n_groups, group,
                       tuple(h_ref[:, blk * gd:(blk + 1) * gd] for blk in range(N_LRU_BLOCKS)), unroll=8)

    for blk in range(N_LRU_BLOCKS):
        sl = slice(blk * gd, (blk + 1) * gd)
        h_ref[:, sl] = hs[blk]
        y = y_ref[0, :, sl]
        gelu = 0.5 * y * (1.0 + jnp.tanh(0.7978845608028654 * (y + 0.044715 * (y * y * y))))
        o_ref[0, :, sl] = (h_s[blk].reshape(tm, gd) * gelu).astype(o_ref.dtype)


def _lru_mixer(z32, layer, p, width, x_col, y_col):
    b, s, _ = z32.shape
    tm = min(512, s)
    hb = tm // SUBLANES
    row = lambda bi, i: (0, 0)
    return pl.pallas_call(
        _lru_kernel,
        grid=(b, s // tm),
        in_specs=[pl.BlockSpec((1, tm, width), lambda bi, i: (bi, i, x_col)),
                  pl.BlockSpec((1, SUBLANES, width), lambda bi, i: (bi, jnp.maximum(i * hb - 1, 0), x_col)),
                  pl.BlockSpec((1, tm, width), lambda bi, i: (bi, i, y_col)),
                  pl.BlockSpec((LRU_CONV_WIDTH, width), row),
                  pl.BlockSpec((1, width), row),
                  pl.BlockSpec((1, N_LRU_BLOCKS, LANES, LANES), lambda bi, i: (layer, 0, 0, 0)),
                  pl.BlockSpec((1, width), row),
                  pl.BlockSpec((1, N_LRU_BLOCKS, LANES, LANES), lambda bi, i: (layer, 0, 0, 0)),
                  pl.BlockSpec((1, width), row),
                  pl.BlockSpec((1, width), row)],
        out_specs=pl.BlockSpec((1, tm, width), lambda bi, i: (bi, i, 0)),
        out_shape=jax.ShapeDtypeStruct((b, s, width), BF16),
        scratch_shapes=[pltpu.VMEM((1, width), F32)]
        + [pltpu.VMEM((N_LRU_BLOCKS, tm // SUBLANES, SUBLANES, LANES), F32)] * 3,
        compiler_params=_cparams(("arbitrary", "arbitrary"), 24 * tm * width * 4),
        name="rg_lru",
    )(z32, z32, z32, p["conv_w"], p["conv_b"], p["wa"], p["ba"], p["wi"], p["bi"], p["lam"])


COL_BLOCK = 512


def _col_blocks(d):
    return [slice(c, c + COL_BLOCK) for c in range(0, d, COL_BLOCK)]


def _residual_cols(acc, cols, x_ref, gt_ref, xn_refs):
    xn = x_ref[:, cols] + gt_ref[0, :, cols] * acc
    for ref in xn_refs:
        ref[:, cols] = xn
    return jnp.sum(xn * xn, axis=-1, keepdims=True)


def _norm_rows(sumsq, xn_ref, g_ref, sc_ref, sh_ref, h_ref):
    rs = lax.rsqrt(sumsq * (1.0 / xn_ref.shape[1]) + EPS)
    gain = g_ref[...] * (1.0 + sc_ref[0])
    h_ref[...] = (xn_ref[...] * rs * gain + sh_ref[0]).astype(h_ref.dtype)


def _mix_out_kernel(yp_ref, ya_ref, yl_ref, w_ref, x_ref, gt_ref, g_ref, sc_ref, sh_ref, xo_ref, h_ref):
    kp = yp_ref.shape[1]
    ka = ya_ref.shape[1]
    sumsq = jnp.zeros((x_ref.shape[0], 1), F32)
    for cols in _col_blocks(x_ref.shape[1]):
        acc = jnp.dot(yp_ref[...], w_ref[0, 0:kp, cols], preferred_element_type=F32)
        acc = acc + jnp.dot(ya_ref[...], w_ref[0, kp:kp + ka, cols], preferred_element_type=F32)
        acc = acc + jnp.dot(yl_ref[...], w_ref[0, kp + ka:, cols], preferred_element_type=F32)
        sumsq = sumsq + _residual_cols(acc, cols, x_ref, gt_ref, (xo_ref,))
    _norm_rows(sumsq, xo_ref, g_ref, sc_ref, sh_ref, h_ref)


def _mix_out(yp, ya, yl, w_stack, layer, x2, gt, g, sc, sh, seq):
    t, d = x2.shape
    tm = min(512, seq)
    tpb = seq // tm
    per_b = lambda i: (i // tpb, 0, 0)
    lhs = lambda y: pl.BlockSpec((tm, y.shape[1]), lambda i: (i, 0))
    vmem = 2 * d * d * 2 + 2 * (tm * d * 2 + 3 * tm * d * 4) + 3 * tm * d * 4
    return pl.pallas_call(
        _mix_out_kernel,
        grid=(t // tm,),
        in_specs=[lhs(yp), lhs(ya), lhs(yl),
                  pl.BlockSpec((1, d, d), lambda i: (layer, 0, 0)),
                  pl.BlockSpec((tm, d), lambda i: (i, 0)),
                  pl.BlockSpec((1, 1, d), per_b),
                  pl.BlockSpec((1, d), lambda i: (0, 0)),
                  pl.BlockSpec((1, 1, d), per_b),
                  pl.BlockSpec((1, 1, d), per_b)],
        out_specs=[pl.BlockSpec((tm, d), lambda i: (i, 0)), pl.BlockSpec((tm, d), lambda i: (i, 0))],
        out_shape=[jax.ShapeDtypeStruct((t, d), F32), jax.ShapeDtypeStruct((t, d), BF16)],
        compiler_params=_cparams(("parallel",), vmem),
        name="mix_out_proj",
    )(yp, ya, yl, w_stack, x2, gt, g, sc, sh)


def _ffn_down_kernel(a_ref, w_ref, x_ref, gt_ref, g_ref, sc_ref, sh_ref, *out_refs):
    xn_ref, h_ref = out_refs[0], out_refs[-1]
    sumsq = jnp.zeros((x_ref.shape[0], 1), F32)
    for cols in _col_blocks(x_ref.shape[1]):
        acc = jnp.dot(a_ref[...], w_ref[0, :, cols], preferred_element_type=F32)
        sumsq = sumsq + _residual_cols(acc, cols, x_ref, gt_ref, (xn_ref,))
    _norm_rows(sumsq, xn_ref, g_ref, sc_ref, sh_ref, h_ref)


def _ffn_down(act, w_stack, layer, x2, gt, g, sc, sh, seq, emit_x, h_dtype):
    t, d = x2.shape
    kf = act.shape[1]
    tm = min(512, seq)
    tpb = seq // tm
    per_b = lambda i: (i // tpb, 0, 0)
    row_tile = pl.BlockSpec((tm, d), lambda i: (i, 0))
    out_specs = [row_tile]
    out_shape = [jax.ShapeDtypeStruct((t, d), h_dtype)]
    if emit_x:
        out_specs = [row_tile, row_tile]
        out_shape = [jax.ShapeDtypeStruct((t, d), F32)] + out_shape
    else:
        assert h_dtype == F32
    vmem = kf * d * 2 + 2 * (tm * kf * 2 + 3 * tm * d * 4) + 2 * tm * COL_BLOCK * 4
    return pl.pallas_call(
        _ffn_down_kernel,
        grid=(t // tm,),
        in_specs=[pl.BlockSpec((tm, kf), lambda i: (i, 0)),
                  pl.BlockSpec((1, kf, d), lambda i: (layer, 0, 0), pipeline_mode=pl.Buffered(1)),
                  row_tile,
                  pl.BlockSpec((1, 1, d), per_b),
                  pl.BlockSpec((1, d), lambda i: (0, 0)),
                  pl.BlockSpec((1, 1, d), per_b),
                  pl.BlockSpec((1, 1, d), per_b)],
        out_specs=out_specs,
        out_shape=out_shape,
        compiler_params=_cparams(("parallel",), vmem),
        name="ffn_down_proj",
    )(act, w_stack, x2, gt, g, sc, sh)


def _ffn_gate_kernel(h_ref, wg_ref, wu_ref, cw_ref, cb_ref, o_ref, carry_ref, *, tiles_per_batch):
    tm = h_ref.shape[0]
    i = pl.program_id(0)
    j = pl.program_id(1)

    @pl.when(i % tiles_per_batch == 0)
    def _():
        carry_ref[j] = jnp.zeros(carry_ref.shape[1:], F32)

    h = h_ref[...]
    u = jnp.dot(h, wg_ref[0].astype(BF16), preferred_element_type=F32)
    ext = jnp.concatenate([carry_ref[j], u], axis=0)
    carry_ref[j] = u[tm - SUBLANES:, :]
    g = cb_ref[...] + cw_ref[FFN_CONV_WIDTH - 1:FFN_CONV_WIDTH, :] * u
    for k in range(FFN_CONV_WIDTH - 1):
        shifted = pltpu.roll(ext, FFN_CONV_WIDTH - 1 - k, 0)[SUBLANES:]
        g = g + cw_ref[k:k + 1, :] * shifted
    up = jnp.dot(h, wu_ref[0].astype(BF16), preferred_element_type=F32)
    o_ref[...] = (g * _sigmoid(g) * up).astype(o_ref.dtype)


def _ffn_gate(h, wg_stack, wu_stack, layer, conv_w, conv_b, seq):
    t, d = h.shape
    kf = wg_stack.shape[2]
    tm = min(1024, seq)
    tf = 512
    wspec = pl.BlockSpec((1, d, tf), lambda i, j: (layer, 0, j))
    vmem = 2 * (tm * d * 2 + 2 * d * tf * 4 + tm * tf * 2) + 2 * d * tf * 2 + 8 * tm * tf * 4
    return pl.pallas_call(
        functools.partial(_ffn_gate_kernel, tiles_per_batch=seq // tm),
        grid=(t // tm, kf // tf),
        in_specs=[pl.BlockSpec((tm, d), lambda i, j: (i, 0)), wspec, wspec,
                  pl.BlockSpec((FFN_CONV_WIDTH, tf), lambda i, j: (0, j)),
                  pl.BlockSpec((1, tf), lambda i, j: (0, j))],
        out_specs=pl.BlockSpec((tm, tf), lambda i, j: (i, j)),
        out_shape=jax.ShapeDtypeStruct((t, kf), BF16),
        scratch_shapes=[pltpu.VMEM((kf // tf, SUBLANES, tf), F32)],
        compiler_params=_cparams(("arbitrary", "arbitrary"), vmem),
        name="ffn_gate_up",
    )(h, wg_stack, wu_stack, conv_w, conv_b)


def kernel(x, c, w_ada, b_ada, g_mix, w_in, b_f, pool_w, pool_scale, lru_conv_w, lru_conv_b, lru_wa, lru_ba,
           lru_wi, lru_bi, lru_lambda, w_out, g_ffn, w_ffn_gate, w_ffn_up, ffn_conv_w, ffn_conv_b, w_ffn_down,
           final_g):
    batch, seq, d = x.shape
    depth = w_ada.shape[0]
    pool_width = pool_w.shape[1] * pool_w.shape[2]
    lru_width = lru_lambda.shape[1]
    n_heads = b_f.shape[1]
    attn_width = n_heads * HEAD_DIM
    assert w_in.shape[2] == pool_width + 3 * attn_width + n_heads + 2 * lru_width
    assert pool_width == lru_width == N_POOL_GROUPS * LANES and seq % SUBLANES == 0

    o_q = pool_width
    o_f = o_q + 3 * attn_width
    o_x = o_f + n_heads
    w_qkv = w_in[:, :, o_q:o_f].astype(BF16)
    w_f = jnp.pad(w_in[:, :, o_f:o_x], ((0, 0), (0, 0), (0, LANES - n_heads)))
    w_pxyf = jnp.concatenate([w_in[:, :, :o_q], w_in[:, :, o_x:], w_f], axis=2).astype(BF16)
    f_col = (pool_width + 2 * lru_width) // LANES
    bf_pad = jnp.pad(b_f, ((0, 0), (0, LANES - n_heads)))
    qkv_scale = jnp.concatenate([jnp.full((1, attn_width), HEAD_DIM ** -0.5 * LOG2E, F32),
                                 jnp.ones((1, 2 * attn_width), F32)], axis=1)
    pool_w16 = pool_w.astype(BF16)
    lru_wa16 = lru_wa.astype(BF16)
    lru_wi16 = lru_wi.astype(BF16)
    w_out16 = w_out.astype(BF16)
    w_down16 = w_ffn_down.astype(BF16)

    mod = _ada_mod(c, w_ada, b_ada)[:, :batch]

    def mod_chunk(layer, idx):
        return mod[layer, :, idx * d:(idx + 1) * d].reshape(batch, 1, d)

    x2 = x.reshape(batch * seq, d)
    h = _norm(x2, g_mix[0][None], mod_chunk(0, 1), mod_chunk(0, 0), seq)
    out = None
    for layer in range(depth):
        sh1, sc1, gt1, sh2, sc2, gt2 = (mod_chunk(layer, idx) for idx in range(6))
        zqkv = _matmul_heads(h, w_qkv, layer, 3 * attn_width // 2, qkv_scale)
        z32 = _matmul(h, w_pxyf, layer, F32, w_pxyf.shape[2]).reshape(batch, seq, w_pxyf.shape[2])

        fbias = _forget_bias(z32, f_col, bf_pad[layer][None], n_heads)
        y_attn = _attention(zqkv, fbias, batch, seq).reshape(batch * seq, attn_width)
        y_pool = _pool_mixer(z32, pool_w16, layer, pool_scale[layer][None], pool_width)
        lru_p = dict(conv_w=lru_conv_w[layer], conv_b=lru_conv_b[layer][None], wa=lru_wa16,
                     ba=lru_ba[layer][None], wi=lru_wi16, bi=lru_bi[layer][None], lam=lru_lambda[layer][None])
        y_lru = _lru_mixer(z32, layer, lru_p, lru_width, 1, 2)

        x2, h = _mix_out(y_pool.reshape(batch * seq, pool_width), y_attn, y_lru.reshape(batch * seq, lru_width),
                         w_out16, layer, x2, gt1, g_ffn[layer][None], sc2, sh2, seq)
        act = _ffn_gate(h, w_ffn_gate, w_ffn_up, layer, ffn_conv_w[layer], ffn_conv_b[layer][None], seq)
        if layer + 1 < depth:
            x2, h = _ffn_down(act, w_down16, layer, x2, gt2, g_mix[layer + 1][None], mod_chunk(layer + 1, 1),
                              mod_chunk(layer + 1, 0), seq, True, BF16)
        else:
            zeros = jnp.zeros((batch, 1, d), F32)
            (out,) = _ffn_down(act, w_down16, layer, x2, gt2, final_g[None], zeros, zeros, seq, False, F32)
    return out.reshape(batch, seq, d)
```

```python
import functools

import jax
import jax.numpy as jnp
from jax import lax
from jax.experimental import pallas as pl
from jax.experimental.pallas import tpu as pltpu

F32 = jnp.float32
BF16 = jnp.bfloat16

HEAD_DIM = 128
N_POOL_GROUPS = 4
POOL_WINDOWS = (2, 4, 8, 16)
POOL_HALO = 16
N_LRU_BLOCKS = 4
LRU_CONV_WIDTH = 4
LRU_C = 8.0
FFN_CONV_WIDTH = 3
EPS = 1e-6

LANES = 128
SUBLANES = 8
VMEM_LIMIT_CAP_V7X = 58 * 2**20
MASK_VALUE = -1e30
LOG2E = 1.4426950408889634


def _cparams(semantics, vmem_bytes):
    return pltpu.CompilerParams(dimension_semantics=semantics,
                                vmem_limit_bytes=int(min(VMEM_LIMIT_CAP_V7X, max(vmem_bytes, 16 * 2**20))))


def _sigmoid(x):
    return 0.5 * jnp.tanh(0.5 * x) + 0.5


def _norm_mod(x, g, sc, sh):
    ms = jnp.mean(x * x, axis=-1, keepdims=True)
    return (x * lax.rsqrt(ms + EPS) * g) * (1.0 + sc) + sh


def _split3_bf16(x):
    hi = x.astype(BF16)
    r1 = x - hi.astype(F32)
    mid = r1.astype(BF16)
    lo = (r1 - mid.astype(F32)).astype(BF16)
    return hi, mid, lo


def _ada_kernel(c_ref, w_ref, b_ref, o_ref, acc_ref):
    nb = c_ref.shape[0]
    n = w_ref.shape[3]
    k = pl.program_id(1)

    @pl.when(k == 0)
    def _():
        acc_ref[...] = jnp.zeros_like(acc_ref)

    ca = [c_ref[b] * _sigmoid(c_ref[b]) for b in range(nb)]

    def lane_chunk(jc, carry):
        sl = pl.ds(pl.multiple_of(jc * LANES, LANES), LANES)
        w = w_ref[0, :, :, sl]
        for b in range(nb):
            acc_ref[b, :, sl] += jnp.sum(w * ca[b], axis=0)
        return carry

    lax.fori_loop(0, n // LANES, lane_chunk, 0, unroll=4)

    @pl.when(k == pl.num_programs(1) - 1)
    def _():
        o_ref[...] = jnp.zeros_like(o_ref)
        for b in range(nb):
            o_ref[0, b:b + 1, :] = jnp.sum(acc_ref[b], axis=0, keepdims=True) + b_ref[0]


ADA_SLAB_ROWS = 256


def _ada_mod(c, w_ada, b_ada):
    depth, d, n = w_ada.shape
    batch = c.shape[0]
    dr = d // SUBLANES
    sr = ADA_SLAB_ROWS // SUBLANES
    c_lanes = jnp.broadcast_to(c[:, :, None], (batch, d, LANES)).reshape(batch, dr, SUBLANES, LANES)
    return pl.pallas_call(
        _ada_kernel,
        grid=(depth, dr // sr),
        in_specs=[pl.BlockSpec((batch, sr, SUBLANES, LANES), lambda l, k: (0, k, 0, 0)),
                  pl.BlockSpec((1, sr, SUBLANES, n), lambda l, k: (l, k, 0, 0)),
                  pl.BlockSpec((1, 1, n), lambda l, k: (l, 0, 0))],
        out_specs=pl.BlockSpec((1, SUBLANES, n), lambda l, k: (l, 0, 0)),
        out_shape=jax.ShapeDtypeStruct((depth, SUBLANES, n), F32),
        scratch_shapes=[pltpu.VMEM((batch, SUBLANES, n), F32)],
        compiler_params=_cparams(("parallel", "arbitrary"), 3 * ADA_SLAB_ROWS * n * 4),
        name="ada_mod",
    )(c_lanes, w_ada.reshape(depth, dr, SUBLANES, n), b_ada.reshape(depth, 1, n))


def _norm_kernel(x_ref, g_ref, sc_ref, sh_ref, h_ref):
    h_ref[...] = _norm_mod(x_ref[...], g_ref[...], sc_ref[0], sh_ref[0]).astype(h_ref.dtype)


def _norm(x2, g, sc, sh, seq):
    t, d = x2.shape
    tm = min(512, seq)
    tpb = seq // tm
    return pl.pallas_call(
        _norm_kernel,
        grid=(t // tm,),
        in_specs=[pl.BlockSpec((tm, d), lambda i: (i, 0)),
                  pl.BlockSpec((1, d), lambda i: (0, 0)),
                  pl.BlockSpec((1, 1, d), lambda i: (i // tpb, 0, 0)),
                  pl.BlockSpec((1, 1, d), lambda i: (i // tpb, 0, 0))],
        out_specs=pl.BlockSpec((tm, d), lambda i: (i, 0)),
        out_shape=jax.ShapeDtypeStruct((t, d), BF16),
        compiler_params=_cparams(("parallel",), 6 * tm * d * 4),
        name="norm_mod",
    )(x2, g, sc, sh)


_NT_DIMS = (((1,), (1,)), ((), ()))


def _mm2_kernel(a_ref, w_ref, wn_ref, o_ref, on_ref):
    o_ref[...] = lax.dot_general(a_ref[...], w_ref[0], _NT_DIMS, preferred_element_type=F32)
    on_ref[...] = lax.dot_general(a_ref[...], wn_ref[0], _NT_DIMS, preferred_element_type=F32)


def _mm_heads_kernel(a_ref, w_ref, cs_ref, o_ref):
    acc = lax.dot_general(a_ref[...], w_ref[0], _NT_DIMS, preferred_element_type=F32) * cs_ref[...]
    for hh in range(o_ref.shape[0]):
        o_ref[hh] = acc[:, hh * HEAD_DIM:(hh + 1) * HEAD_DIM].astype(o_ref.dtype)


def _matmul_wide_narrow(a, wt_stack, layer, row0, wide, narrow):
    t, k = a.shape
    tm = min(1024, t)
    assert row0 % wide == 0 and (row0 + wide) % narrow == 0
    vmem = 2 * (tm * k * 2 + k * (wide + narrow) * 2 + tm * (wide + narrow) * 4) + 2 * tm * wide * 4
    return pl.pallas_call(
        _mm2_kernel,
        grid=(t // tm,),
        in_specs=[pl.BlockSpec((tm, k), lambda i: (i, 0)),
                  pl.BlockSpec((1, wide, k), lambda i: (layer, row0 // wide, 0)),
                  pl.BlockSpec((1, narrow, k), lambda i: (layer, (row0 + wide) // narrow, 0))],
        out_specs=[pl.BlockSpec((tm, wide), lambda i: (i, 0)), pl.BlockSpec((tm, narrow), lambda i: (i, 0))],
        out_shape=[jax.ShapeDtypeStruct((t, wide), F32), jax.ShapeDtypeStruct((t, narrow), F32)],
        compiler_params=_cparams(("parallel",), vmem),
        name="matmul_dmodel",
    )(a, wt_stack, wt_stack)


def _matmul_heads(a, wt_stack, layer, n, tn, col_scale):
    t, k = a.shape
    tm = min(1024, t)
    hpb = tn // HEAD_DIM
    vmem = 2 * (tm * k * 2 + k * tn * 2 + tm * tn * 2) + 2 * tm * tn * 4
    return pl.pallas_call(
        _mm_heads_kernel,
        grid=(t // tm, n // tn),
        in_specs=[pl.BlockSpec((tm, k), lambda i, j: (i, 0)),
                  pl.BlockSpec((1, tn, k), lambda i, j: (layer, j, 0)),
                  pl.BlockSpec((1, tn), lambda i, j: (0, j))],
        out_specs=pl.BlockSpec((hpb, tm, HEAD_DIM), lambda i, j: (j, i, 0)),
        out_shape=jax.ShapeDtypeStruct((n // HEAD_DIM, t, HEAD_DIM), BF16),
        compiler_params=_cparams(("parallel", "parallel"), vmem),
        name="matmul_heads",
    )(a, wt_stack, col_scale)


N_SPLIT = 3


def _fcum_kernel(z_ref, b_ref, o_ref, carry_ref, tri_ref, place_ref):
    ts = z_ref.shape[1]
    w = z_ref.shape[2]
    wo = place_ref.shape[1]

    @pl.when((pl.program_id(0) == 0) & (pl.program_id(1) == 0))
    def _():
        r = lax.broadcasted_iota(jnp.int32, (ts, ts), 0)
        c = lax.broadcasted_iota(jnp.int32, (ts, ts), 1)
        tri_ref[...] = jnp.where(c <= r, 1.0, 0.0).astype(BF16)
        row = lax.broadcasted_iota(jnp.int32, (N_SPLIT * w, wo), 0)
        lane = lax.broadcasted_iota(jnp.int32, (N_SPLIT * w, wo), 1)
        target = jnp.zeros_like(row)
        for idx in range(N_SPLIT):
            in_part = (row >= idx * w) & (row < (idx + 1) * w)
            target = jnp.where(in_part, (row - idx * w) * HEAD_DIM + idx, target)
        place_ref[...] = jnp.where(lane == target, 1.0, 0.0).astype(BF16)

    @pl.when(pl.program_id(1) == 0)
    def _():
        carry_ref[...] = jnp.zeros_like(carry_ref)

    x = z_ref[0] + b_ref[...]
    lf = jnp.minimum(x, 0.0) - jnp.log(1.0 + jnp.exp(-jnp.abs(x)))
    parts = jnp.dot(tri_ref[...], jnp.concatenate(_split3_bf16(lf), axis=1), preferred_element_type=F32)
    cs = carry_ref[...] + parts[:, :w]
    for idx in range(1, N_SPLIT):
        cs = cs + parts[:, idx * w:(idx + 1) * w]
    carry_ref[...] = cs[ts - 1:ts, :]

    split = jnp.concatenate(_split3_bf16(cs * LOG2E), axis=1)
    bias = jnp.dot(split, place_ref[...], preferred_element_type=F32)
    for hh in range(o_ref.shape[0]):
        o_ref[hh, 0] = bias[:, hh * HEAD_DIM:(hh + 1) * HEAD_DIM].astype(o_ref.dtype)


def _forget_bias(zf, bf_pad, n_heads):
    b, s, w = zf.shape
    ts = min(512, s)
    wo = n_heads * HEAD_DIM
    return pl.pallas_call(
        _fcum_kernel,
        grid=(b, s // ts),
        in_specs=[pl.BlockSpec((1, ts, w), lambda bi, i: (bi, i, 0)),
                  pl.BlockSpec((1, w), lambda bi, i: (0, 0))],
        out_specs=pl.BlockSpec((n_heads, 1, ts, HEAD_DIM), lambda bi, i: (0, bi, i, 0)),
        out_shape=jax.ShapeDtypeStruct((n_heads, b, s, HEAD_DIM), BF16),
        scratch_shapes=[pltpu.VMEM((1, w), F32), pltpu.VMEM((ts, ts), BF16), pltpu.VMEM((N_SPLIT * w, wo), BF16)],
        compiler_params=_cparams(("arbitrary", "arbitrary"), 4 * ts * ts * 4 + 8 * ts * wo * 4),
        name="forget_bias",
    )(zf, bf_pad)


ONES_ROWS = 16


MAX_CHAINS = 8
BIG_CHUNK = 4
STALE_MAX_GUARD = 60.0


def _attn_kernel(q_ref, k_ref, v_ref, fb_ref, o_ref, vt_ref, acc_ref, m_ref, *, tq):
    tk = tq
    s_len = k_ref.shape[1]
    step = pl.program_id(2)

    @pl.when(step == 0)
    def _():
        vt_ref[HEAD_DIM:, :] = jnp.ones((ONES_ROWS, s_len), BF16)
        for c in range(s_len // tk):
            vt_ref[:HEAD_DIM, c * tk:(c + 1) * tk] = v_ref[0, c * tk:(c + 1) * tk, :].astype(F32).T.astype(BF16)

    for sub in range(q_ref.shape[1] // tq):
        rows = slice(sub * tq, (sub + 1) * tq)
        o_ref[0, rows, :] = _attn_query_tile(step * (q_ref.shape[1] // tq) + sub, q_ref[0, rows, :], k_ref, fb_ref,
                                             vt_ref, acc_ref, m_ref).astype(o_ref.dtype)


def _attn_query_tile(qi, q, k_ref, fb_ref, vt_ref, acc_ref, m_ref):
    tq = q.shape[0]
    tk = tq
    lane = lax.broadcasted_iota(jnp.int32, (tq, HEAD_DIM), 1)
    minus_one = jnp.where(lane < 3, -1.0, 0.0).astype(BF16)
    q_aug = jnp.concatenate([q, minus_one], axis=1)
    m_ref[...] = jnp.full(m_ref.shape, MASK_VALUE, F32)
    acc_ref[...] = jnp.zeros_like(acc_ref)

    def logits(k0, size):
        k0 = pl.multiple_of(k0, tk)
        k_aug = jnp.concatenate([k_ref[0, pl.ds(k0, size), :], fb_ref[0, 0, pl.ds(k0, size), :]], axis=1)
        return lax.dot_general(k_aug, q_aug, (((1,), (1,)), ((), ())), preferred_element_type=F32)

    def absorb(st, k0, size, masked):
        k0 = pl.multiple_of(k0, tk)
        if masked:
            krow = lax.broadcasted_iota(jnp.int32, (size, tq), 0)
            qcol = lax.broadcasted_iota(jnp.int32, (size, tq), 1)
            st = jnp.where(krow <= qcol, st, MASK_VALUE)
        part = jnp.max(st.reshape(MAX_CHAINS, size // MAX_CHAINS, tq), axis=1)
        m_old = m_ref[...]
        m_new = jnp.maximum(m_old, jnp.max(part, axis=0, keepdims=True))
        m_ref[...] = m_new
        pt = jnp.exp2((st - m_new).astype(BF16))
        acc_ref[...] = jnp.exp2(m_old - m_new) * acc_ref[...] + jnp.dot(
            vt_ref[:, pl.ds(k0, size)], pt, preferred_element_type=F32)

    def absorb_one_pass(st, k0, size):
        k0 = pl.multiple_of(k0, tk)
        m_old = m_ref[...]
        part = jnp.max(st.reshape(MAX_CHAINS, size // MAX_CHAINS, tq), axis=1)
        cmax = jnp.max(part, axis=0, keepdims=True)
        pt = jnp.exp2(st - m_old).astype(BF16)
        pv = jnp.dot(vt_ref[:, pl.ds(k0, size)], pt, preferred_element_type=F32)
        safe = jnp.max(cmax - m_old) <= STALE_MAX_GUARD

        @pl.when(safe)
        def _():
            m_new = jnp.maximum(m_old, cmax)
            m_ref[...] = m_new
            acc_ref[...] = (acc_ref[...] + pv) * jnp.exp2(m_old - m_new)

        @pl.when(jnp.logical_not(safe))
        def _():
            absorb(logits(k0, size), k0, size, False)

    n_big = qi // BIG_CHUNK
    tail_k0 = n_big * (BIG_CHUNK * tk)
    for visible in range(BIG_CHUNK):
        @pl.when(qi - n_big * BIG_CHUNK == visible)
        def _(visible=visible):
            diag_k0 = tail_k0 + visible * tk
            if visible == 0:
                absorb(logits(diag_k0, tk), diag_k0, tk, True)
            else:
                st_a = logits(tail_k0, visible * tk)
                st_b = logits(diag_k0, tk)
                absorb(st_a, tail_k0, visible * tk, False)
                absorb(st_b, diag_k0, tk, True)

    def body(kc, carry):
        k0 = kc * (BIG_CHUNK * tk)
        absorb_one_pass(logits(k0, BIG_CHUNK * tk), k0, BIG_CHUNK * tk)
        return carry

    lax.fori_loop(0, n_big, body, 0)
    o_t = acc_ref[:HEAD_DIM, :] * (1.0 / acc_ref[HEAD_DIM:HEAD_DIM + 1, :])
    return o_t.T


def _attention(zqkv, fbias, batch, seq):
    n_heads = fbias.shape[0]
    b, s = batch, seq
    tq = min(512, s)
    tiles_per_step = 2 if s % (2 * tq) == 0 else 1
    tqs = tq * tiles_per_step
    spb = s // tqs
    assert s % tqs == 0
    vmem = 2 * (3 * s * HEAD_DIM * 2) + (HEAD_DIM + ONES_ROWS) * s * 2 + 6 * BIG_CHUNK * tq * tq * 4
    kv_spec = lambda off: pl.BlockSpec((1, s, HEAD_DIM), lambda bi, h, i: (off + h, bi, 0))
    return pl.pallas_call(
        functools.partial(_attn_kernel, tq=tq),
        grid=(b, n_heads, spb),
        in_specs=[pl.BlockSpec((1, tqs, HEAD_DIM), lambda bi, h, i: (h, bi * spb + i, 0)),
                  kv_spec(n_heads), kv_spec(2 * n_heads),
                  pl.BlockSpec((1, 1, s, HEAD_DIM), lambda bi, h, i: (h, bi, 0, 0))],
        out_specs=pl.BlockSpec((1, tqs, HEAD_DIM), lambda bi, h, i: (bi, i, h)),
        out_shape=jax.ShapeDtypeStruct((b, s, n_heads * HEAD_DIM), BF16),
        scratch_shapes=[pltpu.VMEM((HEAD_DIM + ONES_ROWS, s), BF16),
                        pltpu.VMEM((HEAD_DIM + ONES_ROWS, tq), F32),
                        pltpu.VMEM((1, tq), F32)],
        compiler_params=_cparams(("parallel", "parallel", "arbitrary"), vmem),
        name="fox_attention",
    )(zqkv, zqkv, zqkv, fbias)


def _pool_kernel(u_ref, halo_ref, w_ref, sc_ref, o_ref):
    tm = u_ref.shape[1]
    i = pl.program_id(1)
    halo = jnp.where(i > 0, halo_ref[0], 0.0)
    ext = jnp.concatenate([halo, u_ref[0]], axis=0)
    pos = (i * tm + 1 + lax.broadcasted_iota(jnp.int32, (tm, 1), 0)).astype(F32)
    gd = LANES
    for g, win in enumerate(POOL_WINDOWS):
        e = ext[:, g * gd:(g + 1) * gd]
        ssum = e
        shift = 1
        while shift < win:
            ssum = ssum + pltpu.roll(ssum, shift, 0)
            shift *= 2
        mean = ssum[POOL_HALO:] * (1.0 / jnp.minimum(pos, float(win)))
        dlt = mean - e[POOL_HALO:]
        y = jnp.dot(dlt.astype(BF16), w_ref[0, g], preferred_element_type=F32)
        o_ref[0, :, g * gd:(g + 1) * gd] = (y * sc_ref[:, g * gd:(g + 1) * gd]).astype(o_ref.dtype)


def _pool_mixer(z32, pool_w_stack, layer, scale_row, width):
    b, s, _ = z32.shape
    tm = min(512, s)
    hb = tm // POOL_HALO
    return pl.pallas_call(
        _pool_kernel,
        grid=(b, s // tm),
        in_specs=[pl.BlockSpec((1, tm, width), lambda bi, i: (bi, i, 0)),
                  pl.BlockSpec((1, POOL_HALO, width), lambda bi, i: (bi, jnp.maximum(i * hb - 1, 0), 0)),
                  pl.BlockSpec((1, N_POOL_GROUPS, LANES, LANES), lambda bi, i: (layer, 0, 0, 0)),
                  pl.BlockSpec((1, width), lambda bi, i: (0, 0))],
        out_specs=pl.BlockSpec((1, tm, width), lambda bi, i: (bi, i, 0)),
        out_shape=jax.ShapeDtypeStruct((b, s, width), BF16),
        compiler_params=_cparams(("parallel", "parallel"), 16 * tm * width * 4),
        name="pool_mixer",
    )(z32, z32, pool_w_stack, scale_row)


def _lru_kernel(x_ref, halo_ref, y_ref, cw_ref, cb_ref, wa_ref, ba_ref, wi_ref, bi_ref, lam_ref,
                o_ref, h_ref, a_s, b_s, h_s):
    tm = x_ref.shape[1]
    i = pl.program_id(1)

    @pl.when(i == 0)
    def _():
        h_ref[...] = jnp.zeros_like(h_ref)

    halo = jnp.where(i > 0, halo_ref[0], 0.0)
    ext = jnp.concatenate([halo, x_ref[0]], axis=0)
    xc = cb_ref[...] + cw_ref[LRU_CONV_WIDTH - 1:LRU_CONV_WIDTH, :] * ext[SUBLANES:]
    for k in range(LRU_CONV_WIDTH - 1):
        shifted = pltpu.roll(ext, LRU_CONV_WIDTH - 1 - k, 0)[SUBLANES:]
        xc = xc + cw_ref[k:k + 1, :] * shifted

    lam = lam_ref[...]
    neg_softplus = -(jnp.maximum(-lam, 0.0) + jnp.log(1.0 + jnp.exp(-jnp.abs(lam))))
    gd = LANES
    n_groups = tm // SUBLANES
    sub_row = lax.broadcasted_iota(jnp.int32, (n_groups, SUBLANES, gd), 1)
    for blk in range(N_LRU_BLOCKS):
        sl = slice(blk * gd, (blk + 1) * gd)
        xb = xc[:, sl]
        xb16 = xb.astype(BF16)
        gate_r = _sigmoid(jnp.dot(xb16, wa_ref[0, blk], preferred_element_type=F32) + ba_ref[:, sl])
        gate_i = _sigmoid(jnp.dot(xb16, wi_ref[0, blk], preferred_element_type=F32) + bi_ref[:, sl])
        log_a = LRU_C * gate_r * neg_softplus[:, sl]
        a = jnp.exp(log_a)
        b = jnp.sqrt(1.0 - a * a) * (gate_i * xb)
        a = a.reshape(n_groups, SUBLANES, gd)
        b = b.reshape(n_groups, SUBLANES, gd)
        dist = 1
        while dist < SUBLANES:
            has_prev = sub_row >= dist
            b = b + a * jnp.where(has_prev, pltpu.roll(b, dist, 1), 0.0)
            a = a * jnp.where(has_prev, pltpu.roll(a, dist, 1), 1.0)
            dist *= 2
        a_s[blk] = a
        b_s[blk] = b

    def group(gi, hs):
        out = []
        for blk in range(N_LRU_BLOCKS):
            rows = b_s[blk, gi] + a_s[blk, gi] * hs[blk]
            h_s[blk, gi] = rows
            out.append(rows[SUBLANES - 1:SUBLANES, :])
        return tuple(out)

    hs = lax.fori_loop(0, n_groups, group,
                       tuple(h_ref[:, blk * gd:(blk + 1) * gd] for blk in range(N_LRU_BLOCKS)), unroll=8)

    for blk in range(N_LRU_BLOCKS):
        sl = slice(blk * gd, (blk + 1) * gd)
        h_ref[:, sl] = hs[blk]
        y = y_ref[0, :, sl]
        gelu = 0.5 * y * (1.0 + jnp.tanh(0.7978845608028654 * (y + 0.044715 * (y * y * y))))
        o_ref[0, :, sl] = (h_s[blk].reshape(tm, gd) * gelu).astype(o_ref.dtype)


def _lru_mixer(z32, layer, p, width, x_col, y_col):
    b, s, _ = z32.shape
    tm = min(512, s)
    hb = tm // SUBLANES
    row = lambda bi, i: (0, 0)
    return pl.pallas_call(
        _lru_kernel,
        grid=(b, s // tm),
        in_specs=[pl.BlockSpec((1, tm, width), lambda bi, i: (bi, i, x_col)),
                  pl.BlockSpec((1, SUBLANES, width), lambda bi, i: (bi, jnp.maximum(i * hb - 1, 0), x_col)),
                  pl.BlockSpec((1, tm, width), lambda bi, i: (bi, i, y_col)),
                  pl.BlockSpec((LRU_CONV_WIDTH, width), row),
                  pl.BlockSpec((1, width), row),
                  pl.BlockSpec((1, N_LRU_BLOCKS, LANES, LANES), lambda bi, i: (layer, 0, 0, 0)),
                  pl.BlockSpec((1, width), row),
                  pl.BlockSpec((1, N_LRU_BLOCKS, LANES, LANES), lambda bi, i: (layer, 0, 0, 0)),
                  pl.BlockSpec((1, width), row),
                  pl.BlockSpec((1, width), row)],
        out_specs=pl.BlockSpec((1, tm, width), lambda bi, i: (bi, i, 0)),
        out_shape=jax.ShapeDtypeStruct((b, s, width), BF16),
        scratch_shapes=[pltpu.VMEM((1, width), F32)]
        + [pltpu.VMEM((N_LRU_BLOCKS, tm // SUBLANES, SUBLANES, LANES), F32)] * 3,
        compiler_params=_cparams(("arbitrary", "arbitrary"), 24 * tm * width * 4),
        name="rg_lru",
    )(z32, z32, z32, p["conv_w"], p["conv_b"], p["wa"], p["ba"], p["wi"], p["bi"], p["lam"])


COL_BLOCK = 512


def _col_blocks(d):
    return [slice(c, c + COL_BLOCK) for c in range(0, d, COL_BLOCK)]


def _residual_cols(acc, cols, x_ref, gt_ref, xn_refs):
    xn = x_ref[:, cols] + gt_ref[0, :, cols] * acc
    for ref in xn_refs:
        ref[:, cols] = xn
    return jnp.sum(xn * xn, axis=-1, keepdims=True)


def _norm_rows(sumsq, xn_ref, g_ref, sc_ref, sh_ref, h_ref):
    rs = lax.rsqrt(sumsq * (1.0 / xn_ref.shape[1]) + EPS)
    gain = g_ref[...] * (1.0 + sc_ref[0])
    h_ref[...] = (xn_ref[...] * rs * gain + sh_ref[0]).astype(h_ref.dtype)


def _mix_out_kernel(yp_ref, ya_ref, yl_ref, w_ref, x_ref, gt_ref, g_ref, sc_ref, sh_ref, xo_ref, h_ref):
    kp = yp_ref.shape[1]
    ka = ya_ref.shape[1]
    sumsq = jnp.zeros((x_ref.shape[0], 1), F32)
    for cols in _col_blocks(x_ref.shape[1]):
        acc = jnp.dot(yp_ref[...], w_ref[0, 0:kp, cols], preferred_element_type=F32)
        acc = acc + jnp.dot(ya_ref[...], w_ref[0, kp:kp + ka, cols], preferred_element_type=F32)
        acc = acc + jnp.dot(yl_ref[...], w_ref[0, kp + ka:, cols], preferred_element_type=F32)
        sumsq = sumsq + _residual_cols(acc, cols, x_ref, gt_ref, (xo_ref,))
    _norm_rows(sumsq, xo_ref, g_ref, sc_ref, sh_ref, h_ref)


def _mix_out(yp, ya, yl, w_stack, layer, x2, gt, g, sc, sh, seq):
    t, d = x2.shape
    tm = min(512, seq)
    tpb = seq // tm
    per_b = lambda i: (i // tpb, 0, 0)
    lhs = lambda y: pl.BlockSpec((tm, y.shape[1]), lambda i: (i, 0))
    vmem = 2 * d * d * 2 + 2 * (tm * d * 2 + 3 * tm * d * 4) + 3 * tm * d * 4
    return pl.pallas_call(
        _mix_out_kernel,
        grid=(t // tm,),
        in_specs=[lhs(yp), lhs(ya), lhs(yl),
                  pl.BlockSpec((1, d, d), lambda i: (layer, 0, 0)),
                  pl.BlockSpec((tm, d), lambda i: (i, 0)),
                  pl.BlockSpec((1, 1, d), per_b),
                  pl.BlockSpec((1, d), lambda i: (0, 0)),
                  pl.BlockSpec((1, 1, d), per_b),
                  pl.BlockSpec((1, 1, d), per_b)],
        out_specs=[pl.BlockSpec((tm, d), lambda i: (i, 0)), pl.BlockSpec((tm, d), lambda i: (i, 0))],
        out_shape=[jax.ShapeDtypeStruct((t, d), F32), jax.ShapeDtypeStruct((t, d), BF16)],
        compiler_params=_cparams(("parallel",), vmem),
        name="mix_out_proj",
    )(yp, ya, yl, w_stack, x2, gt, g, sc, sh)


def _ffn_down_kernel(a_ref, w_ref, x_ref, gt_ref, g_ref, sc_ref, sh_ref, *out_refs):
    xn_ref, h_ref = out_refs[0], out_refs[-1]
    sumsq = jnp.zeros((x_ref.shape[0], 1), F32)
    for cols in _col_blocks(x_ref.shape[1]):
        acc = jnp.dot(a_ref[...], w_ref[0, :, cols], preferred_element_type=F32)
        sumsq = sumsq + _residual_cols(acc, cols, x_ref, gt_ref, (xn_ref,))
    _norm_rows(sumsq, xn_ref, g_ref, sc_ref, sh_ref, h_ref)


def _ffn_down(act, w_stack, layer, x2, gt, g, sc, sh, seq, emit_x, h_dtype):
    t, d = x2.shape
    kf = act.shape[1]
    tm = min(512, seq)
    tpb = seq // tm
    per_b = lambda i: (i // tpb, 0, 0)
    row_tile = pl.BlockSpec((tm, d), lambda i: (i, 0))
    out_specs = [row_tile]
    out_shape = [jax.ShapeDtypeStruct((t, d), h_dtype)]
    if emit_x:
        out_specs = [row_tile, row_tile]
        out_shape = [jax.ShapeDtypeStruct((t, d), F32)] + out_shape
    else:
        assert h_dtype == F32
    vmem = kf * d * 2 + 2 * (tm * kf * 2 + 3 * tm * d * 4) + 2 * tm * COL_BLOCK * 4
    return pl.pallas_call(
        _ffn_down_kernel,
        grid=(t // tm,),
        in_specs=[pl.BlockSpec((tm, kf), lambda i: (i, 0)),
                  pl.BlockSpec((1, kf, d), lambda i: (layer, 0, 0), pipeline_mode=pl.Buffered(1)),
                  row_tile,
                  pl.BlockSpec((1, 1, d), per_b),
                  pl.BlockSpec((1, d), lambda i: (0, 0)),
                  pl.BlockSpec((1, 1, d), per_b),
                  pl.BlockSpec((1, 1, d), per_b)],
        out_specs=out_specs,
        out_shape=out_shape,
        compiler_params=_cparams(("parallel",), vmem),
        name="ffn_down_proj",
    )(act, w_stack, x2, gt, g, sc, sh)


def _ffn_gate_kernel(h_ref, wg_ref, wu_ref, cw_ref, cb_ref, o_ref, carry_ref, *, tiles_per_batch):
    tm = h_ref.shape[0]
    i = pl.program_id(0)
    j = pl.program_id(1)

    @pl.when(i % tiles_per_batch == 0)
    def _():
        carry_ref[j] = jnp.zeros(carry_ref.shape[1:], F32)

    h = h_ref[...]
    u = jnp.dot(h, wg_ref[0].astype(BF16), preferred_element_type=F32)
    ext = jnp.concatenate([carry_ref[j], u], axis=0)
    carry_ref[j] = u[tm - SUBLANES:, :]
    g = cb_ref[...] + cw_ref[FFN_CONV_WIDTH - 1:FFN_CONV_WIDTH, :] * u
    for k in range(FFN_CONV_WIDTH - 1):
        shifted = pltpu.roll(ext, FFN_CONV_WIDTH - 1 - k, 0)[SUBLANES:]
        g = g + cw_ref[k:k + 1, :] * shifted
    up = jnp.dot(h, wu_ref[0].astype(BF16), preferred_element_type=F32)
    o_ref[...] = (g * _sigmoid(g) * up).astype(o_ref.dtype)


def _ffn_gate(h, wg_stack, wu_stack, layer, conv_w, conv_b, seq):
    t, d = h.shape
    kf = wg_stack.shape[2]
    tm = min(1024, seq)
    tf = 512
    wspec = pl.BlockSpec((1, d, tf), lambda i, j: (layer, 0, j))
    vmem = 2 * (tm * d * 2 + 2 * d * tf * 4 + tm * tf * 2) + 2 * d * tf * 2 + 8 * tm * tf * 4
    return pl.pallas_call(
        functools.partial(_ffn_gate_kernel, tiles_per_batch=seq // tm),
        grid=(t // tm, kf // tf),
        in_specs=[pl.BlockSpec((tm, d), lambda i, j: (i, 0)), wspec, wspec,
                  pl.BlockSpec((FFN_CONV_WIDTH, tf), lambda i, j: (0, j)),
                  pl.BlockSpec((1, tf), lambda i, j: (0, j))],
        out_specs=pl.BlockSpec((tm, tf), lambda i, j: (i, j)),
        out_shape=jax.ShapeDtypeStruct((t, kf), BF16),
        scratch_shapes=[pltpu.VMEM((kf // tf, SUBLANES, tf), F32)],
        compiler_params=_cparams(("arbitrary", "arbitrary"), vmem),
        name="ffn_gate_up",
    )(h, wg_stack, wu_stack, conv_w, conv_b)


def kernel(x, c, w_ada, b_ada, g_mix, w_in, b_f, pool_w, pool_scale, lru_conv_w, lru_conv_b, lru_wa, lru_ba,
           lru_wi, lru_bi, lru_lambda, w_out, g_ffn, w_ffn_gate, w_ffn_up, ffn_conv_w, ffn_conv_b, w_ffn_down,
           final_g):
    batch, seq, d = x.shape
    depth = w_ada.shape[0]
    pool_width = pool_w.shape[1] * pool_w.shape[2]
    lru_width = lru_lambda.shape[1]
    n_heads = b_f.shape[1]
    attn_width = n_heads * HEAD_DIM
    assert w_in.shape[2] == pool_width + 3 * attn_width + n_heads + 2 * lru_width
    assert pool_width == lru_width == N_POOL_GROUPS * LANES and seq % SUBLANES == 0

    o_q = pool_width
    o_f = o_q + 3 * attn_width
    o_x = o_f + n_heads
    w_in_t = jnp.swapaxes(w_in, 1, 2)
    w_f = jnp.pad(w_in_t[:, o_f:o_x], ((0, 0), (0, LANES - n_heads), (0, 0)))
    w_proj = jnp.concatenate([w_in_t[:, o_q:o_f], w_in_t[:, :o_q], w_in_t[:, o_x:], w_f], axis=1).astype(BF16)
    qkv_width = 3 * attn_width
    pxy_width = pool_width + 2 * lru_width
    bf_pad = jnp.pad(b_f, ((0, 0), (0, LANES - n_heads)))
    qkv_scale = jnp.concatenate([jnp.full((1, attn_width), HEAD_DIM ** -0.5 * LOG2E, F32),
                                 jnp.ones((1, 2 * attn_width), F32)], axis=1)
    pool_w16 = pool_w.astype(BF16)
    lru_wa16 = lru_wa.astype(BF16)
    lru_wi16 = lru_wi.astype(BF16)
    w_out16 = w_out.astype(BF16)
    w_down16 = w_ffn_down.astype(BF16)

    mod = _ada_mod(c, w_ada, b_ada)[:, :batch]

    def mod_chunk(layer, idx):
        return mod[layer, :, idx * d:(idx + 1) * d].reshape(batch, 1, d)

    x2 = x.reshape(batch * seq, d)
    h = _norm(x2, g_mix[0][None], mod_chunk(0, 1), mod_chunk(0, 0), seq)
    out = None
    for layer in range(depth):
        sh1, sc1, gt1, sh2, sc2, gt2 = (mod_chunk(layer, idx) for idx in range(6))
        zqkv = _matmul_heads(h, w_proj, layer, qkv_width, qkv_width // 2, qkv_scale)
        z32, zf = _matmul_wide_narrow(h, w_proj, layer, qkv_width, pxy_width, LANES)
        z32 = z32.reshape(batch, seq, pxy_width)

        fbias = _forget_bias(zf.reshape(batch, seq, LANES), bf_pad[layer][None], n_heads)
        y_attn = _attention(zqkv, fbias, batch, seq).reshape(batch * seq, attn_width)
        y_pool = _pool_mixer(z32, pool_w16, layer, pool_scale[layer][None], pool_width)
        lru_p = dict(conv_w=lru_conv_w[layer], conv_b=lru_conv_b[layer][None], wa=lru_wa16,
                     ba=lru_ba[layer][None], wi=lru_wi16, bi=lru_bi[layer][None], lam=lru_lambda[layer][None])
        y_lru = _lru_mixer(z32, layer, lru_p, lru_width, 1, 2)

        x2, h = _mix_out(y_pool.reshape(batch * seq, pool_width), y_attn, y_lru.reshape(batch * seq, lru_width),
                         w_out16, layer, x2, gt1, g_ffn[layer][None], sc2, sh2, seq)
        act = _ffn_gate(h, w_ffn_gate, w_ffn_up, layer, ffn_conv_w[layer], ffn_conv_b[layer][None], seq)
        if layer + 1 < depth:
            x2, h = _ffn_down(act, w_down16, layer, x2, gt2, g_mix[layer + 1][None], mod_chunk(layer + 1, 1),
                              mod_chunk(layer + 1, 0), seq, True, BF16)
        else:
            zeros = jnp.zeros((batch, 1, d), F32)
            (out,) = _ffn_down(act, w_down16, layer, x2, gt2, final_g[None], zeros, zeros, seq, False, F32)
    return out.reshape(batch, seq, d)
```

```python
import functools

import jax
import jax.numpy as jnp
from jax import lax
from jax.experimental import pallas as pl
from jax.experimental.pallas import tpu as pltpu

F32 = jnp.float32
BF16 = jnp.bfloat16

HEAD_DIM = 128
N_POOL_GROUPS = 4
POOL_WINDOWS = (2, 4, 8, 16)
POOL_HALO = 16
N_LRU_BLOCKS = 4
LRU_CONV_WIDTH = 4
LRU_C = 8.0
FFN_CONV_WIDTH = 3
EPS = 1e-6

LANES = 128
SUBLANES = 8
VMEM_LIMIT_CAP_V7X = 58 * 2**20
MASK_VALUE = -1e30
LOG2E = 1.4426950408889634


def _cparams(semantics, vmem_bytes):
    return pltpu.CompilerParams(dimension_semantics=semantics,
                                vmem_limit_bytes=int(min(VMEM_LIMIT_CAP_V7X, max(vmem_bytes, 16 * 2**20))))


def _sigmoid(x):
    return 0.5 * jnp.tanh(0.5 * x) + 0.5


def _norm_mod(x, g, sc, sh):
    ms = jnp.mean(x * x, axis=-1, keepdims=True)
    return (x * lax.rsqrt(ms + EPS) * g) * (1.0 + sc) + sh


def _split3_bf16(x):
    hi = x.astype(BF16)
    r1 = x - hi.astype(F32)
    mid = r1.astype(BF16)
    lo = (r1 - mid.astype(F32)).astype(BF16)
    return hi, mid, lo


def _ada_kernel(c_ref, w_ref, b_ref, o_ref, acc_ref):
    nb = c_ref.shape[0]
    n = w_ref.shape[3]
    k = pl.program_id(1)

    @pl.when(k == 0)
    def _():
        acc_ref[...] = jnp.zeros_like(acc_ref)

    ca = [c_ref[b] * _sigmoid(c_ref[b]) for b in range(nb)]

    def lane_chunk(jc, carry):
        sl = pl.ds(pl.multiple_of(jc * LANES, LANES), LANES)
        w = w_ref[0, :, :, sl]
        for b in range(nb):
            acc_ref[b, :, sl] += jnp.sum(w * ca[b], axis=0)
        return carry

    lax.fori_loop(0, n // LANES, lane_chunk, 0, unroll=4)

    @pl.when(k == pl.num_programs(1) - 1)
    def _():
        o_ref[...] = jnp.zeros_like(o_ref)
        for b in range(nb):
            o_ref[0, b:b + 1, :] = jnp.sum(acc_ref[b], axis=0, keepdims=True) + b_ref[0]


ADA_SLAB_ROWS = 256


def _ada_mod(c, w_ada, b_ada):
    depth, d, n = w_ada.shape
    batch = c.shape[0]
    dr = d // SUBLANES
    sr = ADA_SLAB_ROWS // SUBLANES
    c_lanes = jnp.broadcast_to(c[:, :, None], (batch, d, LANES)).reshape(batch, dr, SUBLANES, LANES)
    return pl.pallas_call(
        _ada_kernel,
        grid=(depth, dr // sr),
        in_specs=[pl.BlockSpec((batch, sr, SUBLANES, LANES), lambda l, k: (0, k, 0, 0)),
                  pl.BlockSpec((1, sr, SUBLANES, n), lambda l, k: (l, k, 0, 0)),
                  pl.BlockSpec((1, 1, n), lambda l, k: (l, 0, 0))],
        out_specs=pl.BlockSpec((1, SUBLANES, n), lambda l, k: (l, 0, 0)),
        out_shape=jax.ShapeDtypeStruct((depth, SUBLANES, n), F32),
        scratch_shapes=[pltpu.VMEM((batch, SUBLANES, n), F32)],
        compiler_params=_cparams(("parallel", "arbitrary"), 3 * ADA_SLAB_ROWS * n * 4),
        name="ada_mod",
    )(c_lanes, w_ada.reshape(depth, dr, SUBLANES, n), b_ada.reshape(depth, 1, n))


def _norm_kernel(x_ref, g_ref, sc_ref, sh_ref, h_ref):
    h_ref[...] = _norm_mod(x_ref[...], g_ref[...], sc_ref[0], sh_ref[0]).astype(h_ref.dtype)


def _norm(x2, g, sc, sh, seq):
    t, d = x2.shape
    tm = min(512, seq)
    tpb = seq // tm
    return pl.pallas_call(
        _norm_kernel,
        grid=(t // tm,),
        in_specs=[pl.BlockSpec((tm, d), lambda i: (i, 0)),
                  pl.BlockSpec((1, d), lambda i: (0, 0)),
                  pl.BlockSpec((1, 1, d), lambda i: (i // tpb, 0, 0)),
                  pl.BlockSpec((1, 1, d), lambda i: (i // tpb, 0, 0))],
        out_specs=pl.BlockSpec((tm, d), lambda i: (i, 0)),
        out_shape=jax.ShapeDtypeStruct((t, d), BF16),
        compiler_params=_cparams(("parallel",), 6 * tm * d * 4),
        name="norm_mod",
    )(x2, g, sc, sh)


_NT_DIMS = (((1,), (1,)), ((), ()))


def _mm2_kernel(a_ref, w_ref, wn_ref, o_ref, on_ref):
    o_ref[...] = lax.dot_general(a_ref[...], w_ref[0], _NT_DIMS, preferred_element_type=F32)
    on_ref[...] = lax.dot_general(a_ref[...], wn_ref[0], _NT_DIMS, preferred_element_type=F32)


def _mm_heads_kernel(a_ref, w_ref, cs_ref, o_ref):
    acc = lax.dot_general(a_ref[...], w_ref[0], _NT_DIMS, preferred_element_type=F32) * cs_ref[...]
    for hh in range(o_ref.shape[0]):
        o_ref[hh] = acc[:, hh * HEAD_DIM:(hh + 1) * HEAD_DIM].astype(o_ref.dtype)


def _matmul_wide_narrow(a, wt_stack, layer, row0, wide, narrow):
    t, k = a.shape
    tm = min(1024, t)
    assert row0 % wide == 0 and (row0 + wide) % narrow == 0
    vmem = 2 * (tm * k * 2 + k * (wide + narrow) * 2 + tm * (wide + narrow) * 4) + 2 * tm * wide * 4
    return pl.pallas_call(
        _mm2_kernel,
        grid=(t // tm,),
        in_specs=[pl.BlockSpec((tm, k), lambda i: (i, 0)),
                  pl.BlockSpec((1, wide, k), lambda i: (layer, row0 // wide, 0)),
                  pl.BlockSpec((1, narrow, k), lambda i: (layer, (row0 + wide) // narrow, 0))],
        out_specs=[pl.BlockSpec((tm, wide), lambda i: (i, 0)), pl.BlockSpec((tm, narrow), lambda i: (i, 0))],
        out_shape=[jax.ShapeDtypeStruct((t, wide), F32), jax.ShapeDtypeStruct((t, narrow), F32)],
        compiler_params=_cparams(("parallel",), vmem),
        name="matmul_dmodel",
    )(a, wt_stack, wt_stack)


def _matmul_heads(a, wt_stack, layer, n, tn, col_scale):
    t, k = a.shape
    tm = min(1024, t)
    hpb = tn // HEAD_DIM
    vmem = 2 * (tm * k * 2 + k * tn * 2 + tm * tn * 2) + 2 * tm * tn * 4
    return pl.pallas_call(
        _mm_heads_kernel,
        grid=(t // tm, n // tn),
        in_specs=[pl.BlockSpec((tm, k), lambda i, j: (i, 0)),
                  pl.BlockSpec((1, tn, k), lambda i, j: (layer, j, 0)),
                  pl.BlockSpec((1, tn), lambda i, j: (0, j))],
        out_specs=pl.BlockSpec((hpb, tm, HEAD_DIM), lambda i, j: (j, i, 0)),
        out_shape=jax.ShapeDtypeStruct((n // HEAD_DIM, t, HEAD_DIM), BF16),
        compiler_params=_cparams(("parallel", "parallel"), vmem),
        name="matmul_heads",
    )(a, wt_stack, col_scale)


N_SPLIT = 3


def _fcum_kernel(z_ref, b_ref, o_ref, carry_ref, tri_ref, place_ref):
    ts = z_ref.shape[1]
    w = z_ref.shape[2]
    wo = place_ref.shape[1]

    @pl.when((pl.program_id(0) == 0) & (pl.program_id(1) == 0))
    def _():
        r = lax.broadcasted_iota(jnp.int32, (ts, ts), 0)
        c = lax.broadcasted_iota(jnp.int32, (ts, ts), 1)
        tri_ref[...] = jnp.where(c <= r, 1.0, 0.0).astype(BF16)
        row = lax.broadcasted_iota(jnp.int32, (N_SPLIT * w, wo), 0)
        lane = lax.broadcasted_iota(jnp.int32, (N_SPLIT * w, wo), 1)
        target = jnp.zeros_like(row)
        for idx in range(N_SPLIT):
            in_part = (row >= idx * w) & (row < (idx + 1) * w)
            target = jnp.where(in_part, (row - idx * w) * HEAD_DIM + idx, target)
        place_ref[...] = jnp.where(lane == target, 1.0, 0.0).astype(BF16)

    @pl.when(pl.program_id(1) == 0)
    def _():
        carry_ref[...] = jnp.zeros_like(carry_ref)

    x = z_ref[0] + b_ref[...]
    lf = jnp.minimum(x, 0.0) - jnp.log(1.0 + jnp.exp(-jnp.abs(x)))
    parts = jnp.dot(tri_ref[...], jnp.concatenate(_split3_bf16(lf), axis=1), preferred_element_type=F32)
    cs = carry_ref[...] + parts[:, :w]
    for idx in range(1, N_SPLIT):
        cs = cs + parts[:, idx * w:(idx + 1) * w]
    carry_ref[...] = cs[ts - 1:ts, :]

    split = jnp.concatenate(_split3_bf16(cs * LOG2E), axis=1)
    bias = jnp.dot(split, place_ref[...], preferred_element_type=F32)
    for hh in range(o_ref.shape[0]):
        o_ref[hh, 0] = bias[:, hh * HEAD_DIM:(hh + 1) * HEAD_DIM].astype(o_ref.dtype)


def _forget_bias(zf, bf_pad, n_heads):
    b, s, w = zf.shape
    ts = min(512, s)
    wo = n_heads * HEAD_DIM
    return pl.pallas_call(
        _fcum_kernel,
        grid=(b, s // ts),
        in_specs=[pl.BlockSpec((1, ts, w), lambda bi, i: (bi, i, 0)),
                  pl.BlockSpec((1, w), lambda bi, i: (0, 0))],
        out_specs=pl.BlockSpec((n_heads, 1, ts, HEAD_DIM), lambda bi, i: (0, bi, i, 0)),
        out_shape=jax.ShapeDtypeStruct((n_heads, b, s, HEAD_DIM), BF16),
        scratch_shapes=[pltpu.VMEM((1, w), F32), pltpu.VMEM((ts, ts), BF16), pltpu.VMEM((N_SPLIT * w, wo), BF16)],
        compiler_params=_cparams(("arbitrary", "arbitrary"), 4 * ts * ts * 4 + 8 * ts * wo * 4),
        name="forget_bias",
    )(zf, bf_pad)


ONES_ROWS = 16


MAX_CHAINS = 8
BIG_CHUNK = 4
STALE_MAX_GUARD = 60.0


def _attn_kernel(q_ref, k_ref, v_ref, fb_ref, o_ref, vt_ref, acc_ref, m_ref, *, tq):
    tk = tq
    s_len = k_ref.shape[1]
    step = pl.program_id(2)

    @pl.when(step == 0)
    def _():
        vt_ref[HEAD_DIM:, :] = jnp.ones((ONES_ROWS, s_len), BF16)
        for c in range(s_len // tk):
            vt_ref[:HEAD_DIM, c * tk:(c + 1) * tk] = v_ref[0, c * tk:(c + 1) * tk, :].astype(F32).T.astype(BF16)

    for sub in range(q_ref.shape[1] // tq):
        rows = slice(sub * tq, (sub + 1) * tq)
        o_ref[0, rows, :] = _attn_query_tile(step * (q_ref.shape[1] // tq) + sub, q_ref[0, rows, :], k_ref, fb_ref,
                                             vt_ref, acc_ref, m_ref).astype(o_ref.dtype)


def _attn_query_tile(qi, q, k_ref, fb_ref, vt_ref, acc_ref, m_ref):
    tq = q.shape[0]
    tk = tq
    lane = lax.broadcasted_iota(jnp.int32, (tq, HEAD_DIM), 1)
    minus_one = jnp.where(lane < 3, -1.0, 0.0).astype(BF16)
    q_aug = jnp.concatenate([q, minus_one], axis=1)
    m_ref[...] = jnp.full(m_ref.shape, MASK_VALUE, F32)
    acc_ref[...] = jnp.zeros_like(acc_ref)

    def logits(k0, size):
        k0 = pl.multiple_of(k0, tk)
        k_aug = jnp.concatenate([k_ref[0, pl.ds(k0, size), :], fb_ref[0, 0, pl.ds(k0, size), :]], axis=1)
        return lax.dot_general(k_aug, q_aug, (((1,), (1,)), ((), ())), preferred_element_type=F32)

    def absorb(st, k0, size, masked):
        k0 = pl.multiple_of(k0, tk)
        if masked:
            krow = lax.broadcasted_iota(jnp.int32, (size, tq), 0)
            qcol = lax.broadcasted_iota(jnp.int32, (size, tq), 1)
            st = jnp.where(krow <= qcol, st, MASK_VALUE)
        part = jnp.max(st.reshape(MAX_CHAINS, size // MAX_CHAINS, tq), axis=1)
        m_old = m_ref[...]
        m_new = jnp.maximum(m_old, jnp.max(part, axis=0, keepdims=True))
        m_ref[...] = m_new
        pt = jnp.exp2((st - m_new).astype(BF16))
        acc_ref[...] = jnp.exp2(m_old - m_new) * acc_ref[...] + jnp.dot(
            vt_ref[:, pl.ds(k0, size)], pt, preferred_element_type=F32)

    def absorb_one_pass(st, k0, size):
        k0 = pl.multiple_of(k0, tk)
        m_old = m_ref[...]
        part = jnp.max(st.reshape(MAX_CHAINS, size // MAX_CHAINS, tq), axis=1)
        cmax = jnp.max(part, axis=0, keepdims=True)
        pt = jnp.exp2(st - m_old).astype(BF16)
        pv = jnp.dot(vt_ref[:, pl.ds(k0, size)], pt, preferred_element_type=F32)
        safe = jnp.max(cmax - m_old) <= STALE_MAX_GUARD

        @pl.when(safe)
        def _():
            m_new = jnp.maximum(m_old, cmax)
            m_ref[...] = m_new
            acc_ref[...] = (acc_ref[...] + pv) * jnp.exp2(m_old - m_new)

        @pl.when(jnp.logical_not(safe))
        def _():
            absorb(logits(k0, size), k0, size, False)

    n_big = qi // BIG_CHUNK
    tail_k0 = n_big * (BIG_CHUNK * tk)
    for visible in range(BIG_CHUNK):
        @pl.when(qi - n_big * BIG_CHUNK == visible)
        def _(visible=visible):
            diag_k0 = tail_k0 + visible * tk
            if visible == 0:
                absorb(logits(diag_k0, tk), diag_k0, tk, True)
            else:
                st_a = logits(tail_k0, visible * tk)
                st_b = logits(diag_k0, tk)
                absorb(st_a, tail_k0, visible * tk, False)
                absorb(st_b, diag_k0, tk, True)

    def body(kc, carry):
        k0 = kc * (BIG_CHUNK * tk)
        absorb_one_pass(logits(k0, BIG_CHUNK * tk), k0, BIG_CHUNK * tk)
        return carry

    lax.fori_loop(0, n_big, body, 0)
    o_t = acc_ref[:HEAD_DIM, :] * (1.0 / acc_ref[HEAD_DIM:HEAD_DIM + 1, :])
    return o_t.T


def _attention(zqkv, fbias, batch, seq):
    n_heads = fbias.shape[0]
    b, s = batch, seq
    tq = min(512, s)
    tiles_per_step = 2 if s % (2 * tq) == 0 else 1
    tqs = tq * tiles_per_step
    spb = s // tqs
    assert s % tqs == 0
    vmem = 2 * (3 * s * HEAD_DIM * 2) + (HEAD_DIM + ONES_ROWS) * s * 2 + 6 * BIG_CHUNK * tq * tq * 4
    kv_spec = lambda off: pl.BlockSpec((1, s, HEAD_DIM), lambda bi, h, i: (off + h, bi, 0))
    return pl.pallas_call(
        functools.partial(_attn_kernel, tq=tq),
        grid=(b, n_heads, spb),
        in_specs=[pl.BlockSpec((1, tqs, HEAD_DIM), lambda bi, h, i: (h, bi * spb + i, 0)),
                  kv_spec(n_heads), kv_spec(2 * n_heads),
                  pl.BlockSpec((1, 1, s, HEAD_DIM), lambda bi, h, i: (h, bi, 0, 0))],
        out_specs=pl.BlockSpec((1, tqs, HEAD_DIM), lambda bi, h, i: (bi, i, h)),
        out_shape=jax.ShapeDtypeStruct((b, s, n_heads * HEAD_DIM), BF16),
        scratch_shapes=[pltpu.VMEM((HEAD_DIM + ONES_ROWS, s), BF16),
                        pltpu.VMEM((HEAD_DIM + ONES_ROWS, tq), F32),
                        pltpu.VMEM((1, tq), F32)],
        compiler_params=_cparams(("parallel", "parallel", "arbitrary"), vmem),
        name="fox_attention",
    )(zqkv, zqkv, zqkv, fbias)


def _pool_kernel(u_ref, halo_ref, w_ref, sc_ref, o_ref):
    tm = u_ref.shape[1]
    i = pl.program_id(1)
    halo = jnp.where(i > 0, halo_ref[0], 0.0)
    ext = jnp.concatenate([halo, u_ref[0]], axis=0)
    pos = (i * tm + 1 + lax.broadcasted_iota(jnp.int32, (tm, 1), 0)).astype(F32)
    gd = LANES
    for g, win in enumerate(POOL_WINDOWS):
        e = ext[:, g * gd:(g + 1) * gd]
        ssum = e
        shift = 1
        while shift < win:
            ssum = ssum + pltpu.roll(ssum, shift, 0)
            shift *= 2
        mean = ssum[POOL_HALO:] * (1.0 / jnp.minimum(pos, float(win)))
        dlt = mean - e[POOL_HALO:]
        y = jnp.dot(dlt.astype(BF16), w_ref[0, g], preferred_element_type=F32)
        o_ref[0, :, g * gd:(g + 1) * gd] = (y * sc_ref[:, g * gd:(g + 1) * gd]).astype(o_ref.dtype)


def _pool_mixer(z32, pool_w_stack, layer, scale_row, width):
    b, s, _ = z32.shape
    tm = min(512, s)
    hb = tm // POOL_HALO
    return pl.pallas_call(
        _pool_kernel,
        grid=(b, s // tm),
        in_specs=[pl.BlockSpec((1, tm, width), lambda bi, i: (bi, i, 0)),
                  pl.BlockSpec((1, POOL_HALO, width), lambda bi, i: (bi, jnp.maximum(i * hb - 1, 0), 0)),
                  pl.BlockSpec((1, N_POOL_GROUPS, LANES, LANES), lambda bi, i: (layer, 0, 0, 0)),
                  pl.BlockSpec((1, width), lambda bi, i: (0, 0))],
        out_specs=pl.BlockSpec((1, tm, width), lambda bi, i: (bi, i, 0)),
        out_shape=jax.ShapeDtypeStruct((b, s, width), BF16),
        compiler_params=_cparams(("parallel", "parallel"), 16 * tm * width * 4),
        name="pool_mixer",
    )(z32, z32, pool_w_stack, scale_row)


def _lru_kernel(x_ref, halo_ref, y_ref, cw_ref, cb_ref, wa_ref, ba_ref, wi_ref, bi_ref, lam_ref,
                o_ref, h_ref, a_s, b_s, h_s):
    tm = x_ref.shape[1]
    i = pl.program_id(1)

    @pl.when(i == 0)
    def _():
        h_ref[...] = jnp.zeros_like(h_ref)

    halo = jnp.where(i > 0, halo_ref[0], 0.0)
    ext = jnp.concatenate([halo, x_ref[0]], axis=0)
    xc = cb_ref[...] + cw_ref[LRU_CONV_WIDTH - 1:LRU_CONV_WIDTH, :] * ext[SUBLANES:]
    for k in range(LRU_CONV_WIDTH - 1):
        shifted = pltpu.roll(ext, LRU_CONV_WIDTH - 1 - k, 0)[SUBLANES:]
        xc = xc + cw_ref[k:k + 1, :] * shifted

    lam = lam_ref[...]
    neg_softplus = -(jnp.maximum(-lam, 0.0) + jnp.log(1.0 + jnp.exp(-jnp.abs(lam))))
    gd = LANES
    n_groups = tm // SUBLANES
    sub_row = lax.broadcasted_iota(jnp.int32, (n_groups, SUBLANES, gd), 1)
    for blk in range(N_LRU_BLOCKS):
        sl = slice(blk * gd, (blk + 1) * gd)
        xb = xc[:, sl]
        xb16 = xb.astype(BF16)
        gate_r = _sigmoid(jnp.dot(xb16, wa_ref[0, blk], preferred_element_type=F32) + ba_ref[:, sl])
        gate_i = _sigmoid(jnp.dot(xb16, wi_ref[0, blk], preferred_element_type=F32) + bi_ref[:, sl])
        log_a = LRU_C * gate_r * neg_softplus[:, sl]
        a = jnp.exp(log_a)
        b = jnp.sqrt(1.0 - a * a) * (gate_i * xb)
        a = a.reshape(n_groups, SUBLANES, gd)
        b = b.reshape(n_groups, SUBLANES, gd)
        dist = 1
        while dist < SUBLANES:
            has_prev = sub_row >= dist
            b = b + a * jnp.where(has_prev, pltpu.roll(b, dist, 1), 0.0)
            a = a * jnp.where(has_prev, pltpu.roll(a, dist, 1), 1.0)
            dist *= 2
        a_s[blk] = a
        b_s[blk] = b

    def group(gi, hs):
        out = []
        for blk in range(N_LRU_BLOCKS):
            rows = b_s[blk, gi] + a_s[blk, gi] * hs[blk]
            h_s[blk, gi] = rows
            out.append(rows[SUBLANES - 1:SUBLANES, :])
        return tuple(out)

    hs = lax.fori_loop(0, n_groups, group,
                       tuple(h_ref[:, blk * gd:(blk + 1) * gd] for blk in range(N_LRU_BLOCKS)), unroll=8)

    for blk in range(N_LRU_BLOCKS):
        sl = slice(blk * gd, (blk + 1) * gd)
        h_ref[:, sl] = hs[blk]
        y = y_ref[0, :, sl]
        gelu = 0.5 * y * (1.0 + jnp.tanh(0.7978845608028654 * (y + 0.044715 * (y * y * y))))
        o_ref[0, :, sl] = (h_s[blk].reshape(tm, gd) * gelu).astype(o_ref.dtype)


def _lru_mixer(z32, layer, p, width, x_col, y_col):
    b, s, _ = z32.shape
    tm = min(512, s)
    hb = tm // SUBLANES
    row = lambda bi, i: (0, 0)
    return pl.pallas_call(
        _lru_kernel,
        grid=(b, s // tm),
        in_specs=[pl.BlockSpec((1, tm, width), lambda bi, i: (bi, i, x_col)),
                  pl.BlockSpec((1, SUBLANES, width), lambda bi, i: (bi, jnp.maximum(i * hb - 1, 0), x_col)),
                  pl.BlockSpec((1, tm, width), lambda bi, i: (bi, i, y_col)),
                  pl.BlockSpec((LRU_CONV_WIDTH, width), row),
                  pl.BlockSpec((1, width), row),
                  pl.BlockSpec((1, N_LRU_BLOCKS, LANES, LANES), lambda bi, i: (layer, 0, 0, 0)),
                  pl.BlockSpec((1, width), row),
                  pl.BlockSpec((1, N_LRU_BLOCKS, LANES, LANES), lambda bi, i: (layer, 0, 0, 0)),
                  pl.BlockSpec((1, width), row),
                  pl.BlockSpec((1, width), row)],
        out_specs=pl.BlockSpec((1, tm, width), lambda bi, i: (bi, i, 0)),
        out_shape=jax.ShapeDtypeStruct((b, s, width), BF16),
        scratch_shapes=[pltpu.VMEM((1, width), F32)]
        + [pltpu.VMEM((N_LRU_BLOCKS, tm // SUBLANES, SUBLANES, LANES), F32)] * 3,
        compiler_params=_cparams(("arbitrary", "arbitrary"), 24 * tm * width * 4),
        name="rg_lru",
    )(z32, z32, z32, p["conv_w"], p["conv_b"], p["wa"], p["ba"], p["wi"], p["bi"], p["lam"])


COL_BLOCK = 512


def _col_blocks(d):
    return [slice(c, c + COL_BLOCK) for c in range(0, d, COL_BLOCK)]


def _residual_cols(acc, cols, x_ref, gt_ref, xn_refs):
    xn = x_ref[:, cols] + gt_ref[0, :, cols] * acc
    for ref in xn_refs:
        ref[:, cols] = xn
    return jnp.sum(xn * xn, axis=-1, keepdims=True)


def _norm_rows(sumsq, xn_ref, g_ref, sc_ref, sh_ref, h_ref):
    rs = lax.rsqrt(sumsq * (1.0 / xn_ref.shape[1]) + EPS)
    gain = g_ref[...] * (1.0 + sc_ref[0])
    h_ref[...] = (xn_ref[...] * rs * gain + sh_ref[0]).astype(h_ref.dtype)


def _mix_out_kernel(yp_ref, ya_ref, yl_ref, w_ref, x_ref, gt_ref, g_ref, sc_ref, sh_ref, xo_ref, h_ref):
    kp = yp_ref.shape[1]
    ka = ya_ref.shape[1]
    sumsq = jnp.zeros((x_ref.shape[0], 1), F32)
    for cols in _col_blocks(x_ref.shape[1]):
        acc = jnp.dot(yp_ref[...], w_ref[0, 0:kp, cols], preferred_element_type=F32)
        acc = acc + jnp.dot(ya_ref[...], w_ref[0, kp:kp + ka, cols], preferred_element_type=F32)
        acc = acc + jnp.dot(yl_ref[...], w_ref[0, kp + ka:, cols], preferred_element_type=F32)
        sumsq = sumsq + _residual_cols(acc, cols, x_ref, gt_ref, (xo_ref,))
    _norm_rows(sumsq, xo_ref, g_ref, sc_ref, sh_ref, h_ref)


def _mix_out(yp, ya, yl, w_stack, layer, x2, gt, g, sc, sh, seq):
    t, d = x2.shape
    tm = min(512, seq)
    tpb = seq // tm
    per_b = lambda i: (i // tpb, 0, 0)
    lhs = lambda y: pl.BlockSpec((tm, y.shape[1]), lambda i: (i, 0))
    vmem = 2 * d * d * 2 + 2 * (tm * d * 2 + 3 * tm * d * 4) + 3 * tm * d * 4
    return pl.pallas_call(
        _mix_out_kernel,
        grid=(t // tm,),
        in_specs=[lhs(yp), lhs(ya), lhs(yl),
                  pl.BlockSpec((1, d, d), lambda i: (layer, 0, 0)),
                  pl.BlockSpec((tm, d), lambda i: (i, 0)),
                  pl.BlockSpec((1, 1, d), per_b),
                  pl.BlockSpec((1, d), lambda i: (0, 0)),
                  pl.BlockSpec((1, 1, d), per_b),
                  pl.BlockSpec((1, 1, d), per_b)],
        out_specs=[pl.BlockSpec((tm, d), lambda i: (i, 0)), pl.BlockSpec((tm, d), lambda i: (i, 0))],
        out_shape=[jax.ShapeDtypeStruct((t, d), F32), jax.ShapeDtypeStruct((t, d), BF16)],
        compiler_params=_cparams(("parallel",), vmem),
        name="mix_out_proj",
    )(yp, ya, yl, w_stack, x2, gt, g, sc, sh)


def _ffn_down_kernel(a_ref, w_ref, x_ref, gt_ref, g_ref, sc_ref, sh_ref, *out_refs):
    xn_ref, h_ref = out_refs[0], out_refs[-1]
    sumsq = jnp.zeros((x_ref.shape[0], 1), F32)
    for cols in _col_blocks(x_ref.shape[1]):
        acc = jnp.dot(a_ref[...], w_ref[:, cols], preferred_element_type=F32)
        sumsq = sumsq + _residual_cols(acc, cols, x_ref, gt_ref, (xn_ref,))
    _norm_rows(sumsq, xn_ref, g_ref, sc_ref, sh_ref, h_ref)


def _ffn_down(act, w16, x2, gt, g, sc, sh, seq, emit_x, h_dtype):
    t, d = x2.shape
    kf = act.shape[1]
    tm = min(512, seq)
    tpb = seq // tm
    per_b = lambda i: (i // tpb, 0, 0)
    row_tile = pl.BlockSpec((tm, d), lambda i: (i, 0))
    out_specs = [row_tile]
    out_shape = [jax.ShapeDtypeStruct((t, d), h_dtype)]
    if emit_x:
        out_specs = [row_tile, row_tile]
        out_shape = [jax.ShapeDtypeStruct((t, d), F32)] + out_shape
    else:
        assert h_dtype == F32
    vmem = kf * d * 2 + 2 * (tm * kf * 2 + 3 * tm * d * 4) + 2 * tm * COL_BLOCK * 4
    return pl.pallas_call(
        _ffn_down_kernel,
        grid=(t // tm,),
        in_specs=[pl.BlockSpec((tm, kf), lambda i: (i, 0)),
                  pl.BlockSpec((kf, d), lambda i: (0, 0), pipeline_mode=pl.Buffered(1)),
                  row_tile,
                  pl.BlockSpec((1, 1, d), per_b),
                  pl.BlockSpec((1, d), lambda i: (0, 0)),
                  pl.BlockSpec((1, 1, d), per_b),
                  pl.BlockSpec((1, 1, d), per_b)],
        out_specs=out_specs,
        out_shape=out_shape,
        compiler_params=_cparams(("parallel",), vmem),
        name="ffn_down_proj",
    )(act, w16, x2, gt, g, sc, sh)


def _ffn_gate_kernel(h_ref, wg_ref, wu_ref, cw_ref, cb_ref, wd_ref, o_ref, wd16_ref, carry_ref, *, tiles_per_batch):
    tm = h_ref.shape[0]
    i = pl.program_id(0)
    j = pl.program_id(1)

    @pl.when(i % tiles_per_batch == 0)
    def _():
        carry_ref[j] = jnp.zeros(carry_ref.shape[1:], F32)

    @pl.when(i == 0)
    def _():
        wd16_ref[...] = wd_ref[0].astype(BF16)

    h = h_ref[...]
    u = jnp.dot(h, wg_ref[0].astype(BF16), preferred_element_type=F32)
    ext = jnp.concatenate([carry_ref[j], u], axis=0)
    carry_ref[j] = u[tm - SUBLANES:, :]
    g = cb_ref[...] + cw_ref[FFN_CONV_WIDTH - 1:FFN_CONV_WIDTH, :] * u
    for k in range(FFN_CONV_WIDTH - 1):
        shifted = pltpu.roll(ext, FFN_CONV_WIDTH - 1 - k, 0)[SUBLANES:]
        g = g + cw_ref[k:k + 1, :] * shifted
    up = jnp.dot(h, wu_ref[0].astype(BF16), preferred_element_type=F32)
    o_ref[...] = (g * _sigmoid(g) * up).astype(o_ref.dtype)


def _ffn_gate(h, wg_stack, wu_stack, wd_stack, layer, conv_w, conv_b, seq):
    t, d = h.shape
    kf = wg_stack.shape[2]
    tm = min(1024, seq)
    tf = 512
    nj = kf // tf
    wspec = pl.BlockSpec((1, d, tf), lambda i, j: (layer, 0, j))
    slab = lambda i, j: jnp.where(i == 0, j, nj - 1)
    vmem = 2 * (tm * d * 2 + 2 * d * tf * 4 + tm * tf * 2 + tf * d * 6) + 2 * d * tf * 2 + 8 * tm * tf * 4
    return pl.pallas_call(
        functools.partial(_ffn_gate_kernel, tiles_per_batch=seq // tm),
        grid=(t // tm, nj),
        in_specs=[pl.BlockSpec((tm, d), lambda i, j: (i, 0)), wspec, wspec,
                  pl.BlockSpec((FFN_CONV_WIDTH, tf), lambda i, j: (0, j)),
                  pl.BlockSpec((1, tf), lambda i, j: (0, j)),
                  pl.BlockSpec((1, tf, d), lambda i, j: (layer, slab(i, j), 0))],
        out_specs=[pl.BlockSpec((tm, tf), lambda i, j: (i, j)),
                   pl.BlockSpec((tf, d), lambda i, j: (slab(i, j), 0))],
        out_shape=[jax.ShapeDtypeStruct((t, kf), BF16), jax.ShapeDtypeStruct((kf, d), BF16)],
        scratch_shapes=[pltpu.VMEM((nj, SUBLANES, tf), F32)],
        compiler_params=_cparams(("arbitrary", "arbitrary"), vmem),
        name="ffn_gate_up",
    )(h, wg_stack, wu_stack, conv_w, conv_b, wd_stack)


def kernel(x, c, w_ada, b_ada, g_mix, w_in, b_f, pool_w, pool_scale, lru_conv_w, lru_conv_b, lru_wa, lru_ba,
           lru_wi, lru_bi, lru_lambda, w_out, g_ffn, w_ffn_gate, w_ffn_up, ffn_conv_w, ffn_conv_b, w_ffn_down,
           final_g):
    batch, seq, d = x.shape
    depth = w_ada.shape[0]
    pool_width = pool_w.shape[1] * pool_w.shape[2]
    lru_width = lru_lambda.shape[1]
    n_heads = b_f.shape[1]
    attn_width = n_heads * HEAD_DIM
    assert w_in.shape[2] == pool_width + 3 * attn_width + n_heads + 2 * lru_width
    assert pool_width == lru_width == N_POOL_GROUPS * LANES and seq % SUBLANES == 0

    o_q = pool_width
    o_f = o_q + 3 * attn_width
    o_x = o_f + n_heads
    w_in_t = jnp.swapaxes(w_in, 1, 2)
    w_f = jnp.pad(w_in_t[:, o_f:o_x], ((0, 0), (0, LANES - n_heads), (0, 0)))
    w_proj = jnp.concatenate([w_in_t[:, o_q:o_f], w_in_t[:, :o_q], w_in_t[:, o_x:], w_f], axis=1).astype(BF16)
    qkv_width = 3 * attn_width
    pxy_width = pool_width + 2 * lru_width
    bf_pad = jnp.pad(b_f, ((0, 0), (0, LANES - n_heads)))
    qkv_scale = jnp.concatenate([jnp.full((1, attn_width), HEAD_DIM ** -0.5 * LOG2E, F32),
                                 jnp.ones((1, 2 * attn_width), F32)], axis=1)
    pool_w16 = pool_w.astype(BF16)
    lru_wa16 = lru_wa.astype(BF16)
    lru_wi16 = lru_wi.astype(BF16)
    w_out16 = w_out.astype(BF16)

    mod = _ada_mod(c, w_ada, b_ada)[:, :batch]

    def mod_chunk(layer, idx):
        return mod[layer, :, idx * d:(idx + 1) * d].reshape(batch, 1, d)

    x2 = x.reshape(batch * seq, d)
    h = _norm(x2, g_mix[0][None], mod_chunk(0, 1), mod_chunk(0, 0), seq)
    out = None
    for layer in range(depth):
        sh1, sc1, gt1, sh2, sc2, gt2 = (mod_chunk(layer, idx) for idx in range(6))
        zqkv = _matmul_heads(h, w_proj, layer, qkv_width, qkv_width // 2, qkv_scale)
        z32, zf = _matmul_wide_narrow(h, w_proj, layer, qkv_width, pxy_width, LANES)
        z32 = z32.reshape(batch, seq, pxy_width)

        fbias = _forget_bias(zf.reshape(batch, seq, LANES), bf_pad[layer][None], n_heads)
        y_attn = _attention(zqkv, fbias, batch, seq).reshape(batch * seq, attn_width)
        y_pool = _pool_mixer(z32, pool_w16, layer, pool_scale[layer][None], pool_width)
        lru_p = dict(conv_w=lru_conv_w[layer], conv_b=lru_conv_b[layer][None], wa=lru_wa16,
                     ba=lru_ba[layer][None], wi=lru_wi16, bi=lru_bi[layer][None], lam=lru_lambda[layer][None])
        y_lru = _lru_mixer(z32, layer, lru_p, lru_width, 1, 2)

        x2, h = _mix_out(y_pool.reshape(batch * seq, pool_width), y_attn, y_lru.reshape(batch * seq, lru_width),
                         w_out16, layer, x2, gt1, g_ffn[layer][None], sc2, sh2, seq)
        act, w_down16 = _ffn_gate(h, w_ffn_gate, w_ffn_up, w_ffn_down, layer, ffn_conv_w[layer],
                                  ffn_conv_b[layer][None], seq)
        if layer + 1 < depth:
            x2, h = _ffn_down(act, w_down16, x2, gt2, g_mix[layer + 1][None], mod_chunk(layer + 1, 1),
                              mod_chunk(layer + 1, 0), seq, True, BF16)
        else:
            zeros = jnp.zeros((batch, 1, d), F32)
            (out,) = _ffn_down(act, w_down16, x2, gt2, final_g[None], zeros, zeros, seq, False, F32)
    return out.reshape(batch, seq, d)
```

```python
import functools

import jax
import jax.numpy as jnp
from jax import lax
from jax.experimental import pallas as pl
from jax.experimental.pallas import tpu as pltpu

F32 = jnp.float32
BF16 = jnp.bfloat16

HEAD_DIM = 128
N_POOL_GROUPS = 4
POOL_WINDOWS = (2, 4, 8, 16)
POOL_HALO = 16
N_LRU_BLOCKS = 4
LRU_CONV_WIDTH = 4
LRU_C = 8.0
FFN_CONV_WIDTH = 3
EPS = 1e-6

LANES = 128
SUBLANES = 8
VMEM_LIMIT_CAP_V7X = 58 * 2**20
MASK_VALUE = -1e30
LOG2E = 1.4426950408889634


def _cparams(semantics, vmem_bytes):
    return pltpu.CompilerParams(dimension_semantics=semantics,
                                vmem_limit_bytes=int(min(VMEM_LIMIT_CAP_V7X, max(vmem_bytes, 16 * 2**20))))


def _sigmoid(x):
    return 0.5 * jnp.tanh(0.5 * x) + 0.5


def _norm_mod(x, g, sc, sh):
    ms = jnp.mean(x * x, axis=-1, keepdims=True)
    return (x * lax.rsqrt(ms + EPS) * g) * (1.0 + sc) + sh


def _split3_bf16(x):
    hi = x.astype(BF16)
    r1 = x - hi.astype(F32)
    mid = r1.astype(BF16)
    lo = (r1 - mid.astype(F32)).astype(BF16)
    return hi, mid, lo


def _ada_kernel(c_ref, w_ref, b_ref, o_ref, acc_ref):
    nb = c_ref.shape[0]
    n = w_ref.shape[3]
    k = pl.program_id(1)

    @pl.when(k == 0)
    def _():
        acc_ref[...] = jnp.zeros_like(acc_ref)

    ca = [c_ref[b] * _sigmoid(c_ref[b]) for b in range(nb)]

    def lane_chunk(jc, carry):
        sl = pl.ds(pl.multiple_of(jc * LANES, LANES), LANES)
        w = w_ref[0, :, :, sl]
        for b in range(nb):
            acc_ref[b, :, sl] += jnp.sum(w * ca[b], axis=0)
        return carry

    lax.fori_loop(0, n // LANES, lane_chunk, 0, unroll=4)

    @pl.when(k == pl.num_programs(1) - 1)
    def _():
        o_ref[...] = jnp.zeros_like(o_ref)
        for b in range(nb):
            o_ref[0, b:b + 1, :] = jnp.sum(acc_ref[b], axis=0, keepdims=True) + b_ref[0]


ADA_SLAB_ROWS = 256


def _ada_mod(c, w_ada, b_ada):
    depth, d, n = w_ada.shape
    batch = c.shape[0]
    dr = d // SUBLANES
    sr = ADA_SLAB_ROWS // SUBLANES
    c_lanes = jnp.broadcast_to(c[:, :, None], (batch, d, LANES)).reshape(batch, dr, SUBLANES, LANES)
    return pl.pallas_call(
        _ada_kernel,
        grid=(depth, dr // sr),
        in_specs=[pl.BlockSpec((batch, sr, SUBLANES, LANES), lambda l, k: (0, k, 0, 0)),
                  pl.BlockSpec((1, sr, SUBLANES, n), lambda l, k: (l, k, 0, 0)),
                  pl.BlockSpec((1, 1, n), lambda l, k: (l, 0, 0))],
        out_specs=pl.BlockSpec((1, SUBLANES, n), lambda l, k: (l, 0, 0)),
        out_shape=jax.ShapeDtypeStruct((depth, SUBLANES, n), F32),
        scratch_shapes=[pltpu.VMEM((batch, SUBLANES, n), F32)],
        compiler_params=_cparams(("parallel", "arbitrary"), 3 * ADA_SLAB_ROWS * n * 4),
        name="ada_mod",
    )(c_lanes, w_ada.reshape(depth, dr, SUBLANES, n), b_ada.reshape(depth, 1, n))


def _norm_kernel(x_ref, g_ref, sc_ref, sh_ref, h_ref):
    h_ref[...] = _norm_mod(x_ref[...], g_ref[...], sc_ref[0], sh_ref[0]).astype(h_ref.dtype)


def _norm(x2, g, sc, sh, seq):
    t, d = x2.shape
    tm = min(512, seq)
    tpb = seq // tm
    return pl.pallas_call(
        _norm_kernel,
        grid=(t // tm,),
        in_specs=[pl.BlockSpec((tm, d), lambda i: (i, 0)),
                  pl.BlockSpec((1, d), lambda i: (0, 0)),
                  pl.BlockSpec((1, 1, d), lambda i: (i // tpb, 0, 0)),
                  pl.BlockSpec((1, 1, d), lambda i: (i // tpb, 0, 0))],
        out_specs=pl.BlockSpec((tm, d), lambda i: (i, 0)),
        out_shape=jax.ShapeDtypeStruct((t, d), BF16),
        compiler_params=_cparams(("parallel",), 6 * tm * d * 4),
        name="norm_mod",
    )(x2, g, sc, sh)


_NT_DIMS = (((1,), (1,)), ((), ()))


def _mm_windows_kernel(a_ref, *refs):
    n_w = len(refs) - 2
    w_refs, o_ref, on_ref = refs[:n_w], refs[n_w], refs[n_w + 1]
    a = a_ref[...]
    col = 0
    for w_ref in w_refs[:-1]:
        width = w_ref.shape[1]
        o_ref[:, col:col + width] = lax.dot_general(a, w_ref[0].astype(BF16), _NT_DIMS, preferred_element_type=F32)
        col += width
    on_ref[...] = lax.dot_general(a, w_refs[-1][0].astype(BF16), _NT_DIMS, preferred_element_type=F32)


def _mm_heads_kernel(a_ref, w_ref, cs_ref, o_ref):
    acc = lax.dot_general(a_ref[...], w_ref[0], _NT_DIMS, preferred_element_type=F32) * cs_ref[...]
    for hh in range(o_ref.shape[0]):
        o_ref[hh] = acc[:, hh * HEAD_DIM:(hh + 1) * HEAD_DIM].astype(o_ref.dtype)


def _matmul_windows(a, wt_stack, layer, windows):
    t, k = a.shape
    tm = min(1024, t)
    wide = sum(n for _, n in windows[:-1])
    narrow = windows[-1][1]
    rows = wide + narrow
    vmem = rows * k * 6 + 2 * (tm * k * 2 + tm * rows * 4) + 2 * tm * wide * 4
    w_specs = [pl.BlockSpec((pl.Element(1), pl.Element(n), pl.Element(k)), lambda i, r=r: (layer, r, 0),
                            pipeline_mode=pl.Buffered(1))
               for r, n in windows]
    return pl.pallas_call(
        _mm_windows_kernel,
        grid=(t // tm,),
        in_specs=[pl.BlockSpec((tm, k), lambda i: (i, 0))] + w_specs,
        out_specs=[pl.BlockSpec((tm, wide), lambda i: (i, 0)), pl.BlockSpec((tm, narrow), lambda i: (i, 0))],
        out_shape=[jax.ShapeDtypeStruct((t, wide), F32), jax.ShapeDtypeStruct((t, narrow), F32)],
        compiler_params=_cparams(("parallel",), vmem),
        name="matmul_dmodel",
    )(a, *([wt_stack] * len(windows)))


def _matmul_heads(a, wt_stack, layer, row0, n, tn, col_scale):
    t, k = a.shape
    tm = min(1024, t)
    hpb = tn // HEAD_DIM
    vmem = 2 * (tm * k * 2 + k * tn * 2 + tm * tn * 2) + 2 * tm * tn * 4
    return pl.pallas_call(
        _mm_heads_kernel,
        grid=(t // tm, n // tn),
        in_specs=[pl.BlockSpec((tm, k), lambda i, j: (i, 0)),
                  pl.BlockSpec((pl.Element(1), pl.Element(tn), pl.Element(k)),
                               lambda i, j: (layer, pl.multiple_of(row0 + j * tn, HEAD_DIM), 0)),
                  pl.BlockSpec((1, tn), lambda i, j: (0, j))],
        out_specs=pl.BlockSpec((hpb, tm, HEAD_DIM), lambda i, j: (j, i, 0)),
        out_shape=jax.ShapeDtypeStruct((n // HEAD_DIM, t, HEAD_DIM), BF16),
        compiler_params=_cparams(("parallel", "parallel"), vmem),
        name="matmul_heads",
    )(a, wt_stack, col_scale)


N_SPLIT = 3


def _fcum_kernel(z_ref, b_ref, o_ref, carry_ref, tri_ref, place_ref):
    ts = z_ref.shape[1]
    w = z_ref.shape[2]
    wo = place_ref.shape[1]

    @pl.when((pl.program_id(0) == 0) & (pl.program_id(1) == 0))
    def _():
        r = lax.broadcasted_iota(jnp.int32, (ts, ts), 0)
        c = lax.broadcasted_iota(jnp.int32, (ts, ts), 1)
        tri_ref[...] = jnp.where(c <= r, 1.0, 0.0).astype(BF16)
        row = lax.broadcasted_iota(jnp.int32, (N_SPLIT * w, wo), 0)
        lane = lax.broadcasted_iota(jnp.int32, (N_SPLIT * w, wo), 1)
        target = jnp.zeros_like(row)
        for idx in range(N_SPLIT):
            in_part = (row >= idx * w) & (row < (idx + 1) * w)
            target = jnp.where(in_part, (row - idx * w) * HEAD_DIM + idx, target)
        place_ref[...] = jnp.where(lane == target, 1.0, 0.0).astype(BF16)

    @pl.when(pl.program_id(1) == 0)
    def _():
        carry_ref[...] = jnp.zeros_like(carry_ref)

    x = z_ref[0] + b_ref[...]
    lf = jnp.minimum(x, 0.0) - jnp.log(1.0 + jnp.exp(-jnp.abs(x)))
    parts = jnp.dot(tri_ref[...], jnp.concatenate(_split3_bf16(lf), axis=1), preferred_element_type=F32)
    cs = carry_ref[...] + parts[:, :w]
    for idx in range(1, N_SPLIT):
        cs = cs + parts[:, idx * w:(idx + 1) * w]
    carry_ref[...] = cs[ts - 1:ts, :]

    split = jnp.concatenate(_split3_bf16(cs * LOG2E), axis=1)
    bias = jnp.dot(split, place_ref[...], preferred_element_type=F32)
    for hh in range(o_ref.shape[0]):
        o_ref[hh, 0] = bias[:, hh * HEAD_DIM:(hh + 1) * HEAD_DIM].astype(o_ref.dtype)


def _forget_bias(zf, bf_pad, n_heads):
    b, s, w = zf.shape
    ts = min(512, s)
    wo = n_heads * HEAD_DIM
    return pl.pallas_call(
        _fcum_kernel,
        grid=(b, s // ts),
        in_specs=[pl.BlockSpec((1, ts, w), lambda bi, i: (bi, i, 0)),
                  pl.BlockSpec((1, w), lambda bi, i: (0, 0))],
        out_specs=pl.BlockSpec((n_heads, 1, ts, HEAD_DIM), lambda bi, i: (0, bi, i, 0)),
        out_shape=jax.ShapeDtypeStruct((n_heads, b, s, HEAD_DIM), BF16),
        scratch_shapes=[pltpu.VMEM((1, w), F32), pltpu.VMEM((ts, ts), BF16), pltpu.VMEM((N_SPLIT * w, wo), BF16)],
        compiler_params=_cparams(("arbitrary", "arbitrary"), 4 * ts * ts * 4 + 8 * ts * wo * 4),
        name="forget_bias",
    )(zf, bf_pad)


ONES_ROWS = 16


MAX_CHAINS = 8
BIG_CHUNK = 4
STALE_MAX_GUARD = 60.0


def _attn_kernel(q_ref, k_ref, v_ref, fb_ref, o_ref, vt_ref, acc_ref, m_ref, *, tq):
    tk = tq
    s_len = k_ref.shape[1]
    step = pl.program_id(2)

    @pl.when(step == 0)
    def _():
        vt_ref[HEAD_DIM:, :] = jnp.ones((ONES_ROWS, s_len), BF16)
        for c in range(s_len // tk):
            vt_ref[:HEAD_DIM, c * tk:(c + 1) * tk] = v_ref[0, c * tk:(c + 1) * tk, :].astype(F32).T.astype(BF16)

    for sub in range(q_ref.shape[1] // tq):
        rows = slice(sub * tq, (sub + 1) * tq)
        o_ref[0, rows, :] = _attn_query_tile(step * (q_ref.shape[1] // tq) + sub, q_ref[0, rows, :], k_ref, fb_ref,
                                             vt_ref, acc_ref, m_ref).astype(o_ref.dtype)


def _attn_query_tile(qi, q, k_ref, fb_ref, vt_ref, acc_ref, m_ref):
    tq = q.shape[0]
    tk = tq
    lane = lax.broadcasted_iota(jnp.int32, (tq, HEAD_DIM), 1)
    minus_one = jnp.where(lane < 3, -1.0, 0.0).astype(BF16)
    q_aug = jnp.concatenate([q, minus_one], axis=1)
    m_ref[...] = jnp.full(m_ref.shape, MASK_VALUE, F32)
    acc_ref[...] = jnp.zeros_like(acc_ref)

    def logits(k0, size):
        k0 = pl.multiple_of(k0, tk)
        k_aug = jnp.concatenate([k_ref[0, pl.ds(k0, size), :], fb_ref[0, 0, pl.ds(k0, size), :]], axis=1)
        return lax.dot_general(k_aug, q_aug, (((1,), (1,)), ((), ())), preferred_element_type=F32)

    def absorb(st, k0, size, masked):
        k0 = pl.multiple_of(k0, tk)
        if masked:
            krow = lax.broadcasted_iota(jnp.int32, (size, tq), 0)
            qcol = lax.broadcasted_iota(jnp.int32, (size, tq), 1)
            st = jnp.where(krow <= qcol, st, MASK_VALUE)
        part = jnp.max(st.reshape(MAX_CHAINS, size // MAX_CHAINS, tq), axis=1)
        m_old = m_ref[...]
        m_new = jnp.maximum(m_old, jnp.max(part, axis=0, keepdims=True))
        m_ref[...] = m_new
        pt = jnp.exp2((st - m_new).astype(BF16))
        acc_ref[...] = jnp.exp2(m_old - m_new) * acc_ref[...] + jnp.dot(
            vt_ref[:, pl.ds(k0, size)], pt, preferred_element_type=F32)

    def absorb_one_pass(st, k0, size):
        k0 = pl.multiple_of(k0, tk)
        m_old = m_ref[...]
        part = jnp.max(st.reshape(MAX_CHAINS, size // MAX_CHAINS, tq), axis=1)
        cmax = jnp.max(part, axis=0, keepdims=True)
        pt = jnp.exp2(st - m_old).astype(BF16)
        pv = jnp.dot(vt_ref[:, pl.ds(k0, size)], pt, preferred_element_type=F32)
        safe = jnp.max(cmax - m_old) <= STALE_MAX_GUARD

        @pl.when(safe)
        def _():
            m_new = jnp.maximum(m_old, cmax)
            m_ref[...] = m_new
            acc_ref[...] = (acc_ref[...] + pv) * jnp.exp2(m_old - m_new)

        @pl.when(jnp.logical_not(safe))
        def _():
            absorb(logits(k0, size), k0, size, False)

    n_big = qi // BIG_CHUNK
    tail_k0 = n_big * (BIG_CHUNK * tk)
    for visible in range(BIG_CHUNK):
        @pl.when(qi - n_big * BIG_CHUNK == visible)
        def _(visible=visible):
            diag_k0 = tail_k0 + visible * tk
            if visible == 0:
                absorb(logits(diag_k0, tk), diag_k0, tk, True)
            else:
                st_a = logits(tail_k0, visible * tk)
                st_b = logits(diag_k0, tk)
                absorb(st_a, tail_k0, visible * tk, False)
                absorb(st_b, diag_k0, tk, True)

    def body(kc, carry):
        k0 = kc * (BIG_CHUNK * tk)
        absorb_one_pass(logits(k0, BIG_CHUNK * tk), k0, BIG_CHUNK * tk)
        return carry

    lax.fori_loop(0, n_big, body, 0)
    o_t = acc_ref[:HEAD_DIM, :] * (1.0 / acc_ref[HEAD_DIM:HEAD_DIM + 1, :])
    return o_t.T


def _attention(zqkv, fbias, batch, seq):
    n_heads = fbias.shape[0]
    b, s = batch, seq
    tq = min(512, s)
    tiles_per_step = 2 if s % (2 * tq) == 0 else 1
    tqs = tq * tiles_per_step
    spb = s // tqs
    assert s % tqs == 0
    vmem = 2 * (3 * s * HEAD_DIM * 2) + (HEAD_DIM + ONES_ROWS) * s * 2 + 6 * BIG_CHUNK * tq * tq * 4
    kv_spec = lambda off: pl.BlockSpec((1, s, HEAD_DIM), lambda bi, h, i: (off + h, bi, 0))
    return pl.pallas_call(
        functools.partial(_attn_kernel, tq=tq),
        grid=(b, n_heads, spb),
        in_specs=[pl.BlockSpec((1, tqs, HEAD_DIM), lambda bi, h, i: (h, bi * spb + i, 0)),
                  kv_spec(n_heads), kv_spec(2 * n_heads),
                  pl.BlockSpec((1, 1, s, HEAD_DIM), lambda bi, h, i: (h, bi, 0, 0))],
        out_specs=pl.BlockSpec((1, tqs, HEAD_DIM), lambda bi, h, i: (bi, i, h)),
        out_shape=jax.ShapeDtypeStruct((b, s, n_heads * HEAD_DIM), BF16),
        scratch_shapes=[pltpu.VMEM((HEAD_DIM + ONES_ROWS, s), BF16),
                        pltpu.VMEM((HEAD_DIM + ONES_ROWS, tq), F32),
                        pltpu.VMEM((1, tq), F32)],
        compiler_params=_cparams(("parallel", "parallel", "arbitrary"), vmem),
        name="fox_attention",
    )(zqkv, zqkv, zqkv, fbias)


def _pool_kernel(u_ref, halo_ref, w_ref, sc_ref, o_ref):
    tm = u_ref.shape[1]
    i = pl.program_id(1)
    halo = jnp.where(i > 0, halo_ref[0], 0.0)
    ext = jnp.concatenate([halo, u_ref[0]], axis=0)
    pos = (i * tm + 1 + lax.broadcasted_iota(jnp.int32, (tm, 1), 0)).astype(F32)
    gd = LANES
    for g, win in enumerate(POOL_WINDOWS):
        e = ext[:, g * gd:(g + 1) * gd]
        ssum = e
        shift = 1
        while shift < win:
            ssum = ssum + pltpu.roll(ssum, shift, 0)
            shift *= 2
        mean = ssum[POOL_HALO:] * (1.0 / jnp.minimum(pos, float(win)))
        dlt = mean - e[POOL_HALO:]
        y = jnp.dot(dlt.astype(BF16), w_ref[0, g], preferred_element_type=F32)
        o_ref[0, :, g * gd:(g + 1) * gd] = (y * sc_ref[:, g * gd:(g + 1) * gd]).astype(o_ref.dtype)


def _pool_mixer(z32, pool_w_stack, layer, scale_row, width):
    b, s, _ = z32.shape
    tm = min(512, s)
    hb = tm // POOL_HALO
    return pl.pallas_call(
        _pool_kernel,
        grid=(b, s // tm),
        in_specs=[pl.BlockSpec((1, tm, width), lambda bi, i: (bi, i, 0)),
                  pl.BlockSpec((1, POOL_HALO, width), lambda bi, i: (bi, jnp.maximum(i * hb - 1, 0), 0)),
                  pl.BlockSpec((1, N_POOL_GROUPS, LANES, LANES), lambda bi, i: (layer, 0, 0, 0)),
                  pl.BlockSpec((1, width), lambda bi, i: (0, 0))],
        out_specs=pl.BlockSpec((1, tm, width), lambda bi, i: (bi, i, 0)),
        out_shape=jax.ShapeDtypeStruct((b, s, width), BF16),
        compiler_params=_cparams(("parallel", "parallel"), 16 * tm * width * 4),
        name="pool_mixer",
    )(z32, z32, pool_w_stack, scale_row)


def _lru_kernel(x_ref, halo_ref, y_ref, cw_ref, cb_ref, wa_ref, ba_ref, wi_ref, bi_ref, lam_ref,
                o_ref, h_ref, a_s, b_s, h_s):
    tm = x_ref.shape[1]
    i = pl.program_id(1)

    @pl.when(i == 0)
    def _():
        h_ref[...] = jnp.zeros_like(h_ref)

    halo = jnp.where(i > 0, halo_ref[0], 0.0)
    ext = jnp.concatenate([halo, x_ref[0]], axis=0)
    xc = cb_ref[...] + cw_ref[LRU_CONV_WIDTH - 1:LRU_CONV_WIDTH, :] * ext[SUBLANES:]
    for k in range(LRU_CONV_WIDTH - 1):
        shifted = pltpu.roll(ext, LRU_CONV_WIDTH - 1 - k, 0)[SUBLANES:]
        xc = xc + cw_ref[k:k + 1, :] * shifted

    lam = lam_ref[...]
    neg_softplus = -(jnp.maximum(-lam, 0.0) + jnp.log(1.0 + jnp.exp(-jnp.abs(lam))))
    gd = LANES
    n_groups = tm // SUBLANES
    sub_row = lax.broadcasted_iota(jnp.int32, (n_groups, SUBLANES, gd), 1)
    for blk in range(N_LRU_BLOCKS):
        sl = slice(blk * gd, (blk + 1) * gd)
        xb = xc[:, sl]
        xb16 = xb.astype(BF16)
        gate_r = _sigmoid(jnp.dot(xb16, wa_ref[0, blk], preferred_element_type=F32) + ba_ref[:, sl])
        gate_i = _sigmoid(jnp.dot(xb16, wi_ref[0, blk], preferred_element_type=F32) + bi_ref[:, sl])
        log_a = LRU_C * gate_r * neg_softplus[:, sl]
        a = jnp.exp(log_a)
        b = jnp.sqrt(1.0 - a * a) * (gate_i * xb)
        a = a.reshape(n_groups, SUBLANES, gd)
        b = b.reshape(n_groups, SUBLANES, gd)
        dist = 1
        while dist < SUBLANES:
            has_prev = sub_row >= dist
            b = b + a * jnp.where(has_prev, pltpu.roll(b, dist, 1), 0.0)
            a = a * jnp.where(has_prev, pltpu.roll(a, dist, 1), 1.0)
            dist *= 2
        a_s[blk] = a
        b_s[blk] = b

    def group(gi, hs):
        out = []
        for blk in range(N_LRU_BLOCKS):
            rows = b_s[blk, gi] + a_s[blk, gi] * hs[blk]
            h_s[blk, gi] = rows
            out.append(rows[SUBLANES - 1:SUBLANES, :])
        return tuple(out)

    hs = lax.fori_loop(0, n_groups, group,
                       tuple(h_ref[:, blk * gd:(blk + 1) * gd] for blk in range(N_LRU_BLOCKS)), unroll=8)

    for blk in range(N_LRU_BLOCKS):
        sl = slice(blk * gd, (blk + 1) * gd)
        h_ref[:, sl] = hs[blk]
        y = y_ref[0, :, sl]
        gelu = 0.5 * y * (1.0 + jnp.tanh(0.7978845608028654 * (y + 0.044715 * (y * y * y))))
        o_ref[0, :, sl] = (h_s[blk].reshape(tm, gd) * gelu).astype(o_ref.dtype)


def _lru_mixer(z32, layer, p, width, x_col, y_col):
    b, s, _ = z32.shape
    tm = min(512, s)
    hb = tm // SUBLANES
    row = lambda bi, i: (0, 0)
    return pl.pallas_call(
        _lru_kernel,
        grid=(b, s // tm),
        in_specs=[pl.BlockSpec((1, tm, width), lambda bi, i: (bi, i, x_col)),
                  pl.BlockSpec((1, SUBLANES, width), lambda bi, i: (bi, jnp.maximum(i * hb - 1, 0), x_col)),
                  pl.BlockSpec((1, tm, width), lambda bi, i: (bi, i, y_col)),
                  pl.BlockSpec((LRU_CONV_WIDTH, width), row),
                  pl.BlockSpec((1, width), row),
                  pl.BlockSpec((1, N_LRU_BLOCKS, LANES, LANES), lambda bi, i: (layer, 0, 0, 0)),
                  pl.BlockSpec((1, width), row),
                  pl.BlockSpec((1, N_LRU_BLOCKS, LANES, LANES), lambda bi, i: (layer, 0, 0, 0)),
                  pl.BlockSpec((1, width), row),
                  pl.BlockSpec((1, width), row)],
        out_specs=pl.BlockSpec((1, tm, width), lambda bi, i: (bi, i, 0)),
        out_shape=jax.ShapeDtypeStruct((b, s, width), BF16),
        scratch_shapes=[pltpu.VMEM((1, width), F32)]
        + [pltpu.VMEM((N_LRU_BLOCKS, tm // SUBLANES, SUBLANES, LANES), F32)] * 3,
        compiler_params=_cparams(("arbitrary", "arbitrary"), 24 * tm * width * 4),
        name="rg_lru",
    )(z32, z32, z32, p["conv_w"], p["conv_b"], p["wa"], p["ba"], p["wi"], p["bi"], p["lam"])


COL_BLOCK = 512


def _col_blocks(d):
    return [slice(c, c + COL_BLOCK) for c in range(0, d, COL_BLOCK)]


def _residual_cols(acc, cols, x_ref, gt_ref, xn_refs):
    xn = x_ref[:, cols] + gt_ref[0, :, cols] * acc
    for ref in xn_refs:
        ref[:, cols] = xn
    return jnp.sum(xn * xn, axis=-1, keepdims=True)


def _norm_rows(sumsq, xn_ref, g_ref, sc_ref, sh_ref, h_ref):
    rs = lax.rsqrt(sumsq * (1.0 / xn_ref.shape[1]) + EPS)
    gain = g_ref[...] * (1.0 + sc_ref[0])
    h_ref[...] = (xn_ref[...] * rs * gain + sh_ref[0]).astype(h_ref.dtype)


def _mix_out_kernel(yp_ref, ya_ref, yl_ref, w_ref, x_ref, gt_ref, g_ref, sc_ref, sh_ref, xo_ref, h_ref):
    kp = yp_ref.shape[1]
    ka = ya_ref.shape[1]
    sumsq = jnp.zeros((x_ref.shape[0], 1), F32)
    for cols in _col_blocks(x_ref.shape[1]):
        acc = jnp.dot(yp_ref[...], w_ref[0, 0:kp, cols], preferred_element_type=F32)
        acc = acc + jnp.dot(ya_ref[...], w_ref[0, kp:kp + ka, cols], preferred_element_type=F32)
        acc = acc + jnp.dot(yl_ref[...], w_ref[0, kp + ka:, cols], preferred_element_type=F32)
        sumsq = sumsq + _residual_cols(acc, cols, x_ref, gt_ref, (xo_ref,))
    _norm_rows(sumsq, xo_ref, g_ref, sc_ref, sh_ref, h_ref)


def _mix_out(yp, ya, yl, w_stack, layer, x2, gt, g, sc, sh, seq):
    t, d = x2.shape
    tm = min(512, seq)
    tpb = seq // tm
    per_b = lambda i: (i // tpb, 0, 0)
    lhs = lambda y: pl.BlockSpec((tm, y.shape[1]), lambda i: (i, 0))
    vmem = 2 * d * d * 2 + 2 * (tm * d * 2 + 3 * tm * d * 4) + 3 * tm * d * 4
    return pl.pallas_call(
        _mix_out_kernel,
        grid=(t // tm,),
        in_specs=[lhs(yp), lhs(ya), lhs(yl),
                  pl.BlockSpec((1, d, d), lambda i: (layer, 0, 0)),
                  pl.BlockSpec((tm, d), lambda i: (i, 0)),
                  pl.BlockSpec((1, 1, d), per_b),
                  pl.BlockSpec((1, d), lambda i: (0, 0)),
                  pl.BlockSpec((1, 1, d), per_b),
                  pl.BlockSpec((1, 1, d), per_b)],
        out_specs=[pl.BlockSpec((tm, d), lambda i: (i, 0)), pl.BlockSpec((tm, d), lambda i: (i, 0))],
        out_shape=[jax.ShapeDtypeStruct((t, d), F32), jax.ShapeDtypeStruct((t, d), BF16)],
        compiler_params=_cparams(("parallel",), vmem),
        name="mix_out_proj",
    )(yp, ya, yl, w_stack, x2, gt, g, sc, sh)


def _ffn_down_kernel(a_ref, w_ref, x_ref, gt_ref, g_ref, sc_ref, sh_ref, *out_refs):
    xn_ref, h_ref = out_refs[0], out_refs[-1]
    sumsq = jnp.zeros((x_ref.shape[0], 1), F32)
    for cols in _col_blocks(x_ref.shape[1]):
        acc = jnp.dot(a_ref[...], w_ref[:, cols], preferred_element_type=F32)
        sumsq = sumsq + _residual_cols(acc, cols, x_ref, gt_ref, (xn_ref,))
    _norm_rows(sumsq, xn_ref, g_ref, sc_ref, sh_ref, h_ref)


def _ffn_down(act, w16, x2, gt, g, sc, sh, seq, emit_x, h_dtype):
    t, d = x2.shape
    kf = act.shape[1]
    tm = min(512, seq)
    tpb = seq // tm
    per_b = lambda i: (i // tpb, 0, 0)
    row_tile = pl.BlockSpec((tm, d), lambda i: (i, 0))
    out_specs = [row_tile]
    out_shape = [jax.ShapeDtypeStruct((t, d), h_dtype)]
    if emit_x:
        out_specs = [row_tile, row_tile]
        out_shape = [jax.ShapeDtypeStruct((t, d), F32)] + out_shape
    else:
        assert h_dtype == F32
    vmem = kf * d * 2 + 2 * (tm * kf * 2 + 3 * tm * d * 4) + 2 * tm * COL_BLOCK * 4
    return pl.pallas_call(
        _ffn_down_kernel,
        grid=(t // tm,),
        in_specs=[pl.BlockSpec((tm, kf), lambda i: (i, 0)),
                  pl.BlockSpec((kf, d), lambda i: (0, 0), pipeline_mode=pl.Buffered(1)),
                  row_tile,
                  pl.BlockSpec((1, 1, d), per_b),
                  pl.BlockSpec((1, d), lambda i: (0, 0)),
                  pl.BlockSpec((1, 1, d), per_b),
                  pl.BlockSpec((1, 1, d), per_b)],
        out_specs=out_specs,
        out_shape=out_shape,
        compiler_params=_cparams(("parallel",), vmem),
        name="ffn_down_proj",
    )(act, w16, x2, gt, g, sc, sh)


def _ffn_gate_kernel(h_ref, wg_ref, wu_ref, cw_ref, cb_ref, wd_ref, o_ref, wd16_ref, carry_ref, *, tiles_per_batch):
    tm = h_ref.shape[0]
    i = pl.program_id(0)
    j = pl.program_id(1)

    @pl.when(i % tiles_per_batch == 0)
    def _():
        carry_ref[j] = jnp.zeros(carry_ref.shape[1:], F32)

    @pl.when(i == 0)
    def _():
        wd16_ref[...] = wd_ref[0].astype(BF16)

    h = h_ref[...]
    u = jnp.dot(h, wg_ref[0].astype(BF16), preferred_element_type=F32)
    ext = jnp.concatenate([carry_ref[j], u], axis=0)
    carry_ref[j] = u[tm - SUBLANES:, :]
    g = cb_ref[...] + cw_ref[FFN_CONV_WIDTH - 1:FFN_CONV_WIDTH, :] * u
    for k in range(FFN_CONV_WIDTH - 1):
        shifted = pltpu.roll(ext, FFN_CONV_WIDTH - 1 - k, 0)[SUBLANES:]
        g = g + cw_ref[k:k + 1, :] * shifted
    up = jnp.dot(h, wu_ref[0].astype(BF16), preferred_element_type=F32)
    o_ref[...] = (g * _sigmoid(g) * up).astype(o_ref.dtype)


def _ffn_gate(h, wg_stack, wu_stack, wd_stack, layer, conv_w, conv_b, seq):
    t, d = h.shape
    kf = wg_stack.shape[2]
    tm = min(1024, seq)
    tf = 512
    nj = kf // tf
    wspec = pl.BlockSpec((1, d, tf), lambda i, j: (layer, 0, j))
    slab = lambda i, j: jnp.where(i == 0, j, nj - 1)
    vmem = 2 * (tm * d * 2 + 2 * d * tf * 4 + tm * tf * 2 + tf * d * 6) + 2 * d * tf * 2 + 8 * tm * tf * 4
    return pl.pallas_call(
        functools.partial(_ffn_gate_kernel, tiles_per_batch=seq // tm),
        grid=(t // tm, nj),
        in_specs=[pl.BlockSpec((tm, d), lambda i, j: (i, 0)), wspec, wspec,
                  pl.BlockSpec((FFN_CONV_WIDTH, tf), lambda i, j: (0, j)),
                  pl.BlockSpec((1, tf), lambda i, j: (0, j)),
                  pl.BlockSpec((1, tf, d), lambda i, j: (layer, slab(i, j), 0))],
        out_specs=[pl.BlockSpec((tm, tf), lambda i, j: (i, j)),
                   pl.BlockSpec((tf, d), lambda i, j: (slab(i, j), 0))],
        out_shape=[jax.ShapeDtypeStruct((t, kf), BF16), jax.ShapeDtypeStruct((kf, d), BF16)],
        scratch_shapes=[pltpu.VMEM((nj, SUBLANES, tf), F32)],
        compiler_params=_cparams(("arbitrary", "arbitrary"), vmem),
        name="ffn_gate_up",
    )(h, wg_stack, wu_stack, conv_w, conv_b, wd_stack)


def kernel(x, c, w_ada, b_ada, g_mix, w_in, b_f, pool_w, pool_scale, lru_conv_w, lru_conv_b, lru_wa, lru_ba,
           lru_wi, lru_bi, lru_lambda, w_out, g_ffn, w_ffn_gate, w_ffn_up, ffn_conv_w, ffn_conv_b, w_ffn_down,
           final_g):
    batch, seq, d = x.shape
    depth = w_ada.shape[0]
    pool_width = pool_w.shape[1] * pool_w.shape[2]
    lru_width = lru_lambda.shape[1]
    n_heads = b_f.shape[1]
    attn_width = n_heads * HEAD_DIM
    assert w_in.shape[2] == pool_width + 3 * attn_width + n_heads + 2 * lru_width
    assert pool_width == lru_width == N_POOL_GROUPS * LANES and seq % SUBLANES == 0

    o_q = pool_width
    o_f = o_q + 3 * attn_width
    o_x = o_f + n_heads
    w_in_t = jnp.swapaxes(w_in, 1, 2)
    w_in_t16 = w_in_t.astype(BF16)
    qkv_width = 3 * attn_width
    pxy_width = pool_width + 2 * lru_width
    proj_windows = [(0, pool_width), (o_x, 2 * lru_width), (o_f, LANES)]
    bf_pad = jnp.pad(b_f, ((0, 0), (0, LANES - n_heads)))
    qkv_scale = jnp.concatenate([jnp.full((1, attn_width), HEAD_DIM ** -0.5 * LOG2E, F32),
                                 jnp.ones((1, 2 * attn_width), F32)], axis=1)
    pool_w16 = pool_w.astype(BF16)
    lru_wa16 = lru_wa.astype(BF16)
    lru_wi16 = lru_wi.astype(BF16)
    w_out16 = w_out.astype(BF16)

    mod = _ada_mod(c, w_ada, b_ada)[:, :batch]

    def mod_chunk(layer, idx):
        return mod[layer, :, idx * d:(idx + 1) * d].reshape(batch, 1, d)

    x2 = x.reshape(batch * seq, d)
    h = _norm(x2, g_mix[0][None], mod_chunk(0, 1), mod_chunk(0, 0), seq)
    out = None
    for layer in range(depth):
        sh1, sc1, gt1, sh2, sc2, gt2 = (mod_chunk(layer, idx) for idx in range(6))
        zqkv = _matmul_heads(h, w_in_t16, layer, o_q, qkv_width, qkv_width // 2, qkv_scale)
        z32, zf = _matmul_windows(h, w_in_t, layer, proj_windows)
        z32 = z32.reshape(batch, seq, pxy_width)

        fbias = _forget_bias(zf.reshape(batch, seq, LANES), bf_pad[layer][None], n_heads)
        y_attn = _attention(zqkv, fbias, batch, seq).reshape(batch * seq, attn_width)
        y_pool = _pool_mixer(z32, pool_w16, layer, pool_scale[layer][None], pool_width)
        lru_p = dict(conv_w=lru_conv_w[layer], conv_b=lru_conv_b[layer][None], wa=lru_wa16,
                     ba=lru_ba[layer][None], wi=lru_wi16, bi=lru_bi[layer][None], lam=lru_lambda[layer][None])
        y_lru = _lru_mixer(z32, layer, lru_p, lru_width, 1, 2)

        x2, h = _mix_out(y_pool.reshape(batch * seq, pool_width), y_attn, y_lru.reshape(batch * seq, lru_width),
                         w_out16, layer, x2, gt1, g_ffn[layer][None], sc2, sh2, seq)
        act, w_down16 = _ffn_gate(h, w_ffn_gate, w_ffn_up, w_ffn_down, layer, ffn_conv_w[layer],
                                  ffn_conv_b[layer][None], seq)
        if layer + 1 < depth:
            x2, h = _ffn_down(act, w_down16, x2, gt2, g_mix[layer + 1][None], mod_chunk(layer + 1, 1),
                              mod_chunk(layer + 1, 0), seq, True, BF16)
        else:
            zeros = jnp.zeros((batch, 1, d), F32)
            (out,) = _ffn_down(act, w_down16, x2, gt2, final_g[None], zeros, zeros, seq, False, F32)
    return out.reshape(batch, seq, d)
```

```python
import functools

import jax
import jax.numpy as jnp
from jax import lax
from jax.experimental import pallas as pl
from jax.experimental.pallas import tpu as pltpu

F32 = jnp.float32
BF16 = jnp.bfloat16

HEAD_DIM = 128
N_POOL_GROUPS = 4
POOL_WINDOWS = (2, 4, 8, 16)
POOL_HALO = 16
N_LRU_BLOCKS = 4
LRU_CONV_WIDTH = 4
LRU_C = 8.0
FFN_CONV_WIDTH = 3
EPS = 1e-6

LANES = 128
SUBLANES = 8
VMEM_LIMIT_CAP_V7X = 58 * 2**20
MASK_VALUE = -1e30
LOG2E = 1.4426950408889634


def _cparams(semantics, vmem_bytes):
    return pltpu.CompilerParams(dimension_semantics=semantics,
                                vmem_limit_bytes=int(min(VMEM_LIMIT_CAP_V7X, max(vmem_bytes, 16 * 2**20))))


def _sigmoid(x):
    return 0.5 * jnp.tanh(0.5 * x) + 0.5


def _norm_mod(x, g, sc, sh):
    ms = jnp.mean(x * x, axis=-1, keepdims=True)
    return (x * lax.rsqrt(ms + EPS) * g) * (1.0 + sc) + sh


def _split3_bf16(x):
    hi = x.astype(BF16)
    r1 = x - hi.astype(F32)
    mid = r1.astype(BF16)
    lo = (r1 - mid.astype(F32)).astype(BF16)
    return hi, mid, lo


def _ada_kernel(c_ref, w_ref, b_ref, o_ref, acc_ref):
    nb = c_ref.shape[0]
    n = w_ref.shape[3]
    k = pl.program_id(1)

    @pl.when(k == 0)
    def _():
        acc_ref[...] = jnp.zeros_like(acc_ref)

    ca = [c_ref[b] * _sigmoid(c_ref[b]) for b in range(nb)]

    def lane_chunk(jc, carry):
        sl = pl.ds(pl.multiple_of(jc * LANES, LANES), LANES)
        w = w_ref[0, :, :, sl]
        for b in range(nb):
            acc_ref[b, :, sl] += jnp.sum(w * ca[b], axis=0)
        return carry

    lax.fori_loop(0, n // LANES, lane_chunk, 0, unroll=4)

    @pl.when(k == pl.num_programs(1) - 1)
    def _():
        o_ref[...] = jnp.zeros_like(o_ref)
        for b in range(nb):
            o_ref[0, b:b + 1, :] = jnp.sum(acc_ref[b], axis=0, keepdims=True) + b_ref[0]


ADA_SLAB_ROWS = 256


def _ada_mod(c, w_ada, b_ada):
    depth, d, n = w_ada.shape
    batch = c.shape[0]
    dr = d // SUBLANES
    sr = ADA_SLAB_ROWS // SUBLANES
    c_lanes = jnp.broadcast_to(c[:, :, None], (batch, d, LANES)).reshape(batch, dr, SUBLANES, LANES)
    return pl.pallas_call(
        _ada_kernel,
        grid=(depth, dr // sr),
        in_specs=[pl.BlockSpec((batch, sr, SUBLANES, LANES), lambda l, k: (0, k, 0, 0)),
                  pl.BlockSpec((1, sr, SUBLANES, n), lambda l, k: (l, k, 0, 0)),
                  pl.BlockSpec((1, 1, n), lambda l, k: (l, 0, 0))],
        out_specs=pl.BlockSpec((1, SUBLANES, n), lambda l, k: (l, 0, 0)),
        out_shape=jax.ShapeDtypeStruct((depth, SUBLANES, n), F32),
        scratch_shapes=[pltpu.VMEM((batch, SUBLANES, n), F32)],
        compiler_params=_cparams(("parallel", "arbitrary"), 3 * ADA_SLAB_ROWS * n * 4),
        name="ada_mod",
    )(c_lanes, w_ada.reshape(depth, dr, SUBLANES, n), b_ada.reshape(depth, 1, n))


def _norm_kernel(x_ref, g_ref, sc_ref, sh_ref, h_ref):
    h_ref[...] = _norm_mod(x_ref[...], g_ref[...], sc_ref[0], sh_ref[0]).astype(h_ref.dtype)


def _norm(x2, g, sc, sh, seq):
    t, d = x2.shape
    tm = min(512, seq)
    tpb = seq // tm
    return pl.pallas_call(
        _norm_kernel,
        grid=(t // tm,),
        in_specs=[pl.BlockSpec((tm, d), lambda i: (i, 0)),
                  pl.BlockSpec((1, d), lambda i: (0, 0)),
                  pl.BlockSpec((1, 1, d), lambda i: (i // tpb, 0, 0)),
                  pl.BlockSpec((1, 1, d), lambda i: (i // tpb, 0, 0))],
        out_specs=pl.BlockSpec((tm, d), lambda i: (i, 0)),
        out_shape=jax.ShapeDtypeStruct((t, d), BF16),
        compiler_params=_cparams(("parallel",), 6 * tm * d * 4),
        name="norm_mod",
    )(x2, g, sc, sh)


_NT_DIMS = (((1,), (1,)), ((), ()))


def _mm_windows_kernel(a_ref, *refs):
    n_w = len(refs) - 2
    w_refs, o_ref, on_ref = refs[:n_w], refs[n_w], refs[n_w + 1]
    a = a_ref[...]
    col = 0
    for w_ref in w_refs[:-1]:
        width = w_ref.shape[1]
        o_ref[:, col:col + width] = lax.dot_general(a, w_ref[0].astype(BF16), _NT_DIMS, preferred_element_type=F32)
        col += width
    on_ref[...] = lax.dot_general(a, w_refs[-1][0].astype(BF16), _NT_DIMS, preferred_element_type=F32)


def _mm_heads_kernel(a_ref, w_ref, cs_ref, o_ref):
    acc = lax.dot_general(a_ref[...], w_ref[0].astype(BF16), _NT_DIMS, preferred_element_type=F32) * cs_ref[...]
    for hh in range(o_ref.shape[0]):
        o_ref[hh] = acc[:, hh * HEAD_DIM:(hh + 1) * HEAD_DIM].astype(o_ref.dtype)


def _matmul_windows(a, wt_stack, layer, windows):
    t, k = a.shape
    tm = min(1024, t)
    wide = sum(n for _, n in windows[:-1])
    narrow = windows[-1][1]
    rows = wide + narrow
    vmem = rows * k * 6 + 2 * (tm * k * 2 + tm * rows * 4) + 2 * tm * wide * 4
    w_specs = [pl.BlockSpec((pl.Element(1), pl.Element(n), pl.Element(k)), lambda i, r=r: (layer, r, 0),
                            pipeline_mode=pl.Buffered(1))
               for r, n in windows]
    return pl.pallas_call(
        _mm_windows_kernel,
        grid=(t // tm,),
        in_specs=[pl.BlockSpec((tm, k), lambda i: (i, 0))] + w_specs,
        out_specs=[pl.BlockSpec((tm, wide), lambda i: (i, 0)), pl.BlockSpec((tm, narrow), lambda i: (i, 0))],
        out_shape=[jax.ShapeDtypeStruct((t, wide), F32), jax.ShapeDtypeStruct((t, narrow), F32)],
        compiler_params=_cparams(("parallel",), vmem),
        name="matmul_dmodel",
    )(a, *([wt_stack] * len(windows)))


def _matmul_heads(a, wt_stack, layer, row0, n, tn, col_scale):
    t, k = a.shape
    tm = min(1024, t)
    hpb = tn // HEAD_DIM
    vmem = 2 * (tm * k * 2 + k * tn * 4 + tm * tn * 2) + k * tn * 2 + 2 * tm * tn * 4
    return pl.pallas_call(
        _mm_heads_kernel,
        grid=(t // tm, n // tn),
        in_specs=[pl.BlockSpec((tm, k), lambda i, j: (i, 0)),
                  pl.BlockSpec((pl.Element(1), pl.Element(tn), pl.Element(k)),
                               lambda i, j: (layer, pl.multiple_of(row0 + j * tn, HEAD_DIM), 0)),
                  pl.BlockSpec((1, tn), lambda i, j: (0, j))],
        out_specs=pl.BlockSpec((hpb, tm, HEAD_DIM), lambda i, j: (j, i, 0)),
        out_shape=jax.ShapeDtypeStruct((n // HEAD_DIM, t, HEAD_DIM), BF16),
        compiler_params=_cparams(("parallel", "parallel"), vmem),
        name="matmul_heads",
    )(a, wt_stack, col_scale)


N_SPLIT = 3


def _fcum_kernel(z_ref, b_ref, o_ref, carry_ref, tri_ref, place_ref):
    ts = z_ref.shape[1]
    w = z_ref.shape[2]
    wo = place_ref.shape[1]

    @pl.when((pl.program_id(0) == 0) & (pl.program_id(1) == 0))
    def _():
        r = lax.broadcasted_iota(jnp.int32, (ts, ts), 0)
        c = lax.broadcasted_iota(jnp.int32, (ts, ts), 1)
        tri_ref[...] = jnp.where(c <= r, 1.0, 0.0).astype(BF16)
        row = lax.broadcasted_iota(jnp.int32, (N_SPLIT * w, wo), 0)
        lane = lax.broadcasted_iota(jnp.int32, (N_SPLIT * w, wo), 1)
        target = jnp.zeros_like(row)
        for idx in range(N_SPLIT):
            in_part = (row >= idx * w) & (row < (idx + 1) * w)
            target = jnp.where(in_part, (row - idx * w) * HEAD_DIM + idx, target)
        place_ref[...] = jnp.where(lane == target, 1.0, 0.0).astype(BF16)

    @pl.when(pl.program_id(1) == 0)
    def _():
        carry_ref[...] = jnp.zeros_like(carry_ref)

    x = z_ref[0] + b_ref[...]
    lf = jnp.minimum(x, 0.0) - jnp.log(1.0 + jnp.exp(-jnp.abs(x)))
    parts = jnp.dot(tri_ref[...], jnp.concatenate(_split3_bf16(lf), axis=1), preferred_element_type=F32)
    cs = carry_ref[...] + parts[:, :w]
    for idx in range(1, N_SPLIT):
        cs = cs + parts[:, idx * w:(idx + 1) * w]
    carry_ref[...] = cs[ts - 1:ts, :]

    split = jnp.concatenate(_split3_bf16(cs * LOG2E), axis=1)
    bias = jnp.dot(split, place_ref[...], preferred_element_type=F32)
    for hh in range(o_ref.shape[0]):
        o_ref[hh, 0] = bias[:, hh * HEAD_DIM:(hh + 1) * HEAD_DIM].astype(o_ref.dtype)


def _forget_bias(zf, bf_pad, n_heads):
    b, s, w = zf.shape
    ts = min(512, s)
    wo = n_heads * HEAD_DIM
    return pl.pallas_call(
        _fcum_kernel,
        grid=(b, s // ts),
        in_specs=[pl.BlockSpec((1, ts, w), lambda bi, i: (bi, i, 0)),
                  pl.BlockSpec((1, w), lambda bi, i: (0, 0))],
        out_specs=pl.BlockSpec((n_heads, 1, ts, HEAD_DIM), lambda bi, i: (0, bi, i, 0)),
        out_shape=jax.ShapeDtypeStruct((n_heads, b, s, HEAD_DIM), BF16),
        scratch_shapes=[pltpu.VMEM((1, w), F32), pltpu.VMEM((ts, ts), BF16), pltpu.VMEM((N_SPLIT * w, wo), BF16)],
        compiler_params=_cparams(("arbitrary", "arbitrary"), 4 * ts * ts * 4 + 8 * ts * wo * 4),
        name="forget_bias",
    )(zf, bf_pad)


ONES_ROWS = 16


MAX_CHAINS = 8
BIG_CHUNK = 4
STALE_MAX_GUARD = 60.0


def _attn_kernel(q_ref, k_ref, v_ref, fb_ref, o_ref, vt_ref, acc_ref, m_ref, *, tq):
    tk = tq
    s_len = k_ref.shape[1]
    step = pl.program_id(2)

    @pl.when(step == 0)
    def _():
        vt_ref[HEAD_DIM:, :] = jnp.ones((ONES_ROWS, s_len), BF16)
        for c in range(s_len // tk):
            vt_ref[:HEAD_DIM, c * tk:(c + 1) * tk] = v_ref[0, c * tk:(c + 1) * tk, :].astype(F32).T.astype(BF16)

    for sub in range(q_ref.shape[1] // tq):
        rows = slice(sub * tq, (sub + 1) * tq)
        o_ref[0, rows, :] = _attn_query_tile(step * (q_ref.shape[1] // tq) + sub, q_ref[0, rows, :], k_ref, fb_ref,
                                             vt_ref, acc_ref, m_ref).astype(o_ref.dtype)


def _attn_query_tile(qi, q, k_ref, fb_ref, vt_ref, acc_ref, m_ref):
    tq = q.shape[0]
    tk = tq
    lane = lax.broadcasted_iota(jnp.int32, (tq, HEAD_DIM), 1)
    minus_one = jnp.where(lane < 3, -1.0, 0.0).astype(BF16)
    q_aug = jnp.concatenate([q, minus_one], axis=1)
    m_ref[...] = jnp.full(m_ref.shape, MASK_VALUE, F32)
    acc_ref[...] = jnp.zeros_like(acc_ref)

    def logits(k0, size):
        k0 = pl.multiple_of(k0, tk)
        k_aug = jnp.concatenate([k_ref[0, pl.ds(k0, size), :], fb_ref[0, 0, pl.ds(k0, size), :]], axis=1)
        return lax.dot_general(k_aug, q_aug, (((1,), (1,)), ((), ())), preferred_element_type=F32)

    def absorb(st, k0, size, masked):
        k0 = pl.multiple_of(k0, tk)
        if masked:
            krow = lax.broadcasted_iota(jnp.int32, (size, tq), 0)
            qcol = lax.broadcasted_iota(jnp.int32, (size, tq), 1)
            st = jnp.where(krow <= qcol, st, MASK_VALUE)
        part = jnp.max(st.reshape(MAX_CHAINS, size // MAX_CHAINS, tq), axis=1)
        m_old = m_ref[...]
        m_new = jnp.maximum(m_old, jnp.max(part, axis=0, keepdims=True))
        m_ref[...] = m_new
        pt = jnp.exp2((st - m_new).astype(BF16))
        acc_ref[...] = jnp.exp2(m_old - m_new) * acc_ref[...] + jnp.dot(
            vt_ref[:, pl.ds(k0, size)], pt, preferred_element_type=F32)

    def absorb_one_pass(st, k0, size):
        k0 = pl.multiple_of(k0, tk)
        m_old = m_ref[...]
        part = jnp.max(st.reshape(MAX_CHAINS, size // MAX_CHAINS, tq), axis=1)
        cmax = jnp.max(part, axis=0, keepdims=True)
        pt = jnp.exp2(st - m_old).astype(BF16)
        pv = jnp.dot(vt_ref[:, pl.ds(k0, size)], pt, preferred_element_type=F32)
        safe = jnp.max(cmax - m_old) <= STALE_MAX_GUARD

        @pl.when(safe)
        def _():
            m_new = jnp.maximum(m_old, cmax)
            m_ref[...] = m_new
            acc_ref[...] = (acc_ref[...] + pv) * jnp.exp2(m_old - m_new)

        @pl.when(jnp.logical_not(safe))
        def _():
            absorb(logits(k0, size), k0, size, False)

    n_big = qi // BIG_CHUNK
    tail_k0 = n_big * (BIG_CHUNK * tk)
    for visible in range(BIG_CHUNK):
        @pl.when(qi - n_big * BIG_CHUNK == visible)
        def _(visible=visible):
            diag_k0 = tail_k0 + visible * tk
            if visible == 0:
                absorb(logits(diag_k0, tk), diag_k0, tk, True)
            else:
                st_a = logits(tail_k0, visible * tk)
                st_b = logits(diag_k0, tk)
                absorb(st_a, tail_k0, visible * tk, False)
                absorb(st_b, diag_k0, tk, True)

    def body(kc, carry):
        k0 = kc * (BIG_CHUNK * tk)
        absorb_one_pass(logits(k0, BIG_CHUNK * tk), k0, BIG_CHUNK * tk)
        return carry

    lax.fori_loop(0, n_big, body, 0)
    o_t = acc_ref[:HEAD_DIM, :] * (1.0 / acc_ref[HEAD_DIM:HEAD_DIM + 1, :])
    return o_t.T


def _attention(zqkv, fbias, batch, seq):
    n_heads = fbias.shape[0]
    b, s = batch, seq
    tq = min(512, s)
    tiles_per_step = 2 if s % (2 * tq) == 0 else 1
    tqs = tq * tiles_per_step
    spb = s // tqs
    assert s % tqs == 0
    vmem = 2 * (3 * s * HEAD_DIM * 2) + (HEAD_DIM + ONES_ROWS) * s * 2 + 6 * BIG_CHUNK * tq * tq * 4
    kv_spec = lambda off: pl.BlockSpec((1, s, HEAD_DIM), lambda bi, h, i: (off + h, bi, 0))
    return pl.pallas_call(
        functools.partial(_attn_kernel, tq=tq),
        grid=(b, n_heads, spb),
        in_specs=[pl.BlockSpec((1, tqs, HEAD_DIM), lambda bi, h, i: (h, bi * spb + i, 0)),
                  kv_spec(n_heads), kv_spec(2 * n_heads),
                  pl.BlockSpec((1, 1, s, HEAD_DIM), lambda bi, h, i: (h, bi, 0, 0))],
        out_specs=pl.BlockSpec((1, tqs, HEAD_DIM), lambda bi, h, i: (bi, i, h)),
        out_shape=jax.ShapeDtypeStruct((b, s, n_heads * HEAD_DIM), BF16),
        scratch_shapes=[pltpu.VMEM((HEAD_DIM + ONES_ROWS, s), BF16),
                        pltpu.VMEM((HEAD_DIM + ONES_ROWS, tq), F32),
                        pltpu.VMEM((1, tq), F32)],
        compiler_params=_cparams(("parallel", "parallel", "arbitrary"), vmem),
        name="fox_attention",
    )(zqkv, zqkv, zqkv, fbias)


def _pool_kernel(u_ref, halo_ref, w_ref, sc_ref, o_ref):
    tm = u_ref.shape[1]
    i = pl.program_id(1)
    halo = jnp.where(i > 0, halo_ref[0], 0.0)
    ext = jnp.concatenate([halo, u_ref[0]], axis=0)
    pos = (i * tm + 1 + lax.broadcasted_iota(jnp.int32, (tm, 1), 0)).astype(F32)
    gd = LANES
    for g, win in enumerate(POOL_WINDOWS):
        e = ext[:, g * gd:(g + 1) * gd]
        ssum = e
        shift = 1
        while shift < win:
            ssum = ssum + pltpu.roll(ssum, shift, 0)
            shift *= 2
        mean = ssum[POOL_HALO:] * (1.0 / jnp.minimum(pos, float(win)))
        dlt = mean - e[POOL_HALO:]
        y = jnp.dot(dlt.astype(BF16), w_ref[0, g], preferred_element_type=F32)
        o_ref[0, :, g * gd:(g + 1) * gd] = (y * sc_ref[:, g * gd:(g + 1) * gd]).astype(o_ref.dtype)


def _pool_mixer(z32, pool_w_stack, layer, scale_row, width):
    b, s, _ = z32.shape
    tm = min(512, s)
    hb = tm // POOL_HALO
    return pl.pallas_call(
        _pool_kernel,
        grid=(b, s // tm),
        in_specs=[pl.BlockSpec((1, tm, width), lambda bi, i: (bi, i, 0)),
                  pl.BlockSpec((1, POOL_HALO, width), lambda bi, i: (bi, jnp.maximum(i * hb - 1, 0), 0)),
                  pl.BlockSpec((1, N_POOL_GROUPS, LANES, LANES), lambda bi, i: (layer, 0, 0, 0)),
                  pl.BlockSpec((1, width), lambda bi, i: (0, 0))],
        out_specs=pl.BlockSpec((1, tm, width), lambda bi, i: (bi, i, 0)),
        out_shape=jax.ShapeDtypeStruct((b, s, width), BF16),
        compiler_params=_cparams(("parallel", "parallel"), 16 * tm * width * 4),
        name="pool_mixer",
    )(z32, z32, pool_w_stack, scale_row)


def _lru_kernel(x_ref, halo_ref, y_ref, cw_ref, cb_ref, wa_ref, ba_ref, wi_ref, bi_ref, lam_ref,
                o_ref, h_ref, a_s, b_s, h_s):
    tm = x_ref.shape[1]
    i = pl.program_id(1)

    @pl.when(i == 0)
    def _():
        h_ref[...] = jnp.zeros_like(h_ref)

    halo = jnp.where(i > 0, halo_ref[0], 0.0)
    ext = jnp.concatenate([halo, x_ref[0]], axis=0)
    xc = cb_ref[...] + cw_ref[LRU_CONV_WIDTH - 1:LRU_CONV_WIDTH, :] * ext[SUBLANES:]
    for k in range(LRU_CONV_WIDTH - 1):
        shifted = pltpu.roll(ext, LRU_CONV_WIDTH - 1 - k, 0)[SUBLANES:]
        xc = xc + cw_ref[k:k + 1, :] * shifted

    lam = lam_ref[...]
    neg_softplus = -(jnp.maximum(-lam, 0.0) + jnp.log(1.0 + jnp.exp(-jnp.abs(lam))))
    gd = LANES
    n_groups = tm // SUBLANES
    sub_row = lax.broadcasted_iota(jnp.int32, (n_groups, SUBLANES, gd), 1)
    for blk in range(N_LRU_BLOCKS):
        sl = slice(blk * gd, (blk + 1) * gd)
        xb = xc[:, sl]
        xb16 = xb.astype(BF16)
        gate_r = _sigmoid(jnp.dot(xb16, wa_ref[0, blk], preferred_element_type=F32) + ba_ref[:, sl])
        gate_i = _sigmoid(jnp.dot(xb16, wi_ref[0, blk], preferred_element_type=F32) + bi_ref[:, sl])
        log_a = LRU_C * gate_r * neg_softplus[:, sl]
        a = jnp.exp(log_a)
        b = jnp.sqrt(1.0 - a * a) * (gate_i * xb)
        a = a.reshape(n_groups, SUBLANES, gd)
        b = b.reshape(n_groups, SUBLANES, gd)
        dist = 1
        while dist < SUBLANES:
            has_prev = sub_row >= dist
            b = b + a * jnp.where(has_prev, pltpu.roll(b, dist, 1), 0.0)
            a = a * jnp.where(has_prev, pltpu.roll(a, dist, 1), 1.0)
            dist *= 2
        a_s[blk] = a
        b_s[blk] = b

    def group(gi, hs):
        out = []
        for blk in range(N_LRU_BLOCKS):
            rows = b_s[blk, gi] + a_s[blk, gi] * hs[blk]
            h_s[blk, gi] = rows
            out.append(rows[SUBLANES - 1:SUBLANES, :])
        return tuple(out)

    hs = lax.fori_loop(0, n_groups, group,
                       tuple(h_ref[:, blk * gd:(blk + 1) * gd] for blk in range(N_LRU_BLOCKS)), unroll=8)

    for blk in range(N_LRU_BLOCKS):
        sl = slice(blk * gd, (blk + 1) * gd)
        h_ref[:, sl] = hs[blk]
        y = y_ref[0, :, sl]
        gelu = 0.5 * y * (1.0 + jnp.tanh(0.7978845608028654 * (y + 0.044715 * (y * y * y))))
        o_ref[0, :, sl] = (h_s[blk].reshape(tm, gd) * gelu).astype(o_ref.dtype)


def _lru_mixer(z32, layer, p, width, x_col, y_col):
    b, s, _ = z32.shape
    tm = min(512, s)
    hb = tm // SUBLANES
    row = lambda bi, i: (0, 0)
    return pl.pallas_call(
        _lru_kernel,
        grid=(b, s // tm),
        in_specs=[pl.BlockSpec((1, tm, width), lambda bi, i: (bi, i, x_col)),
                  pl.BlockSpec((1, SUBLANES, width), lambda bi, i: (bi, jnp.maximum(i * hb - 1, 0), x_col)),
                  pl.BlockSpec((1, tm, width), lambda bi, i: (bi, i, y_col)),
                  pl.BlockSpec((LRU_CONV_WIDTH, width), row),
                  pl.BlockSpec((1, width), row),
                  pl.BlockSpec((1, N_LRU_BLOCKS, LANES, LANES), lambda bi, i: (layer, 0, 0, 0)),
                  pl.BlockSpec((1, width), row),
                  pl.BlockSpec((1, N_LRU_BLOCKS, LANES, LANES), lambda bi, i: (layer, 0, 0, 0)),
                  pl.BlockSpec((1, width), row),
                  pl.BlockSpec((1, width), row)],
        out_specs=pl.BlockSpec((1, tm, width), lambda bi, i: (bi, i, 0)),
        out_shape=jax.ShapeDtypeStruct((b, s, width), BF16),
        scratch_shapes=[pltpu.VMEM((1, width), F32)]
        + [pltpu.VMEM((N_LRU_BLOCKS, tm // SUBLANES, SUBLANES, LANES), F32)] * 3,
        compiler_params=_cparams(("arbitrary", "arbitrary"), 24 * tm * width * 4),
        name="rg_lru",
    )(z32, z32, z32, p["conv_w"], p["conv_b"], p["wa"], p["ba"], p["wi"], p["bi"], p["lam"])


COL_BLOCK = 512


def _col_blocks(d):
    return [slice(c, c + COL_BLOCK) for c in range(0, d, COL_BLOCK)]


def _residual_cols(acc, cols, x_ref, gt_ref, xn_refs):
    xn = x_ref[:, cols] + gt_ref[0, :, cols] * acc
    for ref in xn_refs:
        ref[:, cols] = xn
    return jnp.sum(xn * xn, axis=-1, keepdims=True)


def _norm_rows(sumsq, xn_ref, g_ref, sc_ref, sh_ref, h_ref):
    rs = lax.rsqrt(sumsq * (1.0 / xn_ref.shape[1]) + EPS)
    gain = g_ref[...] * (1.0 + sc_ref[0])
    h_ref[...] = (xn_ref[...] * rs * gain + sh_ref[0]).astype(h_ref.dtype)


def _mix_out_kernel(yp_ref, ya_ref, yl_ref, w_ref, x_ref, gt_ref, g_ref, sc_ref, sh_ref, xo_ref, h_ref, w16_ref):
    kp = yp_ref.shape[1]
    ka = ya_ref.shape[1]

    @pl.when(pl.program_id(0) == 0)
    def _():
        w16_ref[...] = w_ref[0].astype(BF16)

    sumsq = jnp.zeros((x_ref.shape[0], 1), F32)
    for cols in _col_blocks(x_ref.shape[1]):
        acc = jnp.dot(yp_ref[...], w16_ref[0:kp, cols], preferred_element_type=F32)
        acc = acc + jnp.dot(ya_ref[...], w16_ref[kp:kp + ka, cols], preferred_element_type=F32)
        acc = acc + jnp.dot(yl_ref[...], w16_ref[kp + ka:, cols], preferred_element_type=F32)
        sumsq = sumsq + _residual_cols(acc, cols, x_ref, gt_ref, (xo_ref,))
    _norm_rows(sumsq, xo_ref, g_ref, sc_ref, sh_ref, h_ref)


def _mix_out(yp, ya, yl, w_stack, layer, x2, gt, g, sc, sh, seq):
    t, d = x2.shape
    tm = min(512, seq)
    tpb = seq // tm
    per_b = lambda i: (i // tpb, 0, 0)
    lhs = lambda y: pl.BlockSpec((tm, y.shape[1]), lambda i: (i, 0))
    vmem = d * d * 6 + 2 * (tm * d * 2 + 3 * tm * d * 4) + 3 * tm * d * 4
    return pl.pallas_call(
        _mix_out_kernel,
        grid=(t // tm,),
        in_specs=[lhs(yp), lhs(ya), lhs(yl),
                  pl.BlockSpec((1, d, d), lambda i: (layer, 0, 0), pipeline_mode=pl.Buffered(1)),
                  pl.BlockSpec((tm, d), lambda i: (i, 0)),
                  pl.BlockSpec((1, 1, d), per_b),
                  pl.BlockSpec((1, d), lambda i: (0, 0)),
                  pl.BlockSpec((1, 1, d), per_b),
                  pl.BlockSpec((1, 1, d), per_b)],
        out_specs=[pl.BlockSpec((tm, d), lambda i: (i, 0)), pl.BlockSpec((tm, d), lambda i: (i, 0))],
        out_shape=[jax.ShapeDtypeStruct((t, d), F32), jax.ShapeDtypeStruct((t, d), BF16)],
        scratch_shapes=[pltpu.VMEM((d, d), BF16)],
        compiler_params=_cparams(("arbitrary",), vmem),
        name="mix_out_proj",
    )(yp, ya, yl, w_stack, x2, gt, g, sc, sh)


def _ffn_down_kernel(a_ref, w_ref, x_ref, gt_ref, g_ref, sc_ref, sh_ref, *out_refs):
    xn_ref, h_ref = out_refs[0], out_refs[-1]
    sumsq = jnp.zeros((x_ref.shape[0], 1), F32)
    for cols in _col_blocks(x_ref.shape[1]):
        acc = jnp.dot(a_ref[...], w_ref[:, cols], preferred_element_type=F32)
        sumsq = sumsq + _residual_cols(acc, cols, x_ref, gt_ref, (xn_ref,))
    _norm_rows(sumsq, xn_ref, g_ref, sc_ref, sh_ref, h_ref)


def _ffn_down(act, w16, x2, gt, g, sc, sh, seq, emit_x, h_dtype):
    t, d = x2.shape
    kf = act.shape[1]
    tm = min(512, seq)
    tpb = seq // tm
    per_b = lambda i: (i // tpb, 0, 0)
    row_tile = pl.BlockSpec((tm, d), lambda i: (i, 0))
    out_specs = [row_tile]
    out_shape = [jax.ShapeDtypeStruct((t, d), h_dtype)]
    if emit_x:
        out_specs = [row_tile, row_tile]
        out_shape = [jax.ShapeDtypeStruct((t, d), F32)] + out_shape
    else:
        assert h_dtype == F32
    vmem = kf * d * 2 + 2 * (tm * kf * 2 + 3 * tm * d * 4) + 2 * tm * COL_BLOCK * 4
    return pl.pallas_call(
        _ffn_down_kernel,
        grid=(t // tm,),
        in_specs=[pl.BlockSpec((tm, kf), lambda i: (i, 0)),
                  pl.BlockSpec((kf, d), lambda i: (0, 0), pipeline_mode=pl.Buffered(1)),
                  row_tile,
                  pl.BlockSpec((1, 1, d), per_b),
                  pl.BlockSpec((1, d), lambda i: (0, 0)),
                  pl.BlockSpec((1, 1, d), per_b),
                  pl.BlockSpec((1, 1, d), per_b)],
        out_specs=out_specs,
        out_shape=out_shape,
        compiler_params=_cparams(("parallel",), vmem),
        name="ffn_down_proj",
    )(act, w16, x2, gt, g, sc, sh)


def _ffn_gate_kernel(h_ref, wg_ref, wu_ref, cw_ref, cb_ref, wd_ref, o_ref, wd16_ref, carry_ref, *, tiles_per_batch):
    tm = h_ref.shape[0]
    i = pl.program_id(0)
    j = pl.program_id(1)

    @pl.when(i % tiles_per_batch == 0)
    def _():
        carry_ref[j] = jnp.zeros(carry_ref.shape[1:], F32)

    @pl.when(i == 0)
    def _():
        wd16_ref[...] = wd_ref[0].astype(BF16)

    h = h_ref[...]
    u = jnp.dot(h, wg_ref[0].astype(BF16), preferred_element_type=F32)
    ext = jnp.concatenate([carry_ref[j], u], axis=0)
    carry_ref[j] = u[tm - SUBLANES:, :]
    g = cb_ref[...] + cw_ref[FFN_CONV_WIDTH - 1:FFN_CONV_WIDTH, :] * u
    for k in range(FFN_CONV_WIDTH - 1):
        shifted = pltpu.roll(ext, FFN_CONV_WIDTH - 1 - k, 0)[SUBLANES:]
        g = g + cw_ref[k:k + 1, :] * shifted
    up = jnp.dot(h, wu_ref[0].astype(BF16), preferred_element_type=F32)
    o_ref[...] = (g * _sigmoid(g) * up).astype(o_ref.dtype)


def _ffn_gate(h, wg_stack, wu_stack, wd_stack, layer, conv_w, conv_b, seq):
    t, d = h.shape
    kf = wg_stack.shape[2]
    tm = min(1024, seq)
    tf = 512
    nj = kf // tf
    wspec = pl.BlockSpec((1, d, tf), lambda i, j: (layer, 0, j))
    slab = lambda i, j: jnp.where(i == 0, j, nj - 1)
    vmem = 2 * (tm * d * 2 + 2 * d * tf * 4 + tm * tf * 2 + tf * d * 6) + 2 * d * tf * 2 + 8 * tm * tf * 4
    return pl.pallas_call(
        functools.partial(_ffn_gate_kernel, tiles_per_batch=seq // tm),
        grid=(t // tm, nj),
        in_specs=[pl.BlockSpec((tm, d), lambda i, j: (i, 0)), wspec, wspec,
                  pl.BlockSpec((FFN_CONV_WIDTH, tf), lambda i, j: (0, j)),
                  pl.BlockSpec((1, tf), lambda i, j: (0, j)),
                  pl.BlockSpec((1, tf, d), lambda i, j: (layer, slab(i, j), 0))],
        out_specs=[pl.BlockSpec((tm, tf), lambda i, j: (i, j)),
                   pl.BlockSpec((tf, d), lambda i, j: (slab(i, j), 0))],
        out_shape=[jax.ShapeDtypeStruct((t, kf), BF16), jax.ShapeDtypeStruct((kf, d), BF16)],
        scratch_shapes=[pltpu.VMEM((nj, SUBLANES, tf), F32)],
        compiler_params=_cparams(("arbitrary", "arbitrary"), vmem),
        name="ffn_gate_up",
    )(h, wg_stack, wu_stack, conv_w, conv_b, wd_stack)


def kernel(x, c, w_ada, b_ada, g_mix, w_in, b_f, pool_w, pool_scale, lru_conv_w, lru_conv_b, lru_wa, lru_ba,
           lru_wi, lru_bi, lru_lambda, w_out, g_ffn, w_ffn_gate, w_ffn_up, ffn_conv_w, ffn_conv_b, w_ffn_down,
           final_g):
    batch, seq, d = x.shape
    depth = w_ada.shape[0]
    pool_width = pool_w.shape[1] * pool_w.shape[2]
    lru_width = lru_lambda.shape[1]
    n_heads = b_f.shape[1]
    attn_width = n_heads * HEAD_DIM
    assert w_in.shape[2] == pool_width + 3 * attn_width + n_heads + 2 * lru_width
    assert pool_width == lru_width == N_POOL_GROUPS * LANES and seq % SUBLANES == 0

    o_q = pool_width
    o_f = o_q + 3 * attn_width
    o_x = o_f + n_heads
    w_in_t = jnp.swapaxes(w_in, 1, 2)
    qkv_width = 3 * attn_width
    pxy_width = pool_width + 2 * lru_width
    proj_windows = [(0, pool_width), (o_x, 2 * lru_width), (o_f, LANES)]
    bf_pad = jnp.pad(b_f, ((0, 0), (0, LANES - n_heads)))
    qkv_scale = jnp.concatenate([jnp.full((1, attn_width), HEAD_DIM ** -0.5 * LOG2E, F32),
                                 jnp.ones((1, 2 * attn_width), F32)], axis=1)
    pool_w16 = pool_w.astype(BF16)
    lru_wa16 = lru_wa.astype(BF16)
    lru_wi16 = lru_wi.astype(BF16)

    mod = _ada_mod(c, w_ada, b_ada)[:, :batch]

    def mod_chunk(layer, idx):
        return mod[layer, :, idx * d:(idx + 1) * d].reshape(batch, 1, d)

    x2 = x.reshape(batch * seq, d)
    h = _norm(x2, g_mix[0][None], mod_chunk(0, 1), mod_chunk(0, 0), seq)
    out = None
    for layer in range(depth):
        sh1, sc1, gt1, sh2, sc2, gt2 = (mod_chunk(layer, idx) for idx in range(6))
        zqkv = _matmul_heads(h, w_in_t, layer, o_q, qkv_width, qkv_width // 2, qkv_scale)
        z32, zf = _matmul_windows(h, w_in_t, layer, proj_windows)
        z32 = z32.reshape(batch, seq, pxy_width)

        fbias = _forget_bias(zf.reshape(batch, seq, LANES), bf_pad[layer][None], n_heads)
        y_attn = _attention(zqkv, fbias, batch, seq).reshape(batch * seq, attn_width)
        y_pool = _pool_mixer(z32, pool_w16, layer, pool_scale[layer][None], pool_width)
        lru_p = dict(conv_w=lru_conv_w[layer], conv_b=lru_conv_b[layer][None], wa=lru_wa16,
                     ba=lru_ba[layer][None], wi=lru_wi16, bi=lru_bi[layer][None], lam=lru_lambda[layer][None])
        y_lru = _lru_mixer(z32, layer, lru_p, lru_width, 1, 2)

        x2, h = _mix_out(y_pool.reshape(batch * seq, pool_width), y_attn, y_lru.reshape(batch * seq, lru_width),
                         w_out, layer, x2, gt1, g_ffn[layer][None], sc2, sh2, seq)
        act, w_down16 = _ffn_gate(h, w_ffn_gate, w_ffn_up, w_ffn_down, layer, ffn_conv_w[layer],
                                  ffn_conv_b[layer][None], seq)
        if layer + 1 < depth:
            x2, h = _ffn_down(act, w_down16, x2, gt2, g_mix[layer + 1][None], mod_chunk(layer + 1, 1),
                              mod_chunk(layer + 1, 0), seq, True, BF16)
        else:
            zeros = jnp.zeros((batch, 1, d), F32)
            (out,) = _ffn_down(act, w_down16, x2, gt2, final_g[None], zeros, zeros, seq, False, F32)
    return out.reshape(batch, seq, d)
```

```python
import functools

import jax
import jax.numpy as jnp
from jax import lax
from jax.experimental import pallas as pl
from jax.experimental.pallas import tpu as pltpu

F32 = jnp.float32
BF16 = jnp.bfloat16

HEAD_DIM = 128
N_POOL_GROUPS = 4
POOL_WINDOWS = (2, 4, 8, 16)
POOL_HALO = 16
N_LRU_BLOCKS = 4
LRU_CONV_WIDTH = 4
LRU_C = 8.0
FFN_CONV_WIDTH = 3
EPS = 1e-6

LANES = 128
SUBLANES = 8
VMEM_LIMIT_CAP_V7X = 58 * 2**20
MASK_VALUE = -1e30
LOG2E = 1.4426950408889634


def _cparams(semantics, vmem_bytes):
    return pltpu.CompilerParams(dimension_semantics=semantics,
                                vmem_limit_bytes=int(min(VMEM_LIMIT_CAP_V7X, max(vmem_bytes, 16 * 2**20))))


def _sigmoid(x):
    return 0.5 * jnp.tanh(0.5 * x) + 0.5


def _norm_mod(x, g, sc, sh):
    ms = jnp.mean(x * x, axis=-1, keepdims=True)
    return (x * lax.rsqrt(ms + EPS) * g) * (1.0 + sc) + sh


def _split3_bf16(x):
    hi = x.astype(BF16)
    r1 = x - hi.astype(F32)
    mid = r1.astype(BF16)
    lo = (r1 - mid.astype(F32)).astype(BF16)
    return hi, mid, lo


def _ada_kernel(c_ref, w_ref, b_ref, o_ref, acc_ref):
    nb = c_ref.shape[0]
    n = w_ref.shape[3]
    k = pl.program_id(1)

    @pl.when(k == 0)
    def _():
        acc_ref[...] = jnp.zeros_like(acc_ref)

    ca = [c_ref[b] * _sigmoid(c_ref[b]) for b in range(nb)]

    def lane_chunk(jc, carry):
        sl = pl.ds(pl.multiple_of(jc * LANES, LANES), LANES)
        w = w_ref[0, :, :, sl]
        for b in range(nb):
            acc_ref[b, :, sl] += jnp.sum(w * ca[b], axis=0)
        return carry

    lax.fori_loop(0, n // LANES, lane_chunk, 0, unroll=4)

    @pl.when(k == pl.num_programs(1) - 1)
    def _():
        o_ref[...] = jnp.zeros_like(o_ref)
        for b in range(nb):
            o_ref[0, b:b + 1, :] = jnp.sum(acc_ref[b], axis=0, keepdims=True) + b_ref[0]


ADA_SLAB_ROWS = 256


def _ada_mod(c, w_ada, b_ada):
    depth, d, n = w_ada.shape
    batch = c.shape[0]
    dr = d // SUBLANES
    sr = ADA_SLAB_ROWS // SUBLANES
    c_lanes = jnp.broadcast_to(c[:, :, None], (batch, d, LANES)).reshape(batch, dr, SUBLANES, LANES)
    return pl.pallas_call(
        _ada_kernel,
        grid=(depth, dr // sr),
        in_specs=[pl.BlockSpec((batch, sr, SUBLANES, LANES), lambda l, k: (0, k, 0, 0)),
                  pl.BlockSpec((1, sr, SUBLANES, n), lambda l, k: (l, k, 0, 0)),
                  pl.BlockSpec((1, 1, n), lambda l, k: (l, 0, 0))],
        out_specs=pl.BlockSpec((1, SUBLANES, n), lambda l, k: (l, 0, 0)),
        out_shape=jax.ShapeDtypeStruct((depth, SUBLANES, n), F32),
        scratch_shapes=[pltpu.VMEM((batch, SUBLANES, n), F32)],
        compiler_params=_cparams(("parallel", "arbitrary"), 3 * ADA_SLAB_ROWS * n * 4),
        name="ada_mod",
    )(c_lanes, w_ada.reshape(depth, dr, SUBLANES, n), b_ada.reshape(depth, 1, n))


def _norm_kernel(x_ref, g_ref, sc_ref, sh_ref, h_ref):
    h_ref[...] = _norm_mod(x_ref[...], g_ref[...], sc_ref[0], sh_ref[0]).astype(h_ref.dtype)


def _norm(x2, g, sc, sh, seq):
    t, d = x2.shape
    tm = min(512, seq)
    tpb = seq // tm
    return pl.pallas_call(
        _norm_kernel,
        grid=(t // tm,),
        in_specs=[pl.BlockSpec((tm, d), lambda i: (i, 0)),
                  pl.BlockSpec((1, d), lambda i: (0, 0)),
                  pl.BlockSpec((1, 1, d), lambda i: (i // tpb, 0, 0)),
                  pl.BlockSpec((1, 1, d), lambda i: (i // tpb, 0, 0))],
        out_specs=pl.BlockSpec((tm, d), lambda i: (i, 0)),
        out_shape=jax.ShapeDtypeStruct((t, d), BF16),
        compiler_params=_cparams(("parallel",), 6 * tm * d * 4),
        name="norm_mod",
    )(x2, g, sc, sh)


_NT_DIMS = (((1,), (1,)), ((), ()))


def _mm_windows_kernel(a_ref, *refs):
    n_w = len(refs) - 2
    w_refs, o_ref, on_ref = refs[:n_w], refs[n_w], refs[n_w + 1]
    a = a_ref[...]
    col = 0
    for w_ref in w_refs[:-1]:
        width = w_ref.shape[1]
        o_ref[:, col:col + width] = lax.dot_general(a, w_ref[0].astype(BF16), _NT_DIMS, preferred_element_type=F32)
        col += width
    on_ref[...] = lax.dot_general(a, w_refs[-1][0].astype(BF16), _NT_DIMS, preferred_element_type=F32)


def _mm_heads_kernel(a_ref, w_ref, cs_ref, o_ref):
    acc = lax.dot_general(a_ref[...], w_ref[0].astype(BF16), _NT_DIMS, preferred_element_type=F32) * cs_ref[...]
    for hh in range(o_ref.shape[0]):
        o_ref[hh] = acc[:, hh * HEAD_DIM:(hh + 1) * HEAD_DIM].astype(o_ref.dtype)


def _matmul_windows(a, wt_stack, layer, windows):
    t, k = a.shape
    tm = min(1024, t)
    wide = sum(n for _, n in windows[:-1])
    narrow = windows[-1][1]
    rows = wide + narrow
    vmem = rows * k * 6 + 2 * (tm * k * 2 + tm * rows * 4) + 2 * tm * wide * 4
    w_specs = [pl.BlockSpec((pl.Element(1), pl.Element(n), pl.Element(k)), lambda i, r=r: (layer, r, 0),
                            pipeline_mode=pl.Buffered(1))
               for r, n in windows]
    return pl.pallas_call(
        _mm_windows_kernel,
        grid=(t // tm,),
        in_specs=[pl.BlockSpec((tm, k), lambda i: (i, 0))] + w_specs,
        out_specs=[pl.BlockSpec((tm, wide), lambda i: (i, 0)), pl.BlockSpec((tm, narrow), lambda i: (i, 0))],
        out_shape=[jax.ShapeDtypeStruct((t, wide), F32), jax.ShapeDtypeStruct((t, narrow), F32)],
        compiler_params=_cparams(("parallel",), vmem),
        name="matmul_dmodel",
    )(a, *([wt_stack] * len(windows)))


def _matmul_heads(a, wt_stack, layer, row0, n, tn, col_scale):
    t, k = a.shape
    tm = min(1024, t)
    hpb = tn // HEAD_DIM
    vmem = 2 * (tm * k * 2 + k * tn * 4 + tm * tn * 2) + k * tn * 2 + 2 * tm * tn * 4
    return pl.pallas_call(
        _mm_heads_kernel,
        grid=(t // tm, n // tn),
        in_specs=[pl.BlockSpec((tm, k), lambda i, j: (i, 0)),
                  pl.BlockSpec((pl.Element(1), pl.Element(tn), pl.Element(k)),
                               lambda i, j: (layer, pl.multiple_of(row0 + j * tn, HEAD_DIM), 0)),
                  pl.BlockSpec((1, tn), lambda i, j: (0, j))],
        out_specs=pl.BlockSpec((hpb, tm, HEAD_DIM), lambda i, j: (j, i, 0)),
        out_shape=jax.ShapeDtypeStruct((n // HEAD_DIM, t, HEAD_DIM), BF16),
        compiler_params=_cparams(("parallel", "parallel"), vmem),
        name="matmul_heads",
    )(a, wt_stack, col_scale)


N_SPLIT = 3


def _fcum_kernel(z_ref, b_ref, o_ref, carry_ref, tri_ref, place_ref):
    ts = z_ref.shape[1]
    w = z_ref.shape[2]
    wo = place_ref.shape[1]

    @pl.when((pl.program_id(0) == 0) & (pl.program_id(1) == 0))
    def _():
        r = lax.broadcasted_iota(jnp.int32, (ts, ts), 0)
        c = lax.broadcasted_iota(jnp.int32, (ts, ts), 1)
        tri_ref[...] = jnp.where(c <= r, 1.0, 0.0).astype(BF16)
        row = lax.broadcasted_iota(jnp.int32, (N_SPLIT * w, wo), 0)
        lane = lax.broadcasted_iota(jnp.int32, (N_SPLIT * w, wo), 1)
        target = jnp.zeros_like(row)
        for idx in range(N_SPLIT):
            in_part = (row >= idx * w) & (row < (idx + 1) * w)
            target = jnp.where(in_part, (row - idx * w) * HEAD_DIM + idx, target)
        place_ref[...] = jnp.where(lane == target, 1.0, 0.0).astype(BF16)

    @pl.when(pl.program_id(1) == 0)
    def _():
        carry_ref[...] = jnp.zeros_like(carry_ref)

    x = z_ref[0] + b_ref[...]
    lf = jnp.minimum(x, 0.0) - jnp.log(1.0 + jnp.exp(-jnp.abs(x)))
    parts = jnp.dot(tri_ref[...], jnp.concatenate(_split3_bf16(lf), axis=1), preferred_element_type=F32)
    cs = carry_ref[...] + parts[:, :w]
    for idx in range(1, N_SPLIT):
        cs = cs + parts[:, idx * w:(idx + 1) * w]
    carry_ref[...] = cs[ts - 1:ts, :]

    split = jnp.concatenate(_split3_bf16(cs * LOG2E), axis=1)
    bias = jnp.dot(split, place_ref[...], preferred_element_type=F32)
    for hh in range(o_ref.shape[0]):
        o_ref[hh, 0] = bias[:, hh * HEAD_DIM:(hh + 1) * HEAD_DIM].astype(o_ref.dtype)


def _forget_bias(zf, bf_pad, n_heads):
    b, s, w = zf.shape
    ts = min(512, s)
    wo = n_heads * HEAD_DIM
    return pl.pallas_call(
        _fcum_kernel,
        grid=(b, s // ts),
        in_specs=[pl.BlockSpec((1, ts, w), lambda bi, i: (bi, i, 0)),
                  pl.BlockSpec((1, w), lambda bi, i: (0, 0))],
        out_specs=pl.BlockSpec((n_heads, 1, ts, HEAD_DIM), lambda bi, i: (0, bi, i, 0)),
        out_shape=jax.ShapeDtypeStruct((n_heads, b, s, HEAD_DIM), BF16),
        scratch_shapes=[pltpu.VMEM((1, w), F32), pltpu.VMEM((ts, ts), BF16), pltpu.VMEM((N_SPLIT * w, wo), BF16)],
        compiler_params=_cparams(("arbitrary", "arbitrary"), 4 * ts * ts * 4 + 8 * ts * wo * 4),
        name="forget_bias",
    )(zf, bf_pad)


ONES_ROWS = 16


MAX_CHAINS = 8
BIG_CHUNK = 4
STALE_MAX_GUARD = 60.0


def _attn_kernel(q_ref, k_ref, v_ref, fb_ref, o_ref, vt_ref, acc_ref, m_ref, *, tq):
    tk = tq
    s_len = k_ref.shape[1]
    step = pl.program_id(2)

    @pl.when(step == 0)
    def _():
        vt_ref[HEAD_DIM:, :] = jnp.ones((ONES_ROWS, s_len), BF16)
        for c in range(s_len // tk):
            vt_ref[:HEAD_DIM, c * tk:(c + 1) * tk] = v_ref[0, c * tk:(c + 1) * tk, :].astype(F32).T.astype(BF16)

    for sub in range(q_ref.shape[1] // tq):
        rows = slice(sub * tq, (sub + 1) * tq)
        o_ref[0, rows, :] = _attn_query_tile(step * (q_ref.shape[1] // tq) + sub, q_ref[0, rows, :], k_ref, fb_ref,
                                             vt_ref, acc_ref, m_ref).astype(o_ref.dtype)


def _attn_query_tile(qi, q, k_ref, fb_ref, vt_ref, acc_ref, m_ref):
    tq = q.shape[0]
    tk = tq
    lane = lax.broadcasted_iota(jnp.int32, (tq, HEAD_DIM), 1)
    minus_one = jnp.where(lane < 3, -1.0, 0.0).astype(BF16)
    q_aug = jnp.concatenate([q, minus_one], axis=1)
    m_ref[...] = jnp.full(m_ref.shape, MASK_VALUE, F32)
    acc_ref[...] = jnp.zeros_like(acc_ref)

    def logits(k0, size):
        k0 = pl.multiple_of(k0, tk)
        k_aug = jnp.concatenate([k_ref[0, pl.ds(k0, size), :], fb_ref[0, 0, pl.ds(k0, size), :]], axis=1)
        return lax.dot_general(k_aug, q_aug, (((1,), (1,)), ((), ())), preferred_element_type=F32)

    def absorb(st, k0, size, masked):
        k0 = pl.multiple_of(k0, tk)
        if masked:
            krow = lax.broadcasted_iota(jnp.int32, (size, tq), 0)
            qcol = lax.broadcasted_iota(jnp.int32, (size, tq), 1)
            st = jnp.where(krow <= qcol, st, MASK_VALUE)
        part = jnp.max(st.reshape(MAX_CHAINS, size // MAX_CHAINS, tq), axis=1)
        m_old = m_ref[...]
        m_new = jnp.maximum(m_old, jnp.max(part, axis=0, keepdims=True))
        m_ref[...] = m_new
        pt = jnp.exp2((st - m_new).astype(BF16))
        acc_ref[...] = jnp.exp2(m_old - m_new) * acc_ref[...] + jnp.dot(
            vt_ref[:, pl.ds(k0, size)], pt, preferred_element_type=F32)

    def absorb_one_pass(st, k0, size):
        k0 = pl.multiple_of(k0, tk)
        m_old = m_ref[...]
        part = jnp.max(st.reshape(MAX_CHAINS, size // MAX_CHAINS, tq), axis=1)
        cmax = jnp.max(part, axis=0, keepdims=True)
        pt = jnp.exp2(st - m_old).astype(BF16)
        pv = jnp.dot(vt_ref[:, pl.ds(k0, size)], pt, preferred_element_type=F32)
        safe = jnp.max(cmax - m_old) <= STALE_MAX_GUARD

        @pl.when(safe)
        def _():
            m_new = jnp.maximum(m_old, cmax)
            m_ref[...] = m_new
            acc_ref[...] = (acc_ref[...] + pv) * jnp.exp2(m_old - m_new)

        @pl.when(jnp.logical_not(safe))
        def _():
            absorb(logits(k0, size), k0, size, False)

    n_big = qi // BIG_CHUNK
    tail_k0 = n_big * (BIG_CHUNK * tk)
    for visible in range(BIG_CHUNK):
        @pl.when(qi - n_big * BIG_CHUNK == visible)
        def _(visible=visible):
            diag_k0 = tail_k0 + visible * tk
            if visible == 0:
                absorb(logits(diag_k0, tk), diag_k0, tk, True)
            else:
                st_a = logits(tail_k0, visible * tk)
                st_b = logits(diag_k0, tk)
                absorb(st_a, tail_k0, visible * tk, False)
                absorb(st_b, diag_k0, tk, True)

    def body(kc, carry):
        k0 = kc * (BIG_CHUNK * tk)
        absorb_one_pass(logits(k0, BIG_CHUNK * tk), k0, BIG_CHUNK * tk)
        return carry

    lax.fori_loop(0, n_big, body, 0)
    o_t = acc_ref[:HEAD_DIM, :] * (1.0 / acc_ref[HEAD_DIM:HEAD_DIM + 1, :])
    return o_t.T


def _attention(zqkv, fbias, batch, seq):
    n_heads = fbias.shape[0]
    b, s = batch, seq
    tq = min(512, s)
    tiles_per_step = 2 if s % (2 * tq) == 0 else 1
    tqs = tq * tiles_per_step
    spb = s // tqs
    assert s % tqs == 0
    vmem = 2 * (3 * s * HEAD_DIM * 2) + (HEAD_DIM + ONES_ROWS) * s * 2 + 6 * BIG_CHUNK * tq * tq * 4
    kv_spec = lambda off: pl.BlockSpec((1, s, HEAD_DIM), lambda bi, h, i: (off + h, bi, 0))
    return pl.pallas_call(
        functools.partial(_attn_kernel, tq=tq),
        grid=(b, n_heads, spb),
        in_specs=[pl.BlockSpec((1, tqs, HEAD_DIM), lambda bi, h, i: (h, bi * spb + i, 0)),
                  kv_spec(n_heads), kv_spec(2 * n_heads),
                  pl.BlockSpec((1, 1, s, HEAD_DIM), lambda bi, h, i: (h, bi, 0, 0))],
        out_specs=pl.BlockSpec((1, tqs, HEAD_DIM), lambda bi, h, i: (bi, i, h)),
        out_shape=jax.ShapeDtypeStruct((b, s, n_heads * HEAD_DIM), BF16),
        scratch_shapes=[pltpu.VMEM((HEAD_DIM + ONES_ROWS, s), BF16),
                        pltpu.VMEM((HEAD_DIM + ONES_ROWS, tq), F32),
                        pltpu.VMEM((1, tq), F32)],
        compiler_params=_cparams(("parallel", "parallel", "arbitrary"), vmem),
        name="fox_attention",
    )(zqkv, zqkv, zqkv, fbias)


def _pool_kernel(u_ref, halo_ref, w_ref, sc_ref, o_ref):
    tm = u_ref.shape[1]
    i = pl.program_id(1)
    halo = jnp.where(i > 0, halo_ref[0], 0.0)
    ext = jnp.concatenate([halo, u_ref[0]], axis=0)
    pos = (i * tm + 1 + lax.broadcasted_iota(jnp.int32, (tm, 1), 0)).astype(F32)
    gd = LANES
    for g, win in enumerate(POOL_WINDOWS):
        e = ext[:, g * gd:(g + 1) * gd]
        ssum = e
        shift = 1
        while shift < win:
            ssum = ssum + pltpu.roll(ssum, shift, 0)
            shift *= 2
        mean = ssum[POOL_HALO:] * (1.0 / jnp.minimum(pos, float(win)))
        dlt = mean - e[POOL_HALO:]
        y = jnp.dot(dlt.astype(BF16), w_ref[0, g], preferred_element_type=F32)
        o_ref[0, :, g * gd:(g + 1) * gd] = (y * sc_ref[:, g * gd:(g + 1) * gd]).astype(o_ref.dtype)


def _pool_mixer(z32, pool_w_stack, layer, scale_row, width):
    b, s, _ = z32.shape
    tm = min(512, s)
    hb = tm // POOL_HALO
    return pl.pallas_call(
        _pool_kernel,
        grid=(b, s // tm),
        in_specs=[pl.BlockSpec((1, tm, width), lambda bi, i: (bi, i, 0)),
                  pl.BlockSpec((1, POOL_HALO, width), lambda bi, i: (bi, jnp.maximum(i * hb - 1, 0), 0)),
                  pl.BlockSpec((1, N_POOL_GROUPS, LANES, LANES), lambda bi, i: (layer, 0, 0, 0)),
                  pl.BlockSpec((1, width), lambda bi, i: (0, 0))],
        out_specs=pl.BlockSpec((1, tm, width), lambda bi, i: (bi, i, 0)),
        out_shape=jax.ShapeDtypeStruct((b, s, width), BF16),
        compiler_params=_cparams(("parallel", "parallel"), 16 * tm * width * 4),
        name="pool_mixer",
    )(z32, z32, pool_w_stack, scale_row)


def _lru_kernel(x_ref, halo_ref, y_ref, cw_ref, cb_ref, wa_ref, ba_ref, wi_ref, bi_ref, lam_ref,
                o_ref, h_ref, a_s, b_s, h_s):
    tm = x_ref.shape[1]
    i = pl.program_id(1)

    @pl.when(i == 0)
    def _():
        h_ref[...] = jnp.zeros_like(h_ref)

    halo = jnp.where(i > 0, halo_ref[0], 0.0)
    ext = jnp.concatenate([halo, x_ref[0]], axis=0)
    xc = cb_ref[...] + cw_ref[LRU_CONV_WIDTH - 1:LRU_CONV_WIDTH, :] * ext[SUBLANES:]
    for k in range(LRU_CONV_WIDTH - 1):
        shifted = pltpu.roll(ext, LRU_CONV_WIDTH - 1 - k, 0)[SUBLANES:]
        xc = xc + cw_ref[k:k + 1, :] * shifted

    lam = lam_ref[...]
    neg_softplus = -(jnp.maximum(-lam, 0.0) + jnp.log(1.0 + jnp.exp(-jnp.abs(lam))))
    gd = LANES
    n_groups = tm // SUBLANES
    sub_row = lax.broadcasted_iota(jnp.int32, (n_groups, SUBLANES, gd), 1)
    for blk in range(N_LRU_BLOCKS):
        sl = slice(blk * gd, (blk + 1) * gd)
        xb = xc[:, sl]
        xb16 = xb.astype(BF16)
        gate_r = _sigmoid(jnp.dot(xb16, wa_ref[0, blk], preferred_element_type=F32) + ba_ref[:, sl])
        gate_i = _sigmoid(jnp.dot(xb16, wi_ref[0, blk], preferred_element_type=F32) + bi_ref[:, sl])
        log_a = LRU_C * gate_r * neg_softplus[:, sl]
        a = jnp.exp(log_a)
        b = jnp.sqrt(1.0 - a * a) * (gate_i * xb)
        a = a.reshape(n_groups, SUBLANES, gd)
        b = b.reshape(n_groups, SUBLANES, gd)
        dist = 1
        while dist < SUBLANES:
            has_prev = sub_row >= dist
            b = b + a * jnp.where(has_prev, pltpu.roll(b, dist, 1), 0.0)
            a = a * jnp.where(has_prev, pltpu.roll(a, dist, 1), 1.0)
            dist *= 2
        a_s[blk] = a
        b_s[blk] = b

    def group(gi, hs):
        out = []
        for blk in range(N_LRU_BLOCKS):
            rows = b_s[blk, gi] + a_s[blk, gi] * hs[blk]
            h_s[blk, gi] = rows
            out.append(rows[SUBLANES - 1:SUBLANES, :])
        return tuple(out)

    hs = lax.fori_loop(0, n_groups, group,
                       tuple(h_ref[:, blk * gd:(blk + 1) * gd] for blk in range(N_LRU_BLOCKS)), unroll=8)

    for blk in range(N_LRU_BLOCKS):
        sl = slice(blk * gd, (blk + 1) * gd)
        h_ref[:, sl] = hs[blk]
        y = y_ref[0, :, sl]
        gelu = 0.5 * y * (1.0 + jnp.tanh(0.7978845608028654 * (y + 0.044715 * (y * y * y))))
        o_ref[0, :, sl] = (h_s[blk].reshape(tm, gd) * gelu).astype(o_ref.dtype)


def _lru_mixer(z32, layer, p, width, x_col, y_col):
    b, s, _ = z32.shape
    tm = min(512, s)
    hb = tm // SUBLANES
    row = lambda bi, i: (0, 0)
    return pl.pallas_call(
        _lru_kernel,
        grid=(b, s // tm),
        in_specs=[pl.BlockSpec((1, tm, width), lambda bi, i: (bi, i, x_col)),
                  pl.BlockSpec((1, SUBLANES, width), lambda bi, i: (bi, jnp.maximum(i * hb - 1, 0), x_col)),
                  pl.BlockSpec((1, tm, width), lambda bi, i: (bi, i, y_col)),
                  pl.BlockSpec((LRU_CONV_WIDTH, width), row),
                  pl.BlockSpec((1, width), row),
                  pl.BlockSpec((1, N_LRU_BLOCKS, LANES, LANES), lambda bi, i: (layer, 0, 0, 0)),
                  pl.BlockSpec((1, width), row),
                  pl.BlockSpec((1, N_LRU_BLOCKS, LANES, LANES), lambda bi, i: (layer, 0, 0, 0)),
                  pl.BlockSpec((1, width), row),
                  pl.BlockSpec((1, width), row)],
        out_specs=pl.BlockSpec((1, tm, width), lambda bi, i: (bi, i, 0)),
        out_shape=jax.ShapeDtypeStruct((b, s, width), BF16),
        scratch_shapes=[pltpu.VMEM((1, width), F32)]
        + [pltpu.VMEM((N_LRU_BLOCKS, tm // SUBLANES, SUBLANES, LANES), F32)] * 3,
        compiler_params=_cparams(("arbitrary", "arbitrary"), 24 * tm * width * 4),
        name="rg_lru",
    )(z32, z32, z32, p["conv_w"], p["conv_b"], p["wa"], p["ba"], p["wi"], p["bi"], p["lam"])


COL_BLOCK = 512


def _col_blocks(d):
    return [slice(c, c + COL_BLOCK) for c in range(0, d, COL_BLOCK)]


def _residual_cols(acc, cols, x_ref, gt_ref, xn_refs):
    xn = x_ref[:, cols] + gt_ref[0, :, cols] * acc
    for ref in xn_refs:
        ref[:, cols] = xn
    return jnp.sum(xn * xn, axis=-1, keepdims=True)


def _norm_rows(sumsq, xn_ref, g_ref, sc_ref, sh_ref, h_ref):
    rs = lax.rsqrt(sumsq * (1.0 / xn_ref.shape[1]) + EPS)
    gain = g_ref[...] * (1.0 + sc_ref[0])
    h_ref[...] = (xn_ref[...] * rs * gain + sh_ref[0]).astype(h_ref.dtype)


def _mix_out_kernel(yp_ref, ya_ref, yl_ref, w_ref, x_ref, gt_ref, g_ref, sc_ref, sh_ref, xo_ref, h_ref, w16_ref):
    kp = yp_ref.shape[1]
    ka = ya_ref.shape[1]

    @pl.when(pl.program_id(0) == 0)
    def _():
        w16_ref[...] = w_ref[0].astype(BF16)

    sumsq = jnp.zeros((x_ref.shape[0], 1), F32)
    for cols in _col_blocks(x_ref.shape[1]):
        acc = jnp.dot(yp_ref[...], w16_ref[0:kp, cols], preferred_element_type=F32)
        acc = acc + jnp.dot(ya_ref[...], w16_ref[kp:kp + ka, cols], preferred_element_type=F32)
        acc = acc + jnp.dot(yl_ref[...], w16_ref[kp + ka:, cols], preferred_element_type=F32)
        sumsq = sumsq + _residual_cols(acc, cols, x_ref, gt_ref, (xo_ref,))
    _norm_rows(sumsq, xo_ref, g_ref, sc_ref, sh_ref, h_ref)


def _mix_out(yp, ya, yl, w_stack, layer, x2, gt, g, sc, sh, seq):
    t, d = x2.shape
    tm = min(512, seq)
    tpb = seq // tm
    per_b = lambda i: (i // tpb, 0, 0)
    lhs = lambda y: pl.BlockSpec((tm, y.shape[1]), lambda i: (i, 0))
    vmem = d * d * 6 + 2 * (tm * d * 2 + 3 * tm * d * 4) + 3 * tm * d * 4
    return pl.pallas_call(
        _mix_out_kernel,
        grid=(t // tm,),
        in_specs=[lhs(yp), lhs(ya), lhs(yl),
                  pl.BlockSpec((1, d, d), lambda i: (layer, 0, 0), pipeline_mode=pl.Buffered(1)),
                  pl.BlockSpec((tm, d), lambda i: (i, 0)),
                  pl.BlockSpec((1, 1, d), per_b),
                  pl.BlockSpec((1, d), lambda i: (0, 0)),
                  pl.BlockSpec((1, 1, d), per_b),
                  pl.BlockSpec((1, 1, d), per_b)],
        out_specs=[pl.BlockSpec((tm, d), lambda i: (i, 0)), pl.BlockSpec((tm, d), lambda i: (i, 0))],
        out_shape=[jax.ShapeDtypeStruct((t, d), F32), jax.ShapeDtypeStruct((t, d), BF16)],
        scratch_shapes=[pltpu.VMEM((d, d), BF16)],
        compiler_params=_cparams(("arbitrary",), vmem),
        name="mix_out_proj",
    )(yp, ya, yl, w_stack, x2, gt, g, sc, sh)


def _ffn_down_kernel(a_ref, w_ref, x_ref, gt_ref, g_ref, sc_ref, sh_ref, *out_refs):
    xn_ref, h_ref = out_refs[0], out_refs[-1]
    sumsq = jnp.zeros((x_ref.shape[0], 1), F32)
    for cols in _col_blocks(x_ref.shape[1]):
        acc = jnp.dot(a_ref[...], w_ref[:, cols], preferred_element_type=F32)
        sumsq = sumsq + _residual_cols(acc, cols, x_ref, gt_ref, (xn_ref,))
    _norm_rows(sumsq, xn_ref, g_ref, sc_ref, sh_ref, h_ref)


def _ffn_down(act, w16, x2, gt, g, sc, sh, seq, emit_x, h_dtype):
    t, d = x2.shape
    kf = act.shape[1]
    tm = min(512, seq)
    tpb = seq // tm
    per_b = lambda i: (i // tpb, 0, 0)
    row_tile = pl.BlockSpec((tm, d), lambda i: (i, 0))
    out_specs = [row_tile]
    out_shape = [jax.ShapeDtypeStruct((t, d), h_dtype)]
    if emit_x:
        out_specs = [row_tile, row_tile]
        out_shape = [jax.ShapeDtypeStruct((t, d), F32)] + out_shape
    else:
        assert h_dtype == F32
    vmem = kf * d * 2 + 2 * (tm * kf * 2 + 3 * tm * d * 4) + 2 * tm * COL_BLOCK * 4
    return pl.pallas_call(
        _ffn_down_kernel,
        grid=(t // tm,),
        in_specs=[pl.BlockSpec((tm, kf), lambda i: (i, 0)),
                  pl.BlockSpec((kf, d), lambda i: (0, 0), pipeline_mode=pl.Buffered(1)),
                  row_tile,
                  pl.BlockSpec((1, 1, d), per_b),
                  pl.BlockSpec((1, d), lambda i: (0, 0)),
                  pl.BlockSpec((1, 1, d), per_b),
                  pl.BlockSpec((1, 1, d), per_b)],
        out_specs=out_specs,
        out_shape=out_shape,
        compiler_params=_cparams(("parallel",), vmem),
        name="ffn_down_proj",
    )(act, w16, x2, gt, g, sc, sh)


def _ffn_gate_kernel(h_ref, wg_ref, wu_ref, cw_ref, cb_ref, wd_ref, o_ref, wd16_ref, carry_ref, *, tiles_per_batch):
    tm = h_ref.shape[0]
    i = pl.program_id(0)
    j = pl.program_id(1)

    @pl.when(i % tiles_per_batch == 0)
    def _():
        carry_ref[j] = jnp.zeros(carry_ref.shape[1:], F32)

    @pl.when(i == 0)
    def _():
        wd16_ref[...] = wd_ref[0].astype(BF16)

    h = h_ref[...]
    u = jnp.dot(h, wg_ref[0].astype(BF16), preferred_element_type=F32)
    ext = jnp.concatenate([carry_ref[j], u], axis=0)
    carry_ref[j] = u[tm - SUBLANES:, :]
    g = cb_ref[...] + cw_ref[FFN_CONV_WIDTH - 1:FFN_CONV_WIDTH, :] * u
    for k in range(FFN_CONV_WIDTH - 1):
        shifted = pltpu.roll(ext, FFN_CONV_WIDTH - 1 - k, 0)[SUBLANES:]
        g = g + cw_ref[k:k + 1, :] * shifted
    up = jnp.dot(h, wu_ref[0].astype(BF16), preferred_element_type=F32)
    o_ref[...] = (g * _sigmoid(g) * up).astype(o_ref.dtype)


def _ffn_gate(h, wg_stack, wu_stack, wd_stack, layer, conv_w, conv_b, seq):
    t, d = h.shape
    kf = wg_stack.shape[2]
    tm = min(1024, seq)
    tf = 512
    nj = kf // tf
    wspec = pl.BlockSpec((1, d, tf), lambda i, j: (layer, 0, j))
    slab = lambda i, j: jnp.where(i == 0, j, nj - 1)
    vmem = 2 * (tm * d * 2 + 2 * d * tf * 4 + tm * tf * 2 + tf * d * 6) + 2 * d * tf * 2 + 8 * tm * tf * 4
    return pl.pallas_call(
        functools.partial(_ffn_gate_kernel, tiles_per_batch=seq // tm),
        grid=(t // tm, nj),
        in_specs=[pl.BlockSpec((tm, d), lambda i, j: (i, 0)), wspec, wspec,
                  pl.BlockSpec((FFN_CONV_WIDTH, tf), lambda i, j: (0, j)),
                  pl.BlockSpec((1, tf), lambda i, j: (0, j)),
                  pl.BlockSpec((1, tf, d), lambda i, j: (layer, slab(i, j), 0))],
        out_specs=[pl.BlockSpec((tm, tf), lambda i, j: (i, j)),
                   pl.BlockSpec((tf, d), lambda i, j: (slab(i, j), 0))],
        out_shape=[jax.ShapeDtypeStruct((t, kf), BF16), jax.ShapeDtypeStruct((kf, d), BF16)],
        scratch_shapes=[pltpu.VMEM((nj, SUBLANES, tf), F32)],
        compiler_params=_cparams(("arbitrary", "arbitrary"), vmem),
        name="ffn_gate_up",
    )(h, wg_stack, wu_stack, conv_w, conv_b, wd_stack)


def kernel(x, c, w_ada, b_ada, g_mix, w_in, b_f, pool_w, pool_scale, lru_conv_w, lru_conv_b, lru_wa, lru_ba,
           lru_wi, lru_bi, lru_lambda, w_out, g_ffn, w_ffn_gate, w_ffn_up, ffn_conv_w, ffn_conv_b, w_ffn_down,
           final_g):
    batch, seq, d = x.shape
    depth = w_ada.shape[0]
    pool_width = pool_w.shape[1] * pool_w.shape[2]
    lru_width = lru_lambda.shape[1]
    n_heads = b_f.shape[1]
    attn_width = n_heads * HEAD_DIM
    assert w_in.shape[2] == pool_width + 3 * attn_width + n_heads + 2 * lru_width
    assert pool_width == lru_width == N_POOL_GROUPS * LANES and seq % SUBLANES == 0

    o_q = pool_width
    o_f = o_q + 3 * attn_width
    o_x = o_f + n_heads
    w_in_t = jnp.swapaxes(w_in, 1, 2)
    qkv_width = 3 * attn_width
    pxy_width = pool_width + 2 * lru_width
    proj_windows = [(0, pool_width), (o_x, 2 * lru_width), (o_f, LANES)]
    assert o_f + LANES <= w_in.shape[2] and o_f % SUBLANES == 0 and o_x % SUBLANES == 0
    bf_pad = jnp.pad(b_f, ((0, 0), (0, LANES - n_heads)))
    qkv_scale = jnp.concatenate([jnp.full((1, attn_width), HEAD_DIM ** -0.5 * LOG2E, F32),
                                 jnp.ones((1, 2 * attn_width), F32)], axis=1)
    pool_w16 = pool_w.astype(BF16)
    lru_wa16 = lru_wa.astype(BF16)
    lru_wi16 = lru_wi.astype(BF16)

    mod = _ada_mod(c, w_ada, b_ada)[:, :batch]

    def mod_chunk(layer, idx):
        return mod[layer, :, idx * d:(idx + 1) * d].reshape(batch, 1, d)

    x2 = x.reshape(batch * seq, d)
    h = _norm(x2, g_mix[0][None], mod_chunk(0, 1), mod_chunk(0, 0), seq)
    out = None
    for layer in range(depth):
        sh1, sc1, gt1, sh2, sc2, gt2 = (mod_chunk(layer, idx) for idx in range(6))
        zqkv = _matmul_heads(h, w_in_t, layer, o_q, qkv_width, qkv_width // 2, qkv_scale)
        z32, zf = _matmul_windows(h, w_in_t, layer, proj_windows)
        z32 = z32.reshape(batch, seq, pxy_width)

        fbias = _forget_bias(zf.reshape(batch, seq, LANES), bf_pad[layer][None], n_heads)
        y_attn = _attention(zqkv, fbias, batch, seq).reshape(batch * seq, attn_width)
        y_pool = _pool_mixer(z32, pool_w16, layer, pool_scale[layer][None], pool_width)
        lru_p = dict(conv_w=lru_conv_w[layer], conv_b=lru_conv_b[layer][None], wa=lru_wa16,
                     ba=lru_ba[layer][None], wi=lru_wi16, bi=lru_bi[layer][None], lam=lru_lambda[layer][None])
        y_lru = _lru_mixer(z32, layer, lru_p, lru_width, 1, 2)

        x2, h = _mix_out(y_pool.reshape(batch * seq, pool_width), y_attn, y_lru.reshape(batch * seq, lru_width),
                         w_out, layer, x2, gt1, g_ffn[layer][None], sc2, sh2, seq)
        act, w_down16 = _ffn_gate(h, w_ffn_gate, w_ffn_up, w_ffn_down, layer, ffn_conv_w[layer],
                                  ffn_conv_b[layer][None], seq)
        if layer + 1 < depth:
            x2, h = _ffn_down(act, w_down16, x2, gt2, g_mix[layer + 1][None], mod_chunk(layer + 1, 1),
                              mod_chunk(layer + 1, 0), seq, True, BF16)
        else:
            zeros = jnp.zeros((batch, 1, d), F32)
            (out,) = _ffn_down(act, w_down16, x2, gt2, final_g[None], zeros, zeros, seq, False, F32)
    return out.reshape(batch, seq, d)
```

```python
import functools

import jax
import jax.numpy as jnp
from jax import lax
from jax.experimental import pallas as pl
from jax.experimental.pallas import tpu as pltpu

F32 = jnp.float32
BF16 = jnp.bfloat16

HEAD_DIM = 128
N_POOL_GROUPS = 4
POOL_WINDOWS = (2, 4, 8, 16)
POOL_HALO = 16
N_LRU_BLOCKS = 4
LRU_CONV_WIDTH = 4
LRU_C = 8.0
FFN_CONV_WIDTH = 3
EPS = 1e-6

LANES = 128
SUBLANES = 8
VMEM_LIMIT_CAP_V7X = 58 * 2**20
MASK_VALUE = -1e30
LOG2E = 1.4426950408889634


def _cparams(semantics, vmem_bytes):
    return pltpu.CompilerParams(dimension_semantics=semantics,
                                vmem_limit_bytes=int(min(VMEM_LIMIT_CAP_V7X, max(vmem_bytes, 16 * 2**20))))


def _sigmoid(x):
    return 0.5 * jnp.tanh(0.5 * x) + 0.5


def _norm_mod(x, g, sc, sh):
    ms = jnp.mean(x * x, axis=-1, keepdims=True)
    return (x * lax.rsqrt(ms + EPS) * g) * (1.0 + sc) + sh


def _split3_bf16(x):
    hi = x.astype(BF16)
    r1 = x - hi.astype(F32)
    mid = r1.astype(BF16)
    lo = (r1 - mid.astype(F32)).astype(BF16)
    return hi, mid, lo


def _ada_kernel(c_ref, w_ref, b_ref, o_ref, acc_ref):
    nb = c_ref.shape[0]
    n = w_ref.shape[3]
    k = pl.program_id(1)

    @pl.when(k == 0)
    def _():
        acc_ref[...] = jnp.zeros_like(acc_ref)

    ca = [c_ref[b] * _sigmoid(c_ref[b]) for b in range(nb)]

    def lane_chunk(jc, carry):
        sl = pl.ds(pl.multiple_of(jc * LANES, LANES), LANES)
        w = w_ref[0, :, :, sl]
        for b in range(nb):
            acc_ref[b, :, sl] += jnp.sum(w * ca[b], axis=0)
        return carry

    lax.fori_loop(0, n // LANES, lane_chunk, 0, unroll=4)

    @pl.when(k == pl.num_programs(1) - 1)
    def _():
        o_ref[...] = jnp.zeros_like(o_ref)
        for b in range(nb):
            o_ref[0, b:b + 1, :] = jnp.sum(acc_ref[b], axis=0, keepdims=True) + b_ref[0]


ADA_SLAB_ROWS = 256


def _ada_mod(c, w_ada, b_ada):
    depth, d, n = w_ada.shape
    batch = c.shape[0]
    dr = d // SUBLANES
    sr = ADA_SLAB_ROWS // SUBLANES
    c_lanes = jnp.broadcast_to(c[:, :, None], (batch, d, LANES)).reshape(batch, dr, SUBLANES, LANES)
    return pl.pallas_call(
        _ada_kernel,
        grid=(depth, dr // sr),
        in_specs=[pl.BlockSpec((batch, sr, SUBLANES, LANES), lambda l, k: (0, k, 0, 0)),
                  pl.BlockSpec((1, sr, SUBLANES, n), lambda l, k: (l, k, 0, 0)),
                  pl.BlockSpec((1, 1, n), lambda l, k: (l, 0, 0))],
        out_specs=pl.BlockSpec((1, SUBLANES, n), lambda l, k: (l, 0, 0)),
        out_shape=jax.ShapeDtypeStruct((depth, SUBLANES, n), F32),
        scratch_shapes=[pltpu.VMEM((batch, SUBLANES, n), F32)],
        compiler_params=_cparams(("parallel", "arbitrary"), 3 * ADA_SLAB_ROWS * n * 4),
        name="ada_mod",
    )(c_lanes, w_ada.reshape(depth, dr, SUBLANES, n), b_ada.reshape(depth, 1, n))


_NT_DIMS = (((1,), (1,)), ((), ()))


def _lhs_rows(a_ref, refs, fused_norm):
    if not fused_norm:
        return a_ref[...], refs
    g_ref, sc_ref, sh_ref = refs[:3]
    return _norm_mod(a_ref[...], g_ref[...], sc_ref[0], sh_ref[0]).astype(BF16), refs[3:]


def _norm_specs(norm, tm, k, n_grid_axes):
    if norm is None:
        return [], []
    g, sc, sh, seq = norm
    tpb = seq // tm
    per_b = (lambda i: (i // tpb, 0, 0)) if n_grid_axes == 1 else (lambda i, j: (i // tpb, 0, 0))
    const = (lambda i: (0, 0)) if n_grid_axes == 1 else (lambda i, j: (0, 0))
    return [g, sc, sh], [pl.BlockSpec((1, k), const), pl.BlockSpec((1, 1, k), per_b), pl.BlockSpec((1, 1, k), per_b)]


def _mm_windows_kernel(a_ref, *refs, fused_norm):
    a, refs = _lhs_rows(a_ref, refs, fused_norm)
    n_w = len(refs) - 2
    w_refs, o_ref, on_ref = refs[:n_w], refs[n_w], refs[n_w + 1]
    col = 0
    for w_ref in w_refs[:-1]:
        width = w_ref.shape[1]
        o_ref[:, col:col + width] = lax.dot_general(a, w_ref[0].astype(BF16), _NT_DIMS, preferred_element_type=F32)
        col += width
    on_ref[...] = lax.dot_general(a, w_refs[-1][0].astype(BF16), _NT_DIMS, preferred_element_type=F32)


def _mm_heads_kernel(a_ref, *refs, fused_norm):
    a, (w_ref, cs_ref, o_ref) = _lhs_rows(a_ref, refs, fused_norm)
    acc = lax.dot_general(a, w_ref[0].astype(BF16), _NT_DIMS, preferred_element_type=F32) * cs_ref[...]
    for hh in range(o_ref.shape[0]):
        o_ref[hh] = acc[:, hh * HEAD_DIM:(hh + 1) * HEAD_DIM].astype(o_ref.dtype)


def _matmul_windows(a, wt_stack, layer, windows, norm=None):
    t, k = a.shape
    tm = min(1024, t)
    wide = sum(n for _, n in windows[:-1])
    narrow = windows[-1][1]
    rows = wide + narrow
    vmem = rows * k * 6 + 2 * (tm * k * a.dtype.itemsize + tm * rows * 4) + 2 * tm * wide * 4
    w_specs = [pl.BlockSpec((pl.Element(1), pl.Element(n), pl.Element(k)), lambda i, r=r: (layer, r, 0),
                            pipeline_mode=pl.Buffered(1))
               for r, n in windows]
    norm_args, norm_specs = _norm_specs(norm, tm, k, 1)
    return pl.pallas_call(
        functools.partial(_mm_windows_kernel, fused_norm=norm is not None),
        grid=(t // tm,),
        in_specs=[pl.BlockSpec((tm, k), lambda i: (i, 0))] + norm_specs + w_specs,
        out_specs=[pl.BlockSpec((tm, wide), lambda i: (i, 0)), pl.BlockSpec((tm, narrow), lambda i: (i, 0))],
        out_shape=[jax.ShapeDtypeStruct((t, wide), F32), jax.ShapeDtypeStruct((t, narrow), F32)],
        compiler_params=_cparams(("parallel",), vmem),
        name="matmul_dmodel",
    )(a, *norm_args, *([wt_stack] * len(windows)))


def _matmul_heads(a, wt_stack, layer, row0, n, tn, col_scale, norm=None):
    t, k = a.shape
    tm = min(1024, t)
    hpb = tn // HEAD_DIM
    vmem = 2 * (tm * k * a.dtype.itemsize + k * tn * 4 + tm * tn * 2) + k * tn * 2 + 2 * tm * tn * 4
    norm_args, norm_specs = _norm_specs(norm, tm, k, 2)
    return pl.pallas_call(
        functools.partial(_mm_heads_kernel, fused_norm=norm is not None),
        grid=(t // tm, n // tn),
        in_specs=[pl.BlockSpec((tm, k), lambda i, j: (i, 0))] + norm_specs + [
                  pl.BlockSpec((pl.Element(1), pl.Element(tn), pl.Element(k)),
                               lambda i, j: (layer, pl.multiple_of(row0 + j * tn, HEAD_DIM), 0)),
                  pl.BlockSpec((1, tn), lambda i, j: (0, j))],
        out_specs=pl.BlockSpec((hpb, tm, HEAD_DIM), lambda i, j: (j, i, 0)),
        out_shape=jax.ShapeDtypeStruct((n // HEAD_DIM, t, HEAD_DIM), BF16),
        compiler_params=_cparams(("parallel", "parallel"), vmem),
        name="matmul_heads",
    )(a, *norm_args, wt_stack, col_scale)


N_SPLIT = 3


def _fcum_kernel(z_ref, b_ref, o_ref, carry_ref, tri_ref, place_ref):
    ts = z_ref.shape[1]
    w = z_ref.shape[2]
    wo = place_ref.shape[1]

    @pl.when((pl.program_id(0) == 0) & (pl.program_id(1) == 0))
    def _():
        r = lax.broadcasted_iota(jnp.int32, (ts, ts), 0)
        c = lax.broadcasted_iota(jnp.int32, (ts, ts), 1)
        tri_ref[...] = jnp.where(c <= r, 1.0, 0.0).astype(BF16)
        row = lax.broadcasted_iota(jnp.int32, (N_SPLIT * w, wo), 0)
        lane = lax.broadcasted_iota(jnp.int32, (N_SPLIT * w, wo), 1)
        target = jnp.zeros_like(row)
        for idx in range(N_SPLIT):
            in_part = (row >= idx * w) & (row < (idx + 1) * w)
            target = jnp.where(in_part, (row - idx * w) * HEAD_DIM + idx, target)
        place_ref[...] = jnp.where(lane == target, 1.0, 0.0).astype(BF16)

    @pl.when(pl.program_id(1) == 0)
    def _():
        carry_ref[...] = jnp.zeros_like(carry_ref)

    x = z_ref[0] + b_ref[...]
    lf = jnp.minimum(x, 0.0) - jnp.log(1.0 + jnp.exp(-jnp.abs(x)))
    parts = jnp.dot(tri_ref[...], jnp.concatenate(_split3_bf16(lf), axis=1), preferred_element_type=F32)
    cs = carry_ref[...] + parts[:, :w]
    for idx in range(1, N_SPLIT):
        cs = cs + parts[:, idx * w:(idx + 1) * w]
    carry_ref[...] = cs[ts - 1:ts, :]

    split = jnp.concatenate(_split3_bf16(cs * LOG2E), axis=1)
    bias = jnp.dot(split, place_ref[...], preferred_element_type=F32)
    for hh in range(o_ref.shape[0]):
        o_ref[hh, 0] = bias[:, hh * HEAD_DIM:(hh + 1) * HEAD_DIM].astype(o_ref.dtype)


def _forget_bias(zf, bf_pad, n_heads):
    b, s, w = zf.shape
    ts = min(512, s)
    wo = n_heads * HEAD_DIM
    return pl.pallas_call(
        _fcum_kernel,
        grid=(b, s // ts),
        in_specs=[pl.BlockSpec((1, ts, w), lambda bi, i: (bi, i, 0)),
                  pl.BlockSpec((1, w), lambda bi, i: (0, 0))],
        out_specs=pl.BlockSpec((n_heads, 1, ts, HEAD_DIM), lambda bi, i: (0, bi, i, 0)),
        out_shape=jax.ShapeDtypeStruct((n_heads, b, s, HEAD_DIM), BF16),
        scratch_shapes=[pltpu.VMEM((1, w), F32), pltpu.VMEM((ts, ts), BF16), pltpu.VMEM((N_SPLIT * w, wo), BF16)],
        compiler_params=_cparams(("arbitrary", "arbitrary"), 4 * ts * ts * 4 + 8 * ts * wo * 4),
        name="forget_bias",
    )(zf, bf_pad)


ONES_ROWS = 16


MAX_CHAINS = 8
BIG_CHUNK = 4
STALE_MAX_GUARD = 60.0


def _attn_kernel(q_ref, k_ref, v_ref, fb_ref, o_ref, vt_ref, acc_ref, m_ref, *, tq):
    tk = tq
    s_len = k_ref.shape[1]
    step = pl.program_id(2)

    @pl.when(step == 0)
    def _():
        vt_ref[HEAD_DIM:, :] = jnp.ones((ONES_ROWS, s_len), BF16)
        for c in range(s_len // tk):
            vt_ref[:HEAD_DIM, c * tk:(c + 1) * tk] = v_ref[0, c * tk:(c + 1) * tk, :].astype(F32).T.astype(BF16)

    for sub in range(q_ref.shape[1] // tq):
        rows = slice(sub * tq, (sub + 1) * tq)
        o_ref[0, rows, :] = _attn_query_tile(step * (q_ref.shape[1] // tq) + sub, q_ref[0, rows, :], k_ref, fb_ref,
                                             vt_ref, acc_ref, m_ref).astype(o_ref.dtype)


def _attn_query_tile(qi, q, k_ref, fb_ref, vt_ref, acc_ref, m_ref):
    tq = q.shape[0]
    tk = tq
    lane = lax.broadcasted_iota(jnp.int32, (tq, HEAD_DIM), 1)
    minus_one = jnp.where(lane < 3, -1.0, 0.0).astype(BF16)
    q_aug = jnp.concatenate([q, minus_one], axis=1)
    m_ref[...] = jnp.full(m_ref.shape, MASK_VALUE, F32)
    acc_ref[...] = jnp.zeros_like(acc_ref)

    def logits(k0, size):
        k0 = pl.multiple_of(k0, tk)
        k_aug = jnp.concatenate([k_ref[0, pl.ds(k0, size), :], fb_ref[0, 0, pl.ds(k0, size), :]], axis=1)
        return lax.dot_general(k_aug, q_aug, (((1,), (1,)), ((), ())), preferred_element_type=F32)

    def absorb(st, k0, size, masked):
        k0 = pl.multiple_of(k0, tk)
        if masked:
            krow = lax.broadcasted_iota(jnp.int32, (size, tq), 0)
            qcol = lax.broadcasted_iota(jnp.int32, (size, tq), 1)
            st = jnp.where(krow <= qcol, st, MASK_VALUE)
        part = jnp.max(st.reshape(MAX_CHAINS, size // MAX_CHAINS, tq), axis=1)
        m_old = m_ref[...]
        m_new = jnp.maximum(m_old, jnp.max(part, axis=0, keepdims=True))
        m_ref[...] = m_new
        pt = jnp.exp2((st - m_new).astype(BF16))
        acc_ref[...] = jnp.exp2(m_old - m_new) * acc_ref[...] + jnp.dot(
            vt_ref[:, pl.ds(k0, size)], pt, preferred_element_type=F32)

    def absorb_one_pass(st, k0, size):
        k0 = pl.multiple_of(k0, tk)
        m_old = m_ref[...]
        part = jnp.max(st.reshape(MAX_CHAINS, size // MAX_CHAINS, tq), axis=1)
        cmax = jnp.max(part, axis=0, keepdims=True)
        pt = jnp.exp2(st - m_old).astype(BF16)
        pv = jnp.dot(vt_ref[:, pl.ds(k0, size)], pt, preferred_element_type=F32)
        safe = jnp.max(cmax - m_old) <= STALE_MAX_GUARD

        @pl.when(safe)
        def _():
            m_new = jnp.maximum(m_old, cmax)
            m_ref[...] = m_new
            acc_ref[...] = (acc_ref[...] + pv) * jnp.exp2(m_old - m_new)

        @pl.when(jnp.logical_not(safe))
        def _():
            absorb(logits(k0, size), k0, size, False)

    n_big = qi // BIG_CHUNK
    tail_k0 = n_big * (BIG_CHUNK * tk)
    for visible in range(BIG_CHUNK):
        @pl.when(qi - n_big * BIG_CHUNK == visible)
        def _(visible=visible):
            diag_k0 = tail_k0 + visible * tk
            if visible == 0:
                absorb(logits(diag_k0, tk), diag_k0, tk, True)
            else:
                st_a = logits(tail_k0, visible * tk)
                st_b = logits(diag_k0, tk)
                absorb(st_a, tail_k0, visible * tk, False)
                absorb(st_b, diag_k0, tk, True)

    def body(kc, carry):
        k0 = kc * (BIG_CHUNK * tk)
        absorb_one_pass(logits(k0, BIG_CHUNK * tk), k0, BIG_CHUNK * tk)
        return carry

    lax.fori_loop(0, n_big, body, 0)
    o_t = acc_ref[:HEAD_DIM, :] * (1.0 / acc_ref[HEAD_DIM:HEAD_DIM + 1, :])
    return o_t.T


def _attention(zqkv, fbias, batch, seq):
    n_heads = fbias.shape[0]
    b, s = batch, seq
    tq = min(512, s)
    tiles_per_step = 2 if s % (2 * tq) == 0 else 1
    tqs = tq * tiles_per_step
    spb = s // tqs
    assert s % tqs == 0
    vmem = 2 * (3 * s * HEAD_DIM * 2) + (HEAD_DIM + ONES_ROWS) * s * 2 + 6 * BIG_CHUNK * tq * tq * 4
    kv_spec = lambda off: pl.BlockSpec((1, s, HEAD_DIM), lambda bi, h, i: (off + h, bi, 0))
    return pl.pallas_call(
        functools.partial(_attn_kernel, tq=tq),
        grid=(b, n_heads, spb),
        in_specs=[pl.BlockSpec((1, tqs, HEAD_DIM), lambda bi, h, i: (h, bi * spb + i, 0)),
                  kv_spec(n_heads), kv_spec(2 * n_heads),
                  pl.BlockSpec((1, 1, s, HEAD_DIM), lambda bi, h, i: (h, bi, 0, 0))],
        out_specs=pl.BlockSpec((1, tqs, HEAD_DIM), lambda bi, h, i: (bi, i, h)),
        out_shape=jax.ShapeDtypeStruct((b, s, n_heads * HEAD_DIM), BF16),
        scratch_shapes=[pltpu.VMEM((HEAD_DIM + ONES_ROWS, s), BF16),
                        pltpu.VMEM((HEAD_DIM + ONES_ROWS, tq), F32),
                        pltpu.VMEM((1, tq), F32)],
        compiler_params=_cparams(("parallel", "parallel", "arbitrary"), vmem),
        name="fox_attention",
    )(zqkv, zqkv, zqkv, fbias)


def _pool_kernel(u_ref, halo_ref, w_ref, sc_ref, o_ref):
    tm = u_ref.shape[1]
    i = pl.program_id(1)
    halo = jnp.where(i > 0, halo_ref[0], 0.0)
    ext = jnp.concatenate([halo, u_ref[0]], axis=0)
    pos = (i * tm + 1 + lax.broadcasted_iota(jnp.int32, (tm, 1), 0)).astype(F32)
    gd = LANES
    for g, win in enumerate(POOL_WINDOWS):
        e = ext[:, g * gd:(g + 1) * gd]
        ssum = e
        shift = 1
        while shift < win:
            ssum = ssum + pltpu.roll(ssum, shift, 0)
            shift *= 2
        mean = ssum[POOL_HALO:] * (1.0 / jnp.minimum(pos, float(win)))
        dlt = mean - e[POOL_HALO:]
        y = jnp.dot(dlt.astype(BF16), w_ref[0, g], preferred_element_type=F32)
        o_ref[0, :, g * gd:(g + 1) * gd] = (y * sc_ref[:, g * gd:(g + 1) * gd]).astype(o_ref.dtype)


def _pool_mixer(z32, pool_w_stack, layer, scale_row, width):
    b, s, _ = z32.shape
    tm = min(512, s)
    hb = tm // POOL_HALO
    return pl.pallas_call(
        _pool_kernel,
        grid=(b, s // tm),
        in_specs=[pl.BlockSpec((1, tm, width), lambda bi, i: (bi, i, 0)),
                  pl.BlockSpec((1, POOL_HALO, width), lambda bi, i: (bi, jnp.maximum(i * hb - 1, 0), 0)),
                  pl.BlockSpec((1, N_POOL_GROUPS, LANES, LANES), lambda bi, i: (layer, 0, 0, 0)),
                  pl.BlockSpec((1, width), lambda bi, i: (0, 0))],
        out_specs=pl.BlockSpec((1, tm, width), lambda bi, i: (bi, i, 0)),
        out_shape=jax.ShapeDtypeStruct((b, s, width), BF16),
        compiler_params=_cparams(("parallel", "parallel"), 16 * tm * width * 4),
        name="pool_mixer",
    )(z32, z32, pool_w_stack, scale_row)


def _lru_kernel(x_ref, halo_ref, y_ref, cw_ref, cb_ref, wa_ref, ba_ref, wi_ref, bi_ref, lam_ref,
                o_ref, h_ref, a_s, b_s, h_s):
    tm = x_ref.shape[1]
    i = pl.program_id(1)

    @pl.when(i == 0)
    def _():
        h_ref[...] = jnp.zeros_like(h_ref)

    halo = jnp.where(i > 0, halo_ref[0], 0.0)
    ext = jnp.concatenate([halo, x_ref[0]], axis=0)
    xc = cb_ref[...] + cw_ref[LRU_CONV_WIDTH - 1:LRU_CONV_WIDTH, :] * ext[SUBLANES:]
    for k in range(LRU_CONV_WIDTH - 1):
        shifted = pltpu.roll(ext, LRU_CONV_WIDTH - 1 - k, 0)[SUBLANES:]
        xc = xc + cw_ref[k:k + 1, :] * shifted

    lam = lam_ref[...]
    neg_softplus = -(jnp.maximum(-lam, 0.0) + jnp.log(1.0 + jnp.exp(-jnp.abs(lam))))
    gd = LANES
    n_groups = tm // SUBLANES
    sub_row = lax.broadcasted_iota(jnp.int32, (n_groups, SUBLANES, gd), 1)
    for blk in range(N_LRU_BLOCKS):
        sl = slice(blk * gd, (blk + 1) * gd)
        xb = xc[:, sl]
        xb16 = xb.astype(BF16)
        gate_r = _sigmoid(jnp.dot(xb16, wa_ref[0, blk], preferred_element_type=F32) + ba_ref[:, sl])
        gate_i = _sigmoid(jnp.dot(xb16, wi_ref[0, blk], preferred_element_type=F32) + bi_ref[:, sl])
        log_a = LRU_C * gate_r * neg_softplus[:, sl]
        a = jnp.exp(log_a)
        b = jnp.sqrt(1.0 - a * a) * (gate_i * xb)
        a = a.reshape(n_groups, SUBLANES, gd)
        b = b.reshape(n_groups, SUBLANES, gd)
        dist = 1
        while dist < SUBLANES:
            has_prev = sub_row >= dist
            b = b + a * jnp.where(has_prev, pltpu.roll(b, dist, 1), 0.0)
            a = a * jnp.where(has_prev, pltpu.roll(a, dist, 1), 1.0)
            dist *= 2
        a_s[blk] = a
        b_s[blk] = b

    def group(gi, hs):
        out = []
        for blk in range(N_LRU_BLOCKS):
            rows = b_s[blk, gi] + a_s[blk, gi] * hs[blk]
            h_s[blk, gi] = rows
            out.append(rows[SUBLANES - 1:SUBLANES, :])
        return tuple(out)

    hs = lax.fori_loop(0, n_groups, group,
                       tuple(h_ref[:, blk * gd:(blk + 1) * gd] for blk in range(N_LRU_BLOCKS)), unroll=8)

    for blk in range(N_LRU_BLOCKS):
        sl = slice(blk * gd, (blk + 1) * gd)
        h_ref[:, sl] = hs[blk]
        y = y_ref[0, :, sl]
        gelu = 0.5 * y * (1.0 + jnp.tanh(0.7978845608028654 * (y + 0.044715 * (y * y * y))))
        o_ref[0, :, sl] = (h_s[blk].reshape(tm, gd) * gelu).astype(o_ref.dtype)


def _lru_mixer(z32, layer, p, width, x_col, y_col):
    b, s, _ = z32.shape
    tm = min(512, s)
    hb = tm // SUBLANES
    row = lambda bi, i: (0, 0)
    return pl.pallas_call(
        _lru_kernel,
        grid=(b, s // tm),
        in_specs=[pl.BlockSpec((1, tm, width), lambda bi, i: (bi, i, x_col)),
                  pl.BlockSpec((1, SUBLANES, width), lambda bi, i: (bi, jnp.maximum(i * hb - 1, 0), x_col)),
                  pl.BlockSpec((1, tm, width), lambda bi, i: (bi, i, y_col)),
                  pl.BlockSpec((LRU_CONV_WIDTH, width), row),
                  pl.BlockSpec((1, width), row),
                  pl.BlockSpec((1, N_LRU_BLOCKS, LANES, LANES), lambda bi, i: (layer, 0, 0, 0)),
                  pl.BlockSpec((1, width), row),
                  pl.BlockSpec((1, N_LRU_BLOCKS, LANES, LANES), lambda bi, i: (layer, 0, 0, 0)),
                  pl.BlockSpec((1, width), row),
                  pl.BlockSpec((1, width), row)],
        out_specs=pl.BlockSpec((1, tm, width), lambda bi, i: (bi, i, 0)),
        out_shape=jax.ShapeDtypeStruct((b, s, width), BF16),
        scratch_shapes=[pltpu.VMEM((1, width), F32)]
        + [pltpu.VMEM((N_LRU_BLOCKS, tm // SUBLANES, SUBLANES, LANES), F32)] * 3,
        compiler_params=_cparams(("arbitrary", "arbitrary"), 24 * tm * width * 4),
        name="rg_lru",
    )(z32, z32, z32, p["conv_w"], p["conv_b"], p["wa"], p["ba"], p["wi"], p["bi"], p["lam"])


COL_BLOCK = 512


def _col_blocks(d):
    return [slice(c, c + COL_BLOCK) for c in range(0, d, COL_BLOCK)]


def _residual_cols(acc, cols, x_ref, gt_ref, xn_refs):
    xn = x_ref[:, cols] + gt_ref[0, :, cols] * acc
    for ref in xn_refs:
        ref[:, cols] = xn
    return jnp.sum(xn * xn, axis=-1, keepdims=True)


def _norm_rows(sumsq, xn_ref, g_ref, sc_ref, sh_ref, h_ref):
    rs = lax.rsqrt(sumsq * (1.0 / xn_ref.shape[1]) + EPS)
    gain = g_ref[...] * (1.0 + sc_ref[0])
    h_ref[...] = (xn_ref[...] * rs * gain + sh_ref[0]).astype(h_ref.dtype)


def _mix_out_kernel(yp_ref, ya_ref, yl_ref, w_ref, x_ref, gt_ref, g_ref, sc_ref, sh_ref, xo_ref, h_ref, w16_ref):
    kp = yp_ref.shape[1]
    ka = ya_ref.shape[1]

    @pl.when(pl.program_id(0) == 0)
    def _():
        w16_ref[...] = w_ref[0].astype(BF16)

    sumsq = jnp.zeros((x_ref.shape[0], 1), F32)
    for cols in _col_blocks(x_ref.shape[1]):
        acc = jnp.dot(yp_ref[...], w16_ref[0:kp, cols], preferred_element_type=F32)
        acc = acc + jnp.dot(ya_ref[...], w16_ref[kp:kp + ka, cols], preferred_element_type=F32)
        acc = acc + jnp.dot(yl_ref[...], w16_ref[kp + ka:, cols], preferred_element_type=F32)
        sumsq = sumsq + _residual_cols(acc, cols, x_ref, gt_ref, (xo_ref,))
    _norm_rows(sumsq, xo_ref, g_ref, sc_ref, sh_ref, h_ref)


def _mix_out(yp, ya, yl, w_stack, layer, x2, gt, g, sc, sh, seq):
    t, d = x2.shape
    tm = min(512, seq)
    tpb = seq // tm
    per_b = lambda i: (i // tpb, 0, 0)
    lhs = lambda y: pl.BlockSpec((tm, y.shape[1]), lambda i: (i, 0))
    vmem = d * d * 6 + 2 * (tm * d * 2 + 3 * tm * d * 4) + 3 * tm * d * 4
    return pl.pallas_call(
        _mix_out_kernel,
        grid=(t // tm,),
        in_specs=[lhs(yp), lhs(ya), lhs(yl),
                  pl.BlockSpec((1, d, d), lambda i: (layer, 0, 0), pipeline_mode=pl.Buffered(1)),
                  pl.BlockSpec((tm, d), lambda i: (i, 0)),
                  pl.BlockSpec((1, 1, d), per_b),
                  pl.BlockSpec((1, d), lambda i: (0, 0)),
                  pl.BlockSpec((1, 1, d), per_b),
                  pl.BlockSpec((1, 1, d), per_b)],
        out_specs=[pl.BlockSpec((tm, d), lambda i: (i, 0)), pl.BlockSpec((tm, d), lambda i: (i, 0))],
        out_shape=[jax.ShapeDtypeStruct((t, d), F32), jax.ShapeDtypeStruct((t, d), BF16)],
        scratch_shapes=[pltpu.VMEM((d, d), BF16)],
        compiler_params=_cparams(("arbitrary",), vmem),
        name="mix_out_proj",
    )(yp, ya, yl, w_stack, x2, gt, g, sc, sh)


def _ffn_down_kernel(a_ref, w_ref, x_ref, gt_ref, g_ref, sc_ref, sh_ref, *out_refs):
    xn_ref, h_ref = out_refs[0], out_refs[-1]
    sumsq = jnp.zeros((x_ref.shape[0], 1), F32)
    for cols in _col_blocks(x_ref.shape[1]):
        acc = jnp.dot(a_ref[...], w_ref[:, cols], preferred_element_type=F32)
        sumsq = sumsq + _residual_cols(acc, cols, x_ref, gt_ref, (xn_ref,))
    _norm_rows(sumsq, xn_ref, g_ref, sc_ref, sh_ref, h_ref)


def _ffn_down(act, w16, x2, gt, g, sc, sh, seq, emit_x, h_dtype):
    t, d = x2.shape
    kf = act.shape[1]
    tm = min(512, seq)
    tpb = seq // tm
    per_b = lambda i: (i // tpb, 0, 0)
    row_tile = pl.BlockSpec((tm, d), lambda i: (i, 0))
    out_specs = [row_tile]
    out_shape = [jax.ShapeDtypeStruct((t, d), h_dtype)]
    if emit_x:
        out_specs = [row_tile, row_tile]
        out_shape = [jax.ShapeDtypeStruct((t, d), F32)] + out_shape
    else:
        assert h_dtype == F32
    vmem = kf * d * 2 + 2 * (tm * kf * 2 + 3 * tm * d * 4) + 2 * tm * COL_BLOCK * 4
    return pl.pallas_call(
        _ffn_down_kernel,
        grid=(t // tm,),
        in_specs=[pl.BlockSpec((tm, kf), lambda i: (i, 0)),
                  pl.BlockSpec((kf, d), lambda i: (0, 0), pipeline_mode=pl.Buffered(1)),
                  row_tile,
                  pl.BlockSpec((1, 1, d), per_b),
                  pl.BlockSpec((1, d), lambda i: (0, 0)),
                  pl.BlockSpec((1, 1, d), per_b),
                  pl.BlockSpec((1, 1, d), per_b)],
        out_specs=out_specs,
        out_shape=out_shape,
        compiler_params=_cparams(("parallel",), vmem),
        name="ffn_down_proj",
    )(act, w16, x2, gt, g, sc, sh)


def _ffn_gate_kernel(h_ref, wg_ref, wu_ref, cw_ref, cb_ref, wd_ref, o_ref, wd16_ref, carry_ref, *, tiles_per_batch):
    tm = h_ref.shape[0]
    i = pl.program_id(0)
    j = pl.program_id(1)

    @pl.when(i % tiles_per_batch == 0)
    def _():
        carry_ref[j] = jnp.zeros(carry_ref.shape[1:], F32)

    @pl.when(i == 0)
    def _():
        wd16_ref[...] = wd_ref[0].astype(BF16)

    h = h_ref[...]
    u = jnp.dot(h, wg_ref[0].astype(BF16), preferred_element_type=F32)
    ext = jnp.concatenate([carry_ref[j], u], axis=0)
    carry_ref[j] = u[tm - SUBLANES:, :]
    g = cb_ref[...] + cw_ref[FFN_CONV_WIDTH - 1:FFN_CONV_WIDTH, :] * u
    for k in range(FFN_CONV_WIDTH - 1):
        shifted = pltpu.roll(ext, FFN_CONV_WIDTH - 1 - k, 0)[SUBLANES:]
        g = g + cw_ref[k:k + 1, :] * shifted
    up = jnp.dot(h, wu_ref[0].astype(BF16), preferred_element_type=F32)
    o_ref[...] = (g * _sigmoid(g) * up).astype(o_ref.dtype)


def _ffn_gate(h, wg_stack, wu_stack, wd_stack, layer, conv_w, conv_b, seq):
    t, d = h.shape
    kf = wg_stack.shape[2]
    tm = min(1024, seq)
    tf = 512
    nj = kf // tf
    wspec = pl.BlockSpec((1, d, tf), lambda i, j: (layer, 0, j))
    slab = lambda i, j: jnp.where(i == 0, j, nj - 1)
    vmem = 2 * (tm * d * 2 + 2 * d * tf * 4 + tm * tf * 2 + tf * d * 6) + 2 * d * tf * 2 + 8 * tm * tf * 4
    return pl.pallas_call(
        functools.partial(_ffn_gate_kernel, tiles_per_batch=seq // tm),
        grid=(t // tm, nj),
        in_specs=[pl.BlockSpec((tm, d), lambda i, j: (i, 0)), wspec, wspec,
                  pl.BlockSpec((FFN_CONV_WIDTH, tf), lambda i, j: (0, j)),
                  pl.BlockSpec((1, tf), lambda i, j: (0, j)),
                  pl.BlockSpec((1, tf, d), lambda i, j: (layer, slab(i, j), 0))],
        out_specs=[pl.BlockSpec((tm, tf), lambda i, j: (i, j)),
                   pl.BlockSpec((tf, d), lambda i, j: (slab(i, j), 0))],
        out_shape=[jax.ShapeDtypeStruct((t, kf), BF16), jax.ShapeDtypeStruct((kf, d), BF16)],
        scratch_shapes=[pltpu.VMEM((nj, SUBLANES, tf), F32)],
        compiler_params=_cparams(("arbitrary", "arbitrary"), vmem),
        name="ffn_gate_up",
    )(h, wg_stack, wu_stack, conv_w, conv_b, wd_stack)


def kernel(x, c, w_ada, b_ada, g_mix, w_in, b_f, pool_w, pool_scale, lru_conv_w, lru_conv_b, lru_wa, lru_ba,
           lru_wi, lru_bi, lru_lambda, w_out, g_ffn, w_ffn_gate, w_ffn_up, ffn_conv_w, ffn_conv_b, w_ffn_down,
           final_g):
    batch, seq, d = x.shape
    depth = w_ada.shape[0]
    pool_width = pool_w.shape[1] * pool_w.shape[2]
    lru_width = lru_lambda.shape[1]
    n_heads = b_f.shape[1]
    attn_width = n_heads * HEAD_DIM
    assert w_in.shape[2] == pool_width + 3 * attn_width + n_heads + 2 * lru_width
    assert pool_width == lru_width == N_POOL_GROUPS * LANES and seq % SUBLANES == 0

    o_q = pool_width
    o_f = o_q + 3 * attn_width
    o_x = o_f + n_heads
    w_in_t = jnp.swapaxes(w_in, 1, 2)
    qkv_width = 3 * attn_width
    pxy_width = pool_width + 2 * lru_width
    proj_windows = [(0, pool_width), (o_x, 2 * lru_width), (o_f, LANES)]
    assert o_f + LANES <= w_in.shape[2] and o_f % SUBLANES == 0 and o_x % SUBLANES == 0
    bf_pad = jnp.pad(b_f, ((0, 0), (0, LANES - n_heads)))
    qkv_scale = jnp.concatenate([jnp.full((1, attn_width), HEAD_DIM ** -0.5 * LOG2E, F32),
                                 jnp.ones((1, 2 * attn_width), F32)], axis=1)
    pool_w16 = pool_w.astype(BF16)
    lru_wa16 = lru_wa.astype(BF16)
    lru_wi16 = lru_wi.astype(BF16)

    mod = _ada_mod(c, w_ada, b_ada)[:, :batch]

    def mod_chunk(layer, idx):
        return mod[layer, :, idx * d:(idx + 1) * d].reshape(batch, 1, d)

    x2 = x.reshape(batch * seq, d)
    h = x2
    first_norm = (g_mix[0][None], mod_chunk(0, 1), mod_chunk(0, 0), seq)
    out = None
    for layer in range(depth):
        sh1, sc1, gt1, sh2, sc2, gt2 = (mod_chunk(layer, idx) for idx in range(6))
        norm = first_norm if layer == 0 else None
        zqkv = _matmul_heads(h, w_in_t, layer, o_q, qkv_width, qkv_width // 2, qkv_scale, norm)
        z32, zf = _matmul_windows(h, w_in_t, layer, proj_windows, norm)
        z32 = z32.reshape(batch, seq, pxy_width)

        fbias = _forget_bias(zf.reshape(batch, seq, LANES), bf_pad[layer][None], n_heads)
        y_attn = _attention(zqkv, fbias, batch, seq).reshape(batch * seq, attn_width)
        y_pool = _pool_mixer(z32, pool_w16, layer, pool_scale[layer][None], pool_width)
        lru_p = dict(conv_w=lru_conv_w[layer], conv_b=lru_conv_b[layer][None], wa=lru_wa16,
                     ba=lru_ba[layer][None], wi=lru_wi16, bi=lru_bi[layer][None], lam=lru_lambda[layer][None])
        y_lru = _lru_mixer(z32, layer, lru_p, lru_width, 1, 2)

        x2, h = _mix_out(y_pool.reshape(batch * seq, pool_width), y_attn, y_lru.reshape(batch * seq, lru_width),
                         w_out, layer, x2, gt1, g_ffn[layer][None], sc2, sh2, seq)
        act, w_down16 = _ffn_gate(h, w_ffn_gate, w_ffn_up, w_ffn_down, layer, ffn_conv_w[layer],
                                  ffn_conv_b[layer][None], seq)
        if layer + 1 < depth:
            x2, h = _ffn_down(act, w_down16, x2, gt2, g_mix[layer + 1][None], mod_chunk(layer + 1, 1),
                              mod_chunk(layer + 1, 0), seq, True, BF16)
        else:
            zeros = jnp.zeros((batch, 1, d), F32)
            (out,) = _ffn_down(act, w_down16, x2, gt2, final_g[None], zeros, zeros, seq, False, F32)
    return out.reshape(batch, seq, d)
```
